```python
import math
import numpy as np
import jax
import jax.numpy as jnp
from jax import lax

D_MODEL = 4096
BATCH = 8
SEQ = 2048
DEPTH = 2

HEAD_DIM = 128
ROT_DIM = HEAD_DIM // 4
ROPE_THETA = 500000.0

A_HEADS = D_MODEL // (4 * HEAD_DIM)
A_IDX_HEADS = 4
A_IDX_DIM = 128
A_TOPK_MAX = 256
A_QBLOCK = 128
A_IDX_SCALE = (A_IDX_HEADS * A_IDX_DIM) ** -0.5

B_HEADS = D_MODEL // (4 * HEAD_DIM)
B_KV_GROUPS = 2
B_HEADS_PER_GROUP = B_HEADS // B_KV_GROUPS
B_CMP_LEN = 32
B_CMP_STRIDE = 16
B_SEL_LEN = 64
B_SEL_N = 16
B_WIN = 512
B_QBLOCK = 64
B_PHI_HIDDEN = 256

C_WIDTH = D_MODEL // 2
C_GROUP = 16
C_GROUPS = C_WIDTH // C_GROUP
C_STATE = 64

MIX_WIDTH = A_HEADS * HEAD_DIM + B_HEADS * HEAD_DIM + C_WIDTH
IN_SIZES = (
    A_HEADS * HEAD_DIM, HEAD_DIM, HEAD_DIM,
    A_IDX_HEADS * A_IDX_DIM, A_IDX_DIM, A_IDX_HEADS,
    B_HEADS * HEAD_DIM,
    B_KV_GROUPS * HEAD_DIM, B_KV_GROUPS * HEAD_DIM,
    B_KV_GROUPS * HEAD_DIM, B_KV_GROUPS * HEAD_DIM,
    B_KV_GROUPS * HEAD_DIM, B_KV_GROUPS * HEAD_DIM,
    B_HEADS * 3,
    C_WIDTH,
)
N_IN = sum(IN_SIZES)

N_EXPERTS = 32
N_EXPERT_GROUPS = 4
EXPERTS_PER_GROUP = N_EXPERTS // N_EXPERT_GROUPS
TOP_K = 2
D_EXPERT = 896
MOE_BLOCK = 256

ALPHA = (2 * DEPTH) ** 0.25
BETA = (8 * DEPTH) ** -0.25
LN_EPS = 1e-5

kernel_name = 'hybrid_dsa_nsa_s5_grouped_moe_deepnorm'


def _layer_norm(x, g, b):
    xf = x.astype(jnp.float32)
    mu = jnp.mean(xf, axis=-1, keepdims=True)
    var = jnp.mean(jnp.square(xf - mu), axis=-1, keepdims=True)
    return ((xf - mu) * lax.rsqrt(var + LN_EPS) * g + b).astype(x.dtype)


def _rope_tables(positions):
    inv = ROPE_THETA ** (-jnp.arange(0, ROT_DIM, 2, dtype=jnp.float32) / ROT_DIM)
    ang = positions.astype(jnp.float32)[..., None] * inv
    return jnp.cos(ang), jnp.sin(ang)


def _partial_rope(x, cos, sin):
    half = ROT_DIM // 2
    c = cos[:, :, None, :].astype(x.dtype)
    s = sin[:, :, None, :].astype(x.dtype)
    x1 = x[..., :half]
    x2 = x[..., half:ROT_DIM]
    return jnp.concatenate([x1 * c - x2 * s, x2 * c + x1 * s, x[..., ROT_DIM:]], axis=-1)


def _masked_softmax(s, mask):
    s = jnp.where(mask, s.astype(jnp.float32), -jnp.inf)
    m = jnp.max(s, axis=-1, keepdims=True)
    e = jnp.where(mask, jnp.exp(s - jnp.where(jnp.isfinite(m), m, 0.0)), 0.0)
    return e / jnp.maximum(jnp.sum(e, axis=-1, keepdims=True), 1e-30)


def _dsa_mixer(q, k, v, iq, ik, iw):
    bsz, seq = q.shape[:2]
    n_keep = min(A_TOPK_MAX, seq // 4)
    n_qb = seq // A_QBLOCK
    key_pos = jnp.arange(seq)
    gather = jax.vmap(lambda t, i: t[i])

    def block_fn(args):
        qb, iqb, iwb, start = args
        qpos = start + jnp.arange(A_QBLOCK)
        rel = jax.nn.relu(jnp.einsum('bqhd,bsd->bqhs', iqb, ik))
        score = jnp.einsum('bqh,bqhs->bqs', iwb * A_IDX_SCALE, rel).astype(jnp.float32)
        score = jnp.where(key_pos[None, None, :] <= qpos[None, :, None], score, -jnp.inf)
        _, idx = lax.top_k(score, n_keep)
        k_sel = gather(k, idx)
        v_sel = gather(v, idx)
        s = jnp.einsum('bqhd,bqkd->bhqk', qb, k_sel) * HEAD_DIM ** -0.5
        p = _masked_softmax(s, (idx <= qpos[None, :, None])[:, None])
        return jnp.einsum('bhqk,bqkd->bqhd', p.astype(v_sel.dtype), v_sel)

    blocks = lambda t: t.reshape(bsz, n_qb, A_QBLOCK, *t.shape[2:]).swapaxes(0, 1)
    o = lax.map(block_fn, (blocks(q), blocks(iq), blocks(iw), jnp.arange(n_qb) * A_QBLOCK))
    return o.swapaxes(0, 1).reshape(bsz, seq, A_HEADS * HEAD_DIM)


def _nsa_compress(t, pe, w1, w2, cidx):
    blocks = t[:, cidx] + pe[:, None, :]
    blocks = jnp.swapaxes(blocks, 2, 3)
    flat = blocks.reshape(*blocks.shape[:3], B_CMP_LEN * HEAD_DIM)
    return jax.nn.gelu(flat @ w1) @ w2


def _nsa_mixer(q, kc, vc, ks, vs, kw, vw, gates, phi_pe, phi_w1, phi_w2):
    bsz, seq = q.shape[:2]
    G, R, DH = B_KV_GROUPS, B_HEADS_PER_GROUP, HEAD_DIM
    scale = DH ** -0.5
    qg = q.reshape(bsz, seq, G, R, DH)
    tpos = jnp.arange(seq)

    n_cmp = (seq - B_CMP_LEN) // B_CMP_STRIDE + 1
    cidx = np.arange(n_cmp)[:, None] * B_CMP_STRIDE + np.arange(B_CMP_LEN)[None, :]
    k_cmp = _nsa_compress(kc, phi_pe[0], phi_w1[0], phi_w2[0], cidx)
    v_cmp = _nsa_compress(vc, phi_pe[1], phi_w1[1], phi_w2[1], cidx)
    cmp_mask = jnp.asarray(cidx[:, -1])[None, :] <= tpos[:, None]
    p_cmp = _masked_softmax(jnp.einsum('btgrd,bcgd->bgrtc', qg, k_cmp) * scale, cmp_mask)
    o_cmp = jnp.einsum('bgrtc,bcgd->btgrd', p_cmp.astype(v_cmp.dtype), v_cmp)

    n_blk = seq // B_SEL_LEN
    n_sel = min(B_SEL_N, n_blk)
    blk_lo = np.arange(n_blk) * B_SEL_LEN
    overlap = ((cidx[:, :1] < blk_lo[None, :] + B_SEL_LEN) & (cidx[:, -1:] >= blk_lo[None, :])).astype(np.float32)
    imp = jnp.einsum('bgrtc,cn->bgtn', p_cmp, jnp.asarray(overlap))
    blk = jnp.arange(n_blk)[None, :]
    cur = (tpos // B_SEL_LEN)[:, None]
    imp = jnp.where((blk == 0) | (blk == cur) | (blk == cur - 1), jnp.inf,
                    jnp.where(blk > cur, -jnp.inf, imp))
    _, sel_idx = lax.top_k(imp, n_sel)

    ks_blk = ks.reshape(bsz, n_blk, B_SEL_LEN, G, DH).transpose(0, 3, 1, 2, 4)
    vs_blk = vs.reshape(bsz, n_blk, B_SEL_LEN, G, DH).transpose(0, 3, 1, 2, 4)
    zpad = jnp.zeros((bsz, B_WIN, G, DH), kw.dtype)
    kw_pad = jnp.concatenate([zpad, kw], axis=1)
    vw_pad = jnp.concatenate([zpad, vw], axis=1)
    n_qb = seq // B_QBLOCK
    gather = jax.vmap(jax.vmap(lambda t, i: t[i]))
    win_off = jnp.arange(B_WIN + B_QBLOCK) - B_WIN
    in_blk = jnp.arange(B_SEL_LEN)
    n_tok_sel = n_sel * B_SEL_LEN

    def block_fn(args):
        qb, ib, start = args
        qpos = start + jnp.arange(B_QBLOCK)
        k_sel = gather(ks_blk, ib).reshape(bsz, G, B_QBLOCK, n_tok_sel, DH)
        v_sel = gather(vs_blk, ib).reshape(bsz, G, B_QBLOCK, n_tok_sel, DH)
        kpos = (ib[..., None] * B_SEL_LEN + in_blk).reshape(bsz, G, B_QBLOCK, n_tok_sel)
        sel_mask = (kpos <= qpos[None, None, :, None])[:, :, None]
        p_s = _masked_softmax(jnp.einsum('bqgrd,bgqkd->bgrqk', qb, k_sel) * scale, sel_mask)
        o_s = jnp.einsum('bgrqk,bgqkd->bqgrd', p_s.astype(v_sel.dtype), v_sel)
        kw_b = lax.dynamic_slice_in_dim(kw_pad, start, B_WIN + B_QBLOCK, axis=1)
        vw_b = lax.dynamic_slice_in_dim(vw_pad, start, B_WIN + B_QBLOCK, axis=1)
        wpos = start + win_off
        win_mask = ((wpos[None, :] <= qpos[:, None]) & (wpos[None, :] > qpos[:, None] - B_WIN)
                    & (wpos[None, :] >= 0))
        p_w = _masked_softmax(jnp.einsum('bqgrd,bkgd->bgrqk', qb, kw_b) * scale, win_mask)
        o_w = jnp.einsum('bgrqk,bkgd->bqgrd', p_w.astype(vw_b.dtype), vw_b)
        return o_s, o_w

    q_blocks = qg.reshape(bsz, n_qb, B_QBLOCK, G, R, DH).swapaxes(0, 1)
    i_blocks = sel_idx.reshape(bsz, G, n_qb, B_QBLOCK, n_sel).transpose(2, 0, 1, 3, 4)
    o_slc, o_win = lax.map(block_fn, (q_blocks, i_blocks, jnp.arange(n_qb) * B_QBLOCK))
    unblock = lambda o: o.swapaxes(0, 1).reshape(bsz, seq, G, R, DH)
    g = gates.reshape(bsz, seq, G, R, 3)
    out = g[..., 0:1] * o_cmp + g[..., 1:2] * unblock(o_slc) + g[..., 2:3] * unblock(o_win)
    return out.reshape(bsz, seq, B_HEADS * DH)


def _complex_affine_combine(e1, e2):
    ar1, ai1, br1, bi1 = e1
    ar2, ai2, br2, bi2 = e2
    return (ar2 * ar1 - ai2 * ai1, ar2 * ai1 + ai2 * ar1,
            ar2 * br1 - ai2 * bi1 + br2, ar2 * bi1 + ai2 * br1 + bi2)


def _s5_mixer(u, lam_re, lam_im, log_dt, b_re, b_im, c_re, c_im, d, glu_w, glu_b):
    bsz, seq, _ = u.shape
    f32 = jnp.float32
    lr, li = lam_re.astype(f32), lam_im.astype(f32)
    dt = jnp.exp(log_dt.astype(f32))[:, None]
    mag = jnp.exp(lr * dt)
    a_re, a_im = mag * jnp.cos(li * dt), mag * jnp.sin(li * dt)
    den = lr * lr + li * li
    co_re = ((a_re - 1.0) * lr + a_im * li) / den
    co_im = (a_im * lr - (a_re - 1.0) * li) / den
    br, bi = b_re.astype(f32), b_im.astype(f32)
    bb_re = co_re[..., None] * br - co_im[..., None] * bi
    bb_im = co_re[..., None] * bi + co_im[..., None] * br
    uf = u.astype(f32)
    ug = uf.reshape(bsz, seq, C_GROUPS, C_GROUP)
    x_re = jnp.einsum('btgh,gph->btgp', ug, bb_re)
    x_im = jnp.einsum('btgh,gph->btgp', ug, bb_im)
    shape = (1, seq, C_GROUPS, C_STATE)
    _, _, s_re, s_im = lax.associative_scan(
        _complex_affine_combine,
        (jnp.broadcast_to(a_re, shape), jnp.broadcast_to(a_im, shape), x_re, x_im), axis=1)
    y = (jnp.einsum('btgp,ghp->btgh', s_re, c_re.astype(f32))
         - jnp.einsum('btgp,ghp->btgh', s_im, c_im.astype(f32)))
    y = y.reshape(bsz, seq, C_WIDTH) + d.astype(f32) * uf
    z = jax.nn.gelu(y)
    return (z * jax.nn.sigmoid(z @ glu_w.astype(f32) + glu_b.astype(f32))).astype(u.dtype)


def _mixing_sublayer(h, cos, sin, w_in, phi_pe, phi_w1, phi_w2, lam_re, lam_im, log_dt,
                     b_re, b_im, c_re, c_im, s5_d, glu_w, glu_b, w_out):
    bsz, seq, _ = h.shape
    cuts = np.cumsum(IN_SIZES)[:-1].tolist()
    (a_q, a_k, a_v, a_iq, a_ik, a_iw, b_q, b_kc, b_vc, b_ks, b_vs, b_kw, b_vw, b_g,
     c_u) = jnp.split(h @ w_in, cuts, axis=-1)
    heads = lambda t, n: t.reshape(bsz, seq, n, -1)
    rope = lambda t: _partial_rope(t, cos, sin)
    o_a = _dsa_mixer(rope(heads(a_q, A_HEADS)), rope(heads(a_k, 1))[:, :, 0], a_v,
                     rope(heads(a_iq, A_IDX_HEADS)), rope(heads(a_ik, 1))[:, :, 0], a_iw)
    g = B_KV_GROUPS
    o_b = _nsa_mixer(rope(heads(b_q, B_HEADS)),
                     rope(heads(b_kc, g)), heads(b_vc, g),
                     rope(heads(b_ks, g)), heads(b_vs, g),
                     rope(heads(b_kw, g)), heads(b_vw, g),
                     jax.nn.sigmoid(heads(b_g, B_HEADS)), phi_pe, phi_w1, phi_w2)
    o_c = _s5_mixer(c_u, lam_re, lam_im, log_dt, b_re, b_im, c_re, c_im, s5_d, glu_w, glu_b)
    return jnp.concatenate([o_a, o_b, o_c], axis=-1) @ w_out


def _grouped_experts(xt, eid, gate, w_gate, w_up, w_down):
    n_tok, dm = xt.shape
    n_asg = n_tok * TOP_K
    cap = -(-n_asg // MOE_BLOCK) * MOE_BLOCK + N_EXPERTS * MOE_BLOCK
    e_flat = eid.reshape(-1)
    tok_flat = jnp.arange(n_asg, dtype=jnp.int32) // TOP_K
    g_flat = gate.reshape(-1)
    order = jnp.argsort(e_flat)
    e_sorted = e_flat[order]
    counts = jnp.bincount(e_flat, length=N_EXPERTS)
    padded = (counts + MOE_BLOCK - 1) // MOE_BLOCK * MOE_BLOCK
    start = jnp.cumsum(counts) - counts
    pstart = jnp.cumsum(padded) - padded
    pend = pstart + padded
    dest = pstart[e_sorted] + (jnp.arange(n_asg) - start[e_sorted])
    slot_tok = jnp.full((cap,), n_tok, jnp.int32).at[dest].set(tok_flat[order])
    slot_gate = jnp.zeros((cap,), xt.dtype).at[dest].set(g_flat[order].astype(xt.dtype))
    n_blocks = cap // MOE_BLOCK
    blk_start = jnp.arange(n_blocks) * MOE_BLOCK
    blk_exp = jnp.minimum(jnp.sum(pend[None, :] <= blk_start[:, None], axis=1), N_EXPERTS - 1)
    x_pad = jnp.concatenate([xt, jnp.zeros((1, dm), xt.dtype)], axis=0)
    xb = x_pad[slot_tok].reshape(n_blocks, MOE_BLOCK, dm)

    def run(args):
        xblk, e = args
        hid = jax.nn.silu(xblk @ w_gate[e]) * (xblk @ w_up[e])
        return hid @ w_down[e]

    yb = lax.map(run, (xb, blk_exp)).reshape(cap, dm) * slot_gate[:, None]
    return jnp.zeros((n_tok + 1, dm), yb.dtype).at[slot_tok].add(yb)[:n_tok]


def _moe(h, router_w, router_bias, w_gate, w_up, w_down):
    bsz, seq, dm = h.shape
    n_tok = bsz * seq
    xt = h.reshape(n_tok, dm)
    aff = jax.nn.sigmoid((xt @ router_w).astype(jnp.float32))
    sel = (aff + router_bias.astype(jnp.float32)).reshape(n_tok, N_EXPERT_GROUPS, EXPERTS_PER_GROUP)
    grp_score = jnp.sum(lax.top_k(sel, TOP_K)[0], axis=-1)
    _, grp = lax.top_k(grp_score, 1)
    grp_idx = jnp.broadcast_to(grp[:, :, None], (n_tok, 1, EXPERTS_PER_GROUP))
    sel_in = jnp.take_along_axis(sel, grp_idx, axis=1)[:, 0]
    _, loc = lax.top_k(sel_in, TOP_K)
    eid = grp * EXPERTS_PER_GROUP + loc
    w = jnp.take_along_axis(aff, eid, axis=1)
    w = w / jnp.sum(w, axis=-1, keepdims=True)
    return _grouped_experts(xt, eid, w, w_gate, w_up, w_down).reshape(bsz, seq, dm)


def setup_inputs(seed: int = 0) -> dict:
    key = jax.random.key(seed)
    ks = list(jax.random.split(key, 32))
    f32 = jnp.float32
    L = DEPTH

    def nrm(shape, scale):
        return jax.random.normal(ks.pop(), shape, f32) * scale

    x = nrm((BATCH, SEQ, D_MODEL), 1.0)
    positions = (jax.random.randint(ks.pop(), (BATCH, 1), 0, 4096, dtype=jnp.int32)
                 + jnp.arange(SEQ, dtype=jnp.int32)[None, :])
    ln_in_g = 1.0 + nrm((D_MODEL,), 0.02)
    ln_in_b = nrm((D_MODEL,), 0.02)
    w_in = nrm((L, D_MODEL, N_IN), D_MODEL ** -0.5)
    nsa_phi_pe = nrm((L, 2, B_CMP_LEN, HEAD_DIM), 0.1)
    nsa_phi_w1 = nrm((L, 2, B_CMP_LEN * HEAD_DIM, B_PHI_HIDDEN), (B_CMP_LEN * HEAD_DIM) ** -0.5)
    nsa_phi_w2 = nrm((L, 2, B_PHI_HIDDEN, HEAD_DIM), B_PHI_HIDDEN ** -0.5)
    s5_lam_re = -0.5 + nrm((L, C_GROUPS, C_STATE), 0.01)
    s5_lam_im = math.pi * jnp.arange(C_STATE, dtype=f32)[None, None, :] + nrm((L, C_GROUPS, C_STATE), 0.01)
    s5_log_dt = jax.random.uniform(ks.pop(), (L, C_GROUPS), f32, math.log(1e-3), math.log(1e-1))
    s5_b_re = nrm((L, C_GROUPS, C_STATE, C_GROUP), (2 * C_GROUP) ** -0.5)
    s5_b_im = nrm((L, C_GROUPS, C_STATE, C_GROUP), (2 * C_GROUP) ** -0.5)
    s5_c_re = nrm((L, C_GROUPS, C_GROUP, C_STATE), (2 * C_STATE) ** -0.5)
    s5_c_im = nrm((L, C_GROUPS, C_GROUP, C_STATE), (2 * C_STATE) ** -0.5)
    s5_d = nrm((L, C_WIDTH), 0.5)
    s5_glu_w = nrm((L, C_WIDTH, C_WIDTH), C_WIDTH ** -0.5)
    s5_glu_b = nrm((L, C_WIDTH), 0.01)
    w_out = nrm((L, MIX_WIDTH, D_MODEL), BETA * MIX_WIDTH ** -0.5)
    ln_mix_g = 1.0 + nrm((L, D_MODEL), 0.02)
    ln_mix_b = nrm((L, D_MODEL), 0.02)
    router_w = nrm((D_MODEL, N_EXPERTS), D_MODEL ** -0.5)
    router_bias = nrm((N_EXPERTS,), 0.01)
    moe_w_gate = nrm((L, N_EXPERTS, D_MODEL, D_EXPERT), D_MODEL ** -0.5)
    moe_w_up = nrm((L, N_EXPERTS, D_MODEL, D_EXPERT), D_MODEL ** -0.5)
    moe_w_down = nrm((L, N_EXPERTS, D_EXPERT, D_MODEL), BETA * D_EXPERT ** -0.5)
    ln_ffn_g = 1.0 + nrm((L, D_MODEL), 0.02)
    ln_ffn_b = nrm((L, D_MODEL), 0.02)
    return {'x': x, 'positions': positions, 'ln_in_g': ln_in_g, 'ln_in_b': ln_in_b, 'w_in': w_in,
            'nsa_phi_pe': nsa_phi_pe, 'nsa_phi_w1': nsa_phi_w1, 'nsa_phi_w2': nsa_phi_w2,
            's5_lam_re': s5_lam_re, 's5_lam_im': s5_lam_im, 's5_log_dt': s5_log_dt,
            's5_b_re': s5_b_re, 's5_b_im': s5_b_im, 's5_c_re': s5_c_re, 's5_c_im': s5_c_im,
            's5_d': s5_d, 's5_glu_w': s5_glu_w, 's5_glu_b': s5_glu_b, 'w_out': w_out,
            'ln_mix_g': ln_mix_g, 'ln_mix_b': ln_mix_b, 'router_w': router_w, 'router_bias': router_bias,
            'moe_w_gate': moe_w_gate, 'moe_w_up': moe_w_up, 'moe_w_down': moe_w_down,
            'ln_ffn_g': ln_ffn_g, 'ln_ffn_b': ln_ffn_b}


def reference(x, positions, ln_in_g, ln_in_b, w_in, nsa_phi_pe, nsa_phi_w1, nsa_phi_w2,
              s5_lam_re, s5_lam_im, s5_log_dt, s5_b_re, s5_b_im, s5_c_re, s5_c_im, s5_d,
              s5_glu_w, s5_glu_b, w_out, ln_mix_g, ln_mix_b, router_w, router_bias,
              moe_w_gate, moe_w_up, moe_w_down, ln_ffn_g, ln_ffn_b):
    cos, sin = _rope_tables(positions)
    h = _layer_norm(x, ln_in_g, ln_in_b)
    for l in range(DEPTH):
        mix = _mixing_sublayer(h, cos, sin, w_in[l], nsa_phi_pe[l], nsa_phi_w1[l], nsa_phi_w2[l],
                               s5_lam_re[l], s5_lam_im[l], s5_log_dt[l], s5_b_re[l], s5_b_im[l],
                               s5_c_re[l], s5_c_im[l], s5_d[l], s5_glu_w[l], s5_glu_b[l], w_out[l])
        h = _layer_norm(ALPHA * h + mix, ln_mix_g[l], ln_mix_b[l])
        ffn = _moe(h, router_w, router_bias, moe_w_gate[l], moe_w_up[l], moe_w_down[l])
        h = _layer_norm(ALPHA * h + ffn, ln_ffn_g[l], ln_ffn_b[l])
    return h
```

```python
import functools
import math

import numpy as np
import jax
import jax.numpy as jnp
from jax import lax
from jax.experimental import pallas as pl
from jax.experimental.pallas import tpu as pltpu

F32 = jnp.float32
BF16 = jnp.bfloat16

HEAD_DIM = 128
ROT_DIM = HEAD_DIM // 4
ROPE_THETA = 500000.0
A_IDX_HEADS = 4
A_IDX_DIM = 128
A_TOPK_MAX = 256
A_QBLOCK = 128
B_KV_GROUPS = 2
B_CMP_LEN = 32
B_CMP_STRIDE = 16
B_SEL_LEN = 64
B_SEL_N = 16
B_WIN = 512
B_QBLOCK = 64
C_GROUP = 16
C_STATE = 64
N_EXPERTS = 32
N_EXPERT_GROUPS = 4
EXPERTS_PER_GROUP = N_EXPERTS // N_EXPERT_GROUPS
TOP_K = 2
MOE_BLOCK = 256
LN_EPS = 1e-5

VMEM_LIMIT_BYTES = 56 * 1024 * 1024


def _mm_kernel(a_ref, b_ref, o_ref):
    o_ref[...] = jnp.dot(a_ref[...], b_ref[...], preferred_element_type=F32).astype(o_ref.dtype)


def _matmul(a, b, tm, tn, out_dtype=F32):
    m, k = a.shape
    n = b.shape[1]
    assert m % tm == 0 and n % tn == 0
    return pl.pallas_call(
        _mm_kernel,
        grid=(m // tm, n // tn),
        in_specs=[pl.BlockSpec((tm, k), lambda i, j: (i, 0)),
                  pl.BlockSpec((k, tn), lambda i, j: (0, j))],
        out_specs=pl.BlockSpec((tm, tn), lambda i, j: (i, j)),
        out_shape=jax.ShapeDtypeStruct((m, n), out_dtype),
        compiler_params=pltpu.CompilerParams(
            dimension_semantics=("parallel", "parallel"), vmem_limit_bytes=VMEM_LIMIT_BYTES),
        name="dense_matmul",
    )(a, b)


def _moe_up_kernel(blk_exp_ref, n_act_ref, x_ref, wg_ref, wu_ref, hid_ref):
    i = pl.program_id(0)

    @pl.when(i < n_act_ref[0])
    def _():
        x = x_ref[...]
        g = jnp.dot(x, wg_ref[0], preferred_element_type=F32)
        u = jnp.dot(x, wu_ref[0], preferred_element_type=F32)
        hid_ref[...] = (g * jax.nn.sigmoid(g) * u).astype(hid_ref.dtype)

    @pl.when(i >= n_act_ref[0])
    def _():
        hid_ref[...] = jnp.zeros_like(hid_ref)


def _moe_down_kernel(blk_exp_ref, n_act_ref, hid_ref, wd_ref, gate_ref, y_ref):
    i = pl.program_id(0)

    @pl.when(i < n_act_ref[0])
    def _():
        y = jnp.dot(hid_ref[...], wd_ref[0], preferred_element_type=F32)
        y_ref[...] = y * gate_ref[...]

    @pl.when(i >= n_act_ref[0])
    def _():
        y_ref[...] = jnp.zeros_like(y_ref)


def _moe_experts(xs, blk_exp, n_act, slot_gate, wg, wu, wd):
    cap, dm = xs.shape
    de = wg.shape[-1]
    n_blocks = cap // MOE_BLOCK
    wmap = lambda i, be, na: (be[i], 0, 0)
    rmap = lambda i, be, na: (i, 0)
    hid = pl.pallas_call(
        _moe_up_kernel,
        grid_spec=pltpu.PrefetchScalarGridSpec(
            num_scalar_prefetch=2, grid=(n_blocks,),
            in_specs=[pl.BlockSpec((MOE_BLOCK, dm), rmap),
                      pl.BlockSpec((1, dm, de), wmap),
                      pl.BlockSpec((1, dm, de), wmap)],
            out_specs=pl.BlockSpec((MOE_BLOCK, de), rmap)),
        out_shape=jax.ShapeDtypeStruct((cap, de), BF16),
        compiler_params=pltpu.CompilerParams(
            dimension_semantics=("arbitrary",), vmem_limit_bytes=VMEM_LIMIT_BYTES),
        name="moe_gate_up",
    )(blk_exp, n_act, xs, wg, wu)
    return pl.pallas_call(
        _moe_down_kernel,
        grid_spec=pltpu.PrefetchScalarGridSpec(
            num_scalar_prefetch=2, grid=(n_blocks,),
            in_specs=[pl.BlockSpec((MOE_BLOCK, de), rmap),
                      pl.BlockSpec((1, de, dm), wmap),
                      pl.BlockSpec((MOE_BLOCK, 1), rmap)],
            out_specs=pl.BlockSpec((MOE_BLOCK, dm), rmap)),
        out_shape=jax.ShapeDtypeStruct((cap, dm), F32),
        compiler_params=pltpu.CompilerParams(
            dimension_semantics=("arbitrary",), vmem_limit_bytes=VMEM_LIMIT_BYTES),
        name="moe_down",
    )(blk_exp, n_act, hid, wd, slot_gate[:, None])


def _layer_norm(x, g, b):
    mu = jnp.mean(x, axis=-1, keepdims=True)
    var = jnp.mean(jnp.square(x - mu), axis=-1, keepdims=True)
    return (x - mu) * lax.rsqrt(var + LN_EPS) * g + b


def _rope_tables(positions):
    inv = ROPE_THETA ** (-jnp.arange(0, ROT_DIM, 2, dtype=F32) / ROT_DIM)
    ang = positions.astype(F32)[..., None] * inv
    return jnp.cos(ang), jnp.sin(ang)


def _partial_rope(x, cos, sin):
    half = ROT_DIM // 2
    c = cos[:, :, None, :]
    s = sin[:, :, None, :]
    x1 = x[..., :half]
    x2 = x[..., half:ROT_DIM]
    return jnp.concatenate([x1 * c - x2 * s, x2 * c + x1 * s, x[..., ROT_DIM:]], axis=-1)


def _masked_softmax(s, mask):
    s = jnp.where(mask, s.astype(F32), -jnp.inf)
    m = jnp.max(s, axis=-1, keepdims=True)
    e = jnp.where(mask, jnp.exp(s - jnp.where(jnp.isfinite(m), m, 0.0)), 0.0)
    return e / jnp.maximum(jnp.sum(e, axis=-1, keepdims=True), 1e-30)


def _dsa_mixer(q, k, v, iq, ik, iw):
    bsz, seq, n_heads, dh = q.shape
    n_keep = min(A_TOPK_MAX, seq // 4)
    n_qb = seq // A_QBLOCK
    key_pos = jnp.arange(seq)
    gather = jax.vmap(lambda t, i: t[i])
    idx_scale = (A_IDX_HEADS * A_IDX_DIM) ** -0.5

    def block_fn(args):
        qb, iqb, iwb, start = args
        qpos = start + jnp.arange(A_QBLOCK)
        rel = jax.nn.relu(jnp.einsum('bqhd,bsd->bqhs', iqb, ik))
        score = jnp.einsum('bqh,bqhs->bqs', iwb * idx_scale, rel).astype(F32)
        score = jnp.where(key_pos[None, None, :] <= qpos[None, :, None], score, -jnp.inf)
        _, idx = lax.top_k(score, n_keep)
        k_sel = gather(k, idx)
        v_sel = gather(v, idx)
        s = jnp.einsum('bqhd,bqkd->bhqk', qb, k_sel) * dh ** -0.5
        p = _masked_softmax(s, (idx <= qpos[None, :, None])[:, None])
        return jnp.einsum('bhqk,bqkd->bqhd', p, v_sel)

    blocks = lambda t: t.reshape(bsz, n_qb, A_QBLOCK, *t.shape[2:]).swapaxes(0, 1)
    o = lax.map(block_fn, (blocks(q), blocks(iq), blocks(iw), jnp.arange(n_qb) * A_QBLOCK))
    return o.swapaxes(0, 1).reshape(bsz, seq, n_heads * dh)


def _nsa_compress(t, pe, w1, w2, cidx):
    blocks = t[:, cidx] + pe[:, None, :]
    blocks = jnp.swapaxes(blocks, 2, 3)
    flat = blocks.reshape(*blocks.shape[:3], B_CMP_LEN * HEAD_DIM)
    return jax.nn.gelu(flat @ w1) @ w2


def _nsa_mixer(q, kc, vc, ks, vs, kw, vw, gates, phi_pe, phi_w1, phi_w2):
    bsz, seq, n_heads, dh = q.shape
    G = B_KV_GROUPS
    R = n_heads // G
    scale = dh ** -0.5
    qg = q.reshape(bsz, seq, G, R, dh)
    tpos = jnp.arange(seq)

    n_cmp = (seq - B_CMP_LEN) // B_CMP_STRIDE + 1
    cidx = np.arange(n_cmp)[:, None] * B_CMP_STRIDE + np.arange(B_CMP_LEN)[None, :]
    k_cmp = _nsa_compress(kc, phi_pe[0], phi_w1[0], phi_w2[0], cidx)
    v_cmp = _nsa_compress(vc, phi_pe[1], phi_w1[1], phi_w2[1], cidx)
    cmp_mask = jnp.asarray(cidx[:, -1])[None, :] <= tpos[:, None]
    p_cmp = _masked_softmax(jnp.einsum('btgrd,bcgd->bgrtc', qg, k_cmp) * scale, cmp_mask)
    o_cmp = jnp.einsum('bgrtc,bcgd->btgrd', p_cmp, v_cmp)

    n_blk = seq // B_SEL_LEN
    n_sel = min(B_SEL_N, n_blk)
    blk_lo = np.arange(n_blk) * B_SEL_LEN
    overlap = ((cidx[:, :1] < blk_lo[None, :] + B_SEL_LEN) & (cidx[:, -1:] >= blk_lo[None, :])).astype(np.float32)
    imp = jnp.einsum('bgrtc,cn->bgtn', p_cmp, jnp.asarray(overlap))
    blk = jnp.arange(n_blk)[None, :]
    cur = (tpos // B_SEL_LEN)[:, None]
    imp = jnp.where((blk == 0) | (blk == cur) | (blk == cur - 1), jnp.inf,
                    jnp.where(blk > cur, -jnp.inf, imp))
    _, sel_idx = lax.top_k(imp, n_sel)

    ks_blk = ks.reshape(bsz, n_blk, B_SEL_LEN, G, dh).transpose(0, 3, 1, 2, 4)
    vs_blk = vs.reshape(bsz, n_blk, B_SEL_LEN, G, dh).transpose(0, 3, 1, 2, 4)
    zpad = jnp.zeros((bsz, B_WIN, G, dh), kw.dtype)
    kw_pad = jnp.concatenate([zpad, kw], axis=1)
    vw_pad = jnp.concatenate([zpad, vw], axis=1)
    n_qb = seq // B_QBLOCK
    gather = jax.vmap(jax.vmap(lambda t, i: t[i]))
    win_off = jnp.arange(B_WIN + B_QBLOCK) - B_WIN
    in_blk = jnp.arange(B_SEL_LEN)
    n_tok_sel = n_sel * B_SEL_LEN

    def block_fn(args):
        qb, ib, start = args
        qpos = start + jnp.arange(B_QBLOCK)
        k_sel = gather(ks_blk, ib).reshape(bsz, G, B_QBLOCK, n_tok_sel, dh)
        v_sel = gather(vs_blk, ib).reshape(bsz, G, B_QBLOCK, n_tok_sel, dh)
        kpos = (ib[..., None] * B_SEL_LEN + in_blk).reshape(bsz, G, B_QBLOCK, n_tok_sel)
        sel_mask = (kpos <= qpos[None, None, :, None])[:, :, None]
        p_s = _masked_softmax(jnp.einsum('bqgrd,bgqkd->bgrqk', qb, k_sel) * scale, sel_mask)
        o_s = jnp.einsum('bgrqk,bgqkd->bqgrd', p_s, v_sel)
        kw_b = lax.dynamic_slice_in_dim(kw_pad, start, B_WIN + B_QBLOCK, axis=1)
        vw_b = lax.dynamic_slice_in_dim(vw_pad, start, B_WIN + B_QBLOCK, axis=1)
        wpos = start + win_off
        win_mask = ((wpos[None, :] <= qpos[:, None]) & (wpos[None, :] > qpos[:, None] - B_WIN)
                    & (wpos[None, :] >= 0))
        p_w = _masked_softmax(jnp.einsum('bqgrd,bkgd->bgrqk', qb, kw_b) * scale, win_mask)
        o_w = jnp.einsum('bgrqk,bkgd->bqgrd', p_w, vw_b)
        return o_s, o_w

    q_blocks = qg.reshape(bsz, n_qb, B_QBLOCK, G, R, dh).swapaxes(0, 1)
    i_blocks = sel_idx.reshape(bsz, G, n_qb, B_QBLOCK, n_sel).transpose(2, 0, 1, 3, 4)
    o_slc, o_win = lax.map(block_fn, (q_blocks, i_blocks, jnp.arange(n_qb) * B_QBLOCK))
    unblock = lambda o: o.swapaxes(0, 1).reshape(bsz, seq, G, R, dh)
    g = gates.reshape(bsz, seq, G, R, 3)
    out = g[..., 0:1] * o_cmp + g[..., 1:2] * unblock(o_slc) + g[..., 2:3] * unblock(o_win)
    return out.reshape(bsz, seq, n_heads * dh)


def _complex_affine_combine(e1, e2):
    ar1, ai1, br1, bi1 = e1
    ar2, ai2, br2, bi2 = e2
    return (ar2 * ar1 - ai2 * ai1, ar2 * ai1 + ai2 * ar1,
            ar2 * br1 - ai2 * bi1 + br2, ar2 * bi1 + ai2 * br1 + bi2)


def _s5_mixer(u, lam_re, lam_im, log_dt, b_re, b_im, c_re, c_im, d, glu_w_bf16, glu_b):
    bsz, seq, width = u.shape
    n_groups = width // C_GROUP
    lr, li = lam_re, lam_im
    dt = jnp.exp(log_dt)[:, None]
    mag = jnp.exp(lr * dt)
    a_re, a_im = mag * jnp.cos(li * dt), mag * jnp.sin(li * dt)
    den = lr * lr + li * li
    co_re = ((a_re - 1.0) * lr + a_im * li) / den
    co_im = (a_im * lr - (a_re - 1.0) * li) / den
    bb_re = co_re[..., None] * b_re - co_im[..., None] * b_im
    bb_im = co_re[..., None] * b_im + co_im[..., None] * b_re
    ug = u.reshape(bsz, seq, n_groups, C_GROUP)
    x_re = jnp.einsum('btgh,gph->btgp', ug, bb_re)
    x_im = jnp.einsum('btgh,gph->btgp', ug, bb_im)
    shape = (1, seq, n_groups, C_STATE)
    _, _, s_re, s_im = lax.associative_scan(
        _complex_affine_combine,
        (jnp.broadcast_to(a_re, shape), jnp.broadcast_to(a_im, shape), x_re, x_im), axis=1)
    y = (jnp.einsum('btgp,ghp->btgh', s_re, c_re) - jnp.einsum('btgp,ghp->btgh', s_im, c_im))
    y = y.reshape(bsz, seq, width) + d * u
    z = jax.nn.gelu(y)
    zz = _matmul(z.reshape(bsz * seq, width).astype(BF16), glu_w_bf16, 1024, 512).reshape(bsz, seq, width)
    return z * jax.nn.sigmoid(zz + glu_b)


def _mixing_sublayer(h, cos, sin, w_in, in_sizes, phi_pe, phi_w1, phi_w2, lam_re, lam_im, log_dt,
                     b_re, b_im, c_re, c_im, s5_d, glu_w, glu_b, w_out):
    bsz, seq, dm = h.shape
    n_in = w_in.shape[1]
    n_pad = -(-n_in // 512) * 512
    w_in_p = jnp.pad(w_in.astype(BF16), ((0, 0), (0, n_pad - n_in)))
    proj = _matmul(h.reshape(bsz * seq, dm).astype(BF16), w_in_p, 1024, 512)
    proj = proj[:, :n_in].reshape(bsz, seq, n_in)
    cuts = np.cumsum(in_sizes)[:-1].tolist()
    (a_q, a_k, a_v, a_iq, a_ik, a_iw, b_q, b_kc, b_vc, b_ks, b_vs, b_kw, b_vw, b_g,
     c_u) = jnp.split(proj, cuts, axis=-1)
    heads = lambda t, n: t.reshape(bsz, seq, n, -1)
    rope = lambda t: _partial_rope(t, cos, sin)
    a_heads = a_q.shape[-1] // HEAD_DIM
    b_heads = b_q.shape[-1] // HEAD_DIM
    o_a = _dsa_mixer(rope(heads(a_q, a_heads)), rope(heads(a_k, 1))[:, :, 0], a_v,
                     rope(heads(a_iq, A_IDX_HEADS)), rope(heads(a_ik, 1))[:, :, 0], a_iw)
    g = B_KV_GROUPS
    o_b = _nsa_mixer(rope(heads(b_q, b_heads)),
                     rope(heads(b_kc, g)), heads(b_vc, g),
                     rope(heads(b_ks, g)), heads(b_vs, g),
                     rope(heads(b_kw, g)), heads(b_vw, g),
                     jax.nn.sigmoid(heads(b_g, b_heads)), phi_pe, phi_w1, phi_w2)
    o_c = _s5_mixer(c_u, lam_re, lam_im, log_dt, b_re, b_im, c_re, c_im, s5_d,
                    glu_w.astype(BF16), glu_b)
    cat = jnp.concatenate([o_a, o_b, o_c], axis=-1).reshape(bsz * seq, -1).astype(BF16)
    return _matmul(cat, w_out.astype(BF16), 1024, 512).reshape(bsz, seq, dm)


def _moe(h, router_w, router_bias, w_gate, w_up, w_down):
    bsz, seq, dm = h.shape
    n_tok = bsz * seq
    xt = h.reshape(n_tok, dm)
    aff = jax.nn.sigmoid(xt @ router_w)
    sel = (aff + router_bias).reshape(n_tok, N_EXPERT_GROUPS, EXPERTS_PER_GROUP)
    grp_score = jnp.sum(lax.top_k(sel, TOP_K)[0], axis=-1)
    _, grp = lax.top_k(grp_score, 1)
    grp_idx = jnp.broadcast_to(grp[:, :, None], (n_tok, 1, EXPERTS_PER_GROUP))
    sel_in = jnp.take_along_axis(sel, grp_idx, axis=1)[:, 0]
    _, loc = lax.top_k(sel_in, TOP_K)
    eid = grp * EXPERTS_PER_GROUP + loc
    w = jnp.take_along_axis(aff, eid, axis=1)
    w = w / jnp.sum(w, axis=-1, keepdims=True)

    n_asg = n_tok * TOP_K
    cap = -(-n_asg // MOE_BLOCK) * MOE_BLOCK + N_EXPERTS * MOE_BLOCK
    e_flat = eid.reshape(-1)
    tok_flat = jnp.arange(n_asg, dtype=jnp.int32) // TOP_K
    order = jnp.argsort(e_flat)
    e_sorted = e_flat[order]
    counts = jnp.bincount(e_flat, length=N_EXPERTS)
    padded = (counts + MOE_BLOCK - 1) // MOE_BLOCK * MOE_BLOCK
    start = jnp.cumsum(counts) - counts
    pstart = jnp.cumsum(padded) - padded
    pend = pstart + padded
    dest = (pstart[e_sorted] + (jnp.arange(n_asg) - start[e_sorted])).astype(jnp.int32)
    slot_tok = jnp.full((cap,), n_tok, jnp.int32).at[dest].set(tok_flat[order])
    slot_gate = jnp.zeros((cap,), F32).at[dest].set(w.reshape(-1)[order])
    n_blocks = cap // MOE_BLOCK
    blk_start = jnp.arange(n_blocks) * MOE_BLOCK
    blk_exp = jnp.minimum(jnp.sum(pend[None, :] <= blk_start[:, None], axis=1), N_EXPERTS - 1).astype(jnp.int32)
    n_act = (pend[-1] // MOE_BLOCK).astype(jnp.int32).reshape(1)
    x_pad = jnp.concatenate([xt.astype(BF16), jnp.zeros((1, dm), BF16)], axis=0)
    xs = x_pad[slot_tok]
    y = _moe_experts(xs, blk_exp, n_act, slot_gate, w_gate.astype(BF16), w_up.astype(BF16),
                     w_down.astype(BF16))
    pos = jnp.zeros((n_asg,), jnp.int32).at[order].set(dest).reshape(n_tok, TOP_K)
    out = y[pos[:, 0]] + y[pos[:, 1]]
    return out.reshape(bsz, seq, dm)


def kernel(x, positions, ln_in_g, ln_in_b, w_in, nsa_phi_pe, nsa_phi_w1, nsa_phi_w2, s5_lam_re, s5_lam_im, s5_log_dt, s5_b_re, s5_b_im, s5_c_re, s5_c_im, s5_d, s5_glu_w, s5_glu_b, w_out, ln_mix_g, ln_mix_b, router_w, router_bias, moe_w_gate, moe_w_up, moe_w_down, ln_ffn_g, ln_ffn_b):
    depth = w_in.shape[0]
    d_model = x.shape[-1]
    alpha = (2 * depth) ** 0.25
    a_w = d_model // 4
    g_w = B_KV_GROUPS * HEAD_DIM
    in_sizes = (a_w, HEAD_DIM, HEAD_DIM, A_IDX_HEADS * A_IDX_DIM, A_IDX_DIM, A_IDX_HEADS,
                a_w, g_w, g_w, g_w, g_w, g_w, g_w, (a_w // HEAD_DIM) * 3, d_model // 2)
    cos, sin = _rope_tables(positions)
    h = _layer_norm(x, ln_in_g, ln_in_b)
    for l in range(depth):
        mix = _mixing_sublayer(h, cos, sin, w_in[l], in_sizes, nsa_phi_pe[l], nsa_phi_w1[l], nsa_phi_w2[l],
                               s5_lam_re[l], s5_lam_im[l], s5_log_dt[l], s5_b_re[l], s5_b_im[l],
                               s5_c_re[l], s5_c_im[l], s5_d[l], s5_glu_w[l], s5_glu_b[l], w_out[l])
        h = _layer_norm(alpha * h + mix, ln_mix_g[l], ln_mix_b[l])
        ffn = _moe(h, router_w, router_bias, moe_w_gate[l], moe_w_up[l], moe_w_down[l])
        h = _layer_norm(alpha * h + ffn, ln_ffn_g[l], ln_ffn_b[l])
    return h
```

```python
import functools
import math

import numpy as np
import jax
import jax.numpy as jnp
from jax import lax
from jax.experimental import pallas as pl
from jax.experimental.pallas import tpu as pltpu

F32 = jnp.float32
BF16 = jnp.bfloat16

HEAD_DIM = 128
ROT_DIM = HEAD_DIM // 4
ROPE_THETA = 500000.0
A_IDX_HEADS = 4
A_IDX_DIM = 128
A_TOPK_MAX = 256
A_QBLOCK = 128
B_KV_GROUPS = 2
B_CMP_LEN = 32
B_CMP_STRIDE = 16
B_SEL_LEN = 64
B_SEL_N = 16
B_WIN = 512
C_GROUP = 16
C_STATE = 64
N_EXPERTS = 32
N_EXPERT_GROUPS = 4
EXPERTS_PER_GROUP = N_EXPERTS // N_EXPERT_GROUPS
TOP_K = 2
MOE_BLOCK = 256
LN_EPS = 1e-5

VMEM_LIMIT_BYTES = 56 * 1024 * 1024


def _mm_kernel(a_ref, b_ref, o_ref):
    o_ref[...] = jnp.dot(a_ref[...], b_ref[...], preferred_element_type=F32).astype(o_ref.dtype)


def _matmul(a, b, tm, tn, out_dtype=F32):
    m, k = a.shape
    n = b.shape[1]
    assert m % tm == 0 and n % tn == 0
    return pl.pallas_call(
        _mm_kernel,
        grid=(m // tm, n // tn),
        in_specs=[pl.BlockSpec((tm, k), lambda i, j: (i, 0)),
                  pl.BlockSpec((k, tn), lambda i, j: (0, j))],
        out_specs=pl.BlockSpec((tm, tn), lambda i, j: (i, j)),
        out_shape=jax.ShapeDtypeStruct((m, n), out_dtype),
        compiler_params=pltpu.CompilerParams(
            dimension_semantics=("parallel", "parallel"), vmem_limit_bytes=VMEM_LIMIT_BYTES),
        name="dense_matmul",
    )(a, b)


def _moe_up_kernel(blk_exp_ref, n_act_ref, x_ref, wg_ref, wu_ref, hid_ref):
    i = pl.program_id(0)

    @pl.when(i < n_act_ref[0])
    def _():
        x = x_ref[...]
        g = jnp.dot(x, wg_ref[0], preferred_element_type=F32)
        u = jnp.dot(x, wu_ref[0], preferred_element_type=F32)
        hid_ref[...] = (g * jax.nn.sigmoid(g) * u).astype(hid_ref.dtype)

    @pl.when(i >= n_act_ref[0])
    def _():
        hid_ref[...] = jnp.zeros_like(hid_ref)


def _moe_down_kernel(blk_exp_ref, n_act_ref, hid_ref, wd_ref, gate_ref, y_ref):
    i = pl.program_id(0)

    @pl.when(i < n_act_ref[0])
    def _():
        y = jnp.dot(hid_ref[...], wd_ref[0], preferred_element_type=F32)
        y_ref[...] = y * gate_ref[...]

    @pl.when(i >= n_act_ref[0])
    def _():
        y_ref[...] = jnp.zeros_like(y_ref)


def _moe_experts(xs, blk_exp, n_act, slot_gate, wg, wu, wd):
    cap, dm = xs.shape
    de = wg.shape[-1]
    n_blocks = cap // MOE_BLOCK
    wmap = lambda i, be, na: (be[i], 0, 0)
    rmap = lambda i, be, na: (i, 0)
    hid = pl.pallas_call(
        _moe_up_kernel,
        grid_spec=pltpu.PrefetchScalarGridSpec(
            num_scalar_prefetch=2, grid=(n_blocks,),
            in_specs=[pl.BlockSpec((MOE_BLOCK, dm), rmap),
                      pl.BlockSpec((1, dm, de), wmap),
                      pl.BlockSpec((1, dm, de), wmap)],
            out_specs=pl.BlockSpec((MOE_BLOCK, de), rmap)),
        out_shape=jax.ShapeDtypeStruct((cap, de), BF16),
        compiler_params=pltpu.CompilerParams(
            dimension_semantics=("arbitrary",), vmem_limit_bytes=VMEM_LIMIT_BYTES),
        name="moe_gate_up",
    )(blk_exp, n_act, xs, wg, wu)
    return pl.pallas_call(
        _moe_down_kernel,
        grid_spec=pltpu.PrefetchScalarGridSpec(
            num_scalar_prefetch=2, grid=(n_blocks,),
            in_specs=[pl.BlockSpec((MOE_BLOCK, de), rmap),
                      pl.BlockSpec((1, de, dm), wmap),
                      pl.BlockSpec((MOE_BLOCK, 1), rmap)],
            out_specs=pl.BlockSpec((MOE_BLOCK, dm), rmap)),
        out_shape=jax.ShapeDtypeStruct((cap, dm), F32),
        compiler_params=pltpu.CompilerParams(
            dimension_semantics=("arbitrary",), vmem_limit_bytes=VMEM_LIMIT_BYTES),
        name="moe_down",
    )(blk_exp, n_act, hid, wd, slot_gate[:, None])


def _layer_norm(x, g, b):
    mu = jnp.mean(x, axis=-1, keepdims=True)
    var = jnp.mean(jnp.square(x - mu), axis=-1, keepdims=True)
    return (x - mu) * lax.rsqrt(var + LN_EPS) * g + b


def _rope_tables(positions):
    inv = ROPE_THETA ** (-jnp.arange(0, ROT_DIM, 2, dtype=F32) / ROT_DIM)
    ang = positions.astype(F32)[..., None] * inv
    return jnp.cos(ang), jnp.sin(ang)


def _partial_rope(x, cos, sin):
    half = ROT_DIM // 2
    c = cos[:, :, None, :]
    s = sin[:, :, None, :]
    x1 = x[..., :half]
    x2 = x[..., half:ROT_DIM]
    return jnp.concatenate([x1 * c - x2 * s, x2 * c + x1 * s, x[..., ROT_DIM:]], axis=-1)


INT32_MIN = -2 ** 31


def _order_key(x):
    b = lax.bitcast_convert_type(x + 0.0, jnp.int32)
    return b ^ ((b >> 31) & 0x7FFFFFFF)


def _count(mask):
    return jnp.sum(jnp.where(mask, 1.0, 0.0), axis=-1, keepdims=True)


def _topk_mask(key, pos, k, n_pos_bits):
    kf = float(k)
    thr = jnp.where(_count(key >= 0) >= kf, 0, INT32_MIN).astype(jnp.int32)

    def value_step(i, thr):
        cand = thr + jnp.left_shift(jnp.int32(1), 30 - i)
        return jnp.where(_count(key >= cand) >= kf, cand, thr)

    thr = lax.fori_loop(0, 31, value_step, thr)
    above = key > thr
    tied = key == thr
    need = kf - _count(above)

    def pos_step(i, last):
        cand = last + jnp.left_shift(jnp.int32(1), n_pos_bits - 1 - i)
        return jnp.where(_count(tied & (pos < cand)) < need, cand, last)

    last = lax.fori_loop(0, n_pos_bits, pos_step, jnp.zeros_like(thr))
    return above | (tied & (pos <= last))


def _softmax_weights(s, mask):
    s = jnp.where(mask, s, -jnp.inf)
    m = jnp.max(s, axis=-1, keepdims=True)
    m = jnp.where(jnp.isfinite(m), m, 0.0)
    e = jnp.where(mask, jnp.exp(s - m), 0.0)
    return e, jnp.maximum(jnp.sum(e, axis=-1, keepdims=True), 1e-30)


def _dot_nt(a, b):
    return lax.dot_general(a, b, (((1,), (1,)), ((), ())), preferred_element_type=F32)


def _attend(q, k, v, mask, scale):
    e, denom = _softmax_weights(_dot_nt(q, k) * scale, mask)
    return jnp.dot(e.astype(BF16), v, preferred_element_type=F32) / denom


def _dsa_kernel(q_ref, iq_ref, iw_ref, k_ref, v_ref, ik_ref, o_ref, *, n_keep, n_heads, pos_bits):
    seq = k_ref.shape[1]
    start = pl.program_id(1) * A_QBLOCK
    qpos = start + lax.broadcasted_iota(jnp.int32, (A_QBLOCK, 1), 0)
    kpos = lax.broadcasted_iota(jnp.int32, (1, seq), 1)
    causal = kpos <= qpos
    ik = ik_ref[0]
    iw = iw_ref[0] * (A_IDX_HEADS * A_IDX_DIM) ** -0.5
    score = jnp.zeros((A_QBLOCK, seq), F32)
    for h in range(A_IDX_HEADS):
        rel = _dot_nt(iq_ref[0, :, h * A_IDX_DIM:(h + 1) * A_IDX_DIM], ik)
        score = score + iw[:, h:h + 1] * jnp.maximum(rel, 0.0)
    score = jnp.where(causal, score, -jnp.inf)
    sel = _topk_mask(_order_key(score), kpos, n_keep, pos_bits) & causal
    k = k_ref[0]
    v = v_ref[0]
    for h in range(n_heads):
        hs = slice(h * HEAD_DIM, (h + 1) * HEAD_DIM)
        o_ref[0, :, hs] = _attend(q_ref[0, :, hs], k, v, sel, HEAD_DIM ** -0.5).astype(o_ref.dtype)


def _dsa_mixer(q, k, v, iq, ik, iw):
    bsz, seq, qw = q.shape
    n_keep = min(A_TOPK_MAX, seq // 4)
    pos_bits = max(1, (seq - 1).bit_length())
    qmap = lambda b, i: (b, i, 0)
    kmap = lambda b, i: (b, 0, 0)
    return pl.pallas_call(
        functools.partial(_dsa_kernel, n_keep=n_keep, n_heads=qw // HEAD_DIM, pos_bits=pos_bits),
        grid=(bsz, seq // A_QBLOCK),
        in_specs=[pl.BlockSpec((1, A_QBLOCK, qw), qmap),
                  pl.BlockSpec((1, A_QBLOCK, iq.shape[-1]), qmap),
                  pl.BlockSpec((1, A_QBLOCK, iw.shape[-1]), qmap),
                  pl.BlockSpec((1, seq, HEAD_DIM), kmap),
                  pl.BlockSpec((1, seq, HEAD_DIM), kmap),
                  pl.BlockSpec((1, seq, A_IDX_DIM), kmap)],
        out_specs=pl.BlockSpec((1, A_QBLOCK, qw), qmap),
        out_shape=jax.ShapeDtypeStruct((bsz, seq, qw), BF16),
        compiler_params=pltpu.CompilerParams(
            dimension_semantics=("parallel", "arbitrary"), vmem_limit_bytes=VMEM_LIMIT_BYTES),
        name="dsa_mixer",
    )(q, iq, iw, k, v, ik)


NSA_QBLOCK = 128
GELU_C = math.sqrt(2.0 / math.pi)


def _gelu_tanh(x):
    return 0.5 * x * (1.0 + jnp.tanh(GELU_C * (x + 0.044715 * (x * x * x))))


def _nsa_compress_kernel(kc_ref, vc_ref, pe_ref, w1_ref, w2_ref, ko_ref, vo_ref):
    n_chunk = ko_ref.shape[2]
    half = B_CMP_LEN // 2
    assert B_CMP_STRIDE == half
    row = lax.broadcasted_iota(jnp.int32, (n_chunk, 1), 0)
    for t, (src, dst) in enumerate(((kc_ref, ko_ref), (vc_ref, vo_ref))):
        lo = jnp.zeros((n_chunk, w1_ref.shape[-1]), F32)
        hi = jnp.zeros((n_chunk, w1_ref.shape[-1]), F32)
        for l in range(half):
            x = src[0, pl.ds(l, n_chunk, stride=B_CMP_STRIDE), :]
            lo = lo + jnp.dot((x + pe_ref[t, l:l + 1, :]).astype(BF16),
                              w1_ref[t, l * HEAD_DIM:(l + 1) * HEAD_DIM, :], preferred_element_type=F32)
            hi = hi + jnp.dot((x + pe_ref[t, half + l:half + l + 1, :]).astype(BF16),
                              w1_ref[t, (half + l) * HEAD_DIM:(half + l + 1) * HEAD_DIM, :],
                              preferred_element_type=F32)
        pre = lo + pltpu.roll(hi, n_chunk - 1, 0)
        out = jnp.dot(_gelu_tanh(pre).astype(BF16), w2_ref[t], preferred_element_type=F32)
        dst[0, 0] = jnp.where(row < n_chunk - 1, out, 0.0).astype(dst.dtype)


def _nsa_compress(kc, vc, phi_pe, phi_w1, phi_w2):
    bsz, seq, gw = kc.shape
    n_groups = gw // HEAD_DIM
    n_chunk = seq // B_CMP_STRIDE
    tmap = lambda b, g: (b, 0, g)
    cmap = lambda b, g: (0, 0, 0)
    omap = lambda b, g: (b, g, 0, 0)
    out_sds = jax.ShapeDtypeStruct((bsz, n_groups, n_chunk, HEAD_DIM), BF16)
    return pl.pallas_call(
        _nsa_compress_kernel,
        grid=(bsz, n_groups),
        in_specs=[pl.BlockSpec((1, seq, HEAD_DIM), tmap),
                  pl.BlockSpec((1, seq, HEAD_DIM), tmap),
                  pl.BlockSpec(phi_pe.shape, cmap),
                  pl.BlockSpec(phi_w1.shape, cmap),
                  pl.BlockSpec(phi_w2.shape, cmap)],
        out_specs=[pl.BlockSpec((1, 1, n_chunk, HEAD_DIM), omap),
                   pl.BlockSpec((1, 1, n_chunk, HEAD_DIM), omap)],
        out_shape=[out_sds, out_sds],
        compiler_params=pltpu.CompilerParams(
            dimension_semantics=("parallel", "parallel"), vmem_limit_bytes=VMEM_LIMIT_BYTES),
        name="nsa_compress",
    )(kc, vc, phi_pe, phi_w1.astype(BF16), phi_w2.astype(BF16))


def _nsa_kernel(q_ref, gate_ref, kcmp_ref, vcmp_ref, ks_ref, vs_ref, kw_ref, vw_ref, ovl_ref, expand_ref,
                o_ref, *, n_sel, heads_per_group):
    seq = ks_ref.shape[1]
    n_cmp = kcmp_ref.shape[2]
    n_blk = seq // B_SEL_LEN
    scale = HEAD_DIM ** -0.5
    start = pl.program_id(2) * NSA_QBLOCK
    qpos = start + lax.broadcasted_iota(jnp.int32, (NSA_QBLOCK, 1), 0)
    heads = [q_ref[0, :, r * HEAD_DIM:(r + 1) * HEAD_DIM] for r in range(heads_per_group)]

    cend = lax.broadcasted_iota(jnp.int32, (1, n_cmp), 1) * B_CMP_STRIDE + (B_CMP_LEN - 1)
    cmask = cend <= qpos
    kcmp = kcmp_ref[0, 0]
    vcmp = vcmp_ref[0, 0]
    o_cmp = []
    psum = jnp.zeros((NSA_QBLOCK, n_cmp), F32)
    for q in heads:
        e, denom = _softmax_weights(_dot_nt(q, kcmp) * scale, cmask)
        p = e / denom
        psum = psum + p
        o_cmp.append(jnp.dot(p.astype(BF16), vcmp, preferred_element_type=F32))
    p_hi = psum.astype(BF16)
    p_lo = (psum - p_hi.astype(F32)).astype(BF16)
    ovl = ovl_ref[...]
    imp = (jnp.dot(p_hi, ovl, preferred_element_type=F32) + jnp.dot(p_lo, ovl, preferred_element_type=F32))
    blk = lax.broadcasted_iota(jnp.int32, (1, imp.shape[1]), 1)
    cur = qpos // B_SEL_LEN
    imp = jnp.where((blk == 0) | (blk == cur) | (blk == cur - 1), jnp.inf,
                    jnp.where(blk > cur, -jnp.inf, imp))
    key = jnp.where(blk < n_blk, _order_key(imp), INT32_MIN)
    blk_sel = _topk_mask(key, blk, n_sel, max(1, (n_blk - 1).bit_length()))

    kpos = lax.broadcasted_iota(jnp.int32, (1, seq), 1)
    in_sel = jnp.dot(jnp.where(blk_sel, 1.0, 0.0).astype(BF16), expand_ref[...], preferred_element_type=F32)
    sel_mask = (in_sel > 0.5) & (kpos <= qpos)
    ks = ks_ref[0]
    vs = vs_ref[0]
    o_sel = [_attend(q, ks, vs, sel_mask, scale) for q in heads]

    n_win = B_WIN + NSA_QBLOCK
    kstart = pl.multiple_of(jnp.maximum(start - B_WIN, 0), NSA_QBLOCK)
    wpos = kstart + lax.broadcasted_iota(jnp.int32, (1, n_win), 1)
    win_mask = (wpos <= qpos) & (wpos > qpos - B_WIN)
    kw = kw_ref[0, pl.ds(kstart, n_win), :]
    vw = vw_ref[0, pl.ds(kstart, n_win), :]
    o_win = [_attend(q, kw, vw, win_mask, scale) for q in heads]

    gates = jax.nn.sigmoid(gate_ref[0, 0])
    for r in range(heads_per_group):
        gr = gates[:, 3 * r:3 * r + 3]
        out = gr[:, 0:1] * o_cmp[r] + gr[:, 1:2] * o_sel[r] + gr[:, 2:3] * o_win[r]
        o_ref[0, :, r * HEAD_DIM:(r + 1) * HEAD_DIM] = out.astype(o_ref.dtype)


def _nsa_mixer(q, kc, vc, ks, vs, kw, vw, gate_logits, phi_pe, phi_w1, phi_w2):
    bsz, seq, qw = q.shape
    n_groups = kc.shape[-1] // HEAD_DIM
    hpg = qw // HEAD_DIM // n_groups
    assert seq % NSA_QBLOCK == 0 and seq >= B_WIN + NSA_QBLOCK and seq % B_SEL_LEN == 0
    kcmp, vcmp = _nsa_compress(kc, vc, phi_pe, phi_w1, phi_w2)
    n_cmp = seq // B_CMP_STRIDE
    n_blk = seq // B_SEL_LEN
    n_sel = min(B_SEL_N, n_blk)
    lanes = 128
    assert n_blk <= lanes
    c_lo = np.arange(n_cmp)[:, None] * B_CMP_STRIDE
    b_lo = np.arange(lanes)[None, :] * B_SEL_LEN
    overlap = ((c_lo < b_lo + B_SEL_LEN) & (c_lo + B_CMP_LEN - 1 >= b_lo) & (np.arange(lanes)[None, :] < n_blk))
    expand = (np.arange(seq)[None, :] // B_SEL_LEN) == np.arange(lanes)[:, None]
    qmap = lambda b, g, i: (b, i, g)
    kmap = lambda b, g, i: (b, 0, g)
    cmap = lambda b, g, i: (b, g, 0, 0)
    const = lambda b, g, i: (0, 0)
    return pl.pallas_call(
        functools.partial(_nsa_kernel, n_sel=n_sel, heads_per_group=hpg),
        grid=(bsz, n_groups, seq // NSA_QBLOCK),
        in_specs=[pl.BlockSpec((1, NSA_QBLOCK, hpg * HEAD_DIM), qmap),
                  pl.BlockSpec((1, 1, NSA_QBLOCK, 3 * hpg), lambda b, g, i: (b, g, i, 0)),
                  pl.BlockSpec((1, 1, n_cmp, HEAD_DIM), cmap),
                  pl.BlockSpec((1, 1, n_cmp, HEAD_DIM), cmap),
                  pl.BlockSpec((1, seq, HEAD_DIM), kmap),
                  pl.BlockSpec((1, seq, HEAD_DIM), kmap),
                  pl.BlockSpec((1, seq, HEAD_DIM), kmap),
                  pl.BlockSpec((1, seq, HEAD_DIM), kmap),
                  pl.BlockSpec((n_cmp, lanes), const),
                  pl.BlockSpec((lanes, seq), const)],
        out_specs=pl.BlockSpec((1, NSA_QBLOCK, hpg * HEAD_DIM), qmap),
        out_shape=jax.ShapeDtypeStruct((bsz, seq, qw), BF16),
        compiler_params=pltpu.CompilerParams(
            dimension_semantics=("parallel", "parallel", "arbitrary"), vmem_limit_bytes=VMEM_LIMIT_BYTES),
        name="nsa_mixer",
    )(q, gate_logits, kcmp, vcmp, ks, vs, kw, vw,
      jnp.asarray(overlap, BF16), jnp.asarray(expand, BF16))


def _complex_affine_combine(e1, e2):
    ar1, ai1, br1, bi1 = e1
    ar2, ai2, br2, bi2 = e2
    return (ar2 * ar1 - ai2 * ai1, ar2 * ai1 + ai2 * ar1,
            ar2 * br1 - ai2 * bi1 + br2, ar2 * bi1 + ai2 * br1 + bi2)


def _s5_mixer(u, lam_re, lam_im, log_dt, b_re, b_im, c_re, c_im, d, glu_w_bf16, glu_b):
    bsz, seq, width = u.shape
    n_groups = width // C_GROUP
    lr, li = lam_re, lam_im
    dt = jnp.exp(log_dt)[:, None]
    mag = jnp.exp(lr * dt)
    a_re, a_im = mag * jnp.cos(li * dt), mag * jnp.sin(li * dt)
    den = lr * lr + li * li
    co_re = ((a_re - 1.0) * lr + a_im * li) / den
    co_im = (a_im * lr - (a_re - 1.0) * li) / den
    bb_re = co_re[..., None] * b_re - co_im[..., None] * b_im
    bb_im = co_re[..., None] * b_im + co_im[..., None] * b_re
    ug = u.reshape(bsz, seq, n_groups, C_GROUP)
    x_re = jnp.einsum('btgh,gph->btgp', ug, bb_re)
    x_im = jnp.einsum('btgh,gph->btgp', ug, bb_im)
    shape = (1, seq, n_groups, C_STATE)
    _, _, s_re, s_im = lax.associative_scan(
        _complex_affine_combine,
        (jnp.broadcast_to(a_re, shape), jnp.broadcast_to(a_im, shape), x_re, x_im), axis=1)
    y = (jnp.einsum('btgp,ghp->btgh', s_re, c_re) - jnp.einsum('btgp,ghp->btgh', s_im, c_im))
    y = y.reshape(bsz, seq, width) + d * u
    z = jax.nn.gelu(y)
    zz = _matmul(z.reshape(bsz * seq, width).astype(BF16), glu_w_bf16, 1024, 512).reshape(bsz, seq, width)
    return z * jax.nn.sigmoid(zz + glu_b)


def _mixing_sublayer(h, cos, sin, w_in, in_sizes, phi_pe, phi_w1, phi_w2, lam_re, lam_im, log_dt,
                     b_re, b_im, c_re, c_im, s5_d, glu_w, glu_b, w_out):
    bsz, seq, dm = h.shape
    n_in = w_in.shape[1]
    n_pad = -(-n_in // 512) * 512
    w_in_p = jnp.pad(w_in.astype(BF16), ((0, 0), (0, n_pad - n_in)))
    proj = _matmul(h.reshape(bsz * seq, dm).astype(BF16), w_in_p, 1024, 512)
    proj = proj[:, :n_in].reshape(bsz, seq, n_in)
    cuts = np.cumsum(in_sizes)[:-1].tolist()
    (a_q, a_k, a_v, a_iq, a_ik, a_iw, b_q, b_kc, b_vc, b_ks, b_vs, b_kw, b_vw, b_g,
     c_u) = jnp.split(proj, cuts, axis=-1)
    rope = lambda t, n: _partial_rope(t.reshape(bsz, seq, n, -1), cos, sin).reshape(bsz, seq, -1)
    a_heads = a_q.shape[-1] // HEAD_DIM
    b_heads = b_q.shape[-1] // HEAD_DIM
    g = B_KV_GROUPS
    o_a = _dsa_mixer(rope(a_q, a_heads).astype(BF16), rope(a_k, 1).astype(BF16), a_v.astype(BF16),
                     rope(a_iq, A_IDX_HEADS).astype(BF16), rope(a_ik, 1).astype(BF16), a_iw)
    gate_logits = b_g.reshape(bsz, seq, g, -1).transpose(0, 2, 1, 3)
    o_b = _nsa_mixer(rope(b_q, b_heads).astype(BF16), rope(b_kc, g), b_vc,
                     rope(b_ks, g).astype(BF16), b_vs.astype(BF16),
                     rope(b_kw, g).astype(BF16), b_vw.astype(BF16),
                     gate_logits, phi_pe, phi_w1, phi_w2)
    o_c = _s5_mixer(c_u, lam_re, lam_im, log_dt, b_re, b_im, c_re, c_im, s5_d,
                    glu_w.astype(BF16), glu_b)
    cat = jnp.concatenate([o_a, o_b, o_c.astype(BF16)], axis=-1).reshape(bsz * seq, -1)
    return _matmul(cat, w_out.astype(BF16), 1024, 512).reshape(bsz, seq, dm)


def _moe(h, router_w, router_bias, w_gate, w_up, w_down):
    bsz, seq, dm = h.shape
    n_tok = bsz * seq
    xt = h.reshape(n_tok, dm)
    aff = jax.nn.sigmoid(xt @ router_w)
    sel = (aff + router_bias).reshape(n_tok, N_EXPERT_GROUPS, EXPERTS_PER_GROUP)
    grp_score = jnp.sum(lax.top_k(sel, TOP_K)[0], axis=-1)
    _, grp = lax.top_k(grp_score, 1)
    grp_idx = jnp.broadcast_to(grp[:, :, None], (n_tok, 1, EXPERTS_PER_GROUP))
    sel_in = jnp.take_along_axis(sel, grp_idx, axis=1)[:, 0]
    _, loc = lax.top_k(sel_in, TOP_K)
    eid = grp * EXPERTS_PER_GROUP + loc
    w = jnp.take_along_axis(aff, eid, axis=1)
    w = w / jnp.sum(w, axis=-1, keepdims=True)

    n_asg = n_tok * TOP_K
    cap = -(-n_asg // MOE_BLOCK) * MOE_BLOCK + N_EXPERTS * MOE_BLOCK
    e_flat = eid.reshape(-1)
    tok_flat = jnp.arange(n_asg, dtype=jnp.int32) // TOP_K
    order = jnp.argsort(e_flat)
    e_sorted = e_flat[order]
    counts = jnp.bincount(e_flat, length=N_EXPERTS)
    padded = (counts + MOE_BLOCK - 1) // MOE_BLOCK * MOE_BLOCK
    start = jnp.cumsum(counts) - counts
    pstart = jnp.cumsum(padded) - padded
    pend = pstart + padded
    dest = (pstart[e_sorted] + (jnp.arange(n_asg) - start[e_sorted])).astype(jnp.int32)
    slot_tok = jnp.full((cap,), n_tok, jnp.int32).at[dest].set(tok_flat[order])
    slot_gate = jnp.zeros((cap,), F32).at[dest].set(w.reshape(-1)[order])
    n_blocks = cap // MOE_BLOCK
    blk_start = jnp.arange(n_blocks) * MOE_BLOCK
    blk_exp = jnp.minimum(jnp.sum(pend[None, :] <= blk_start[:, None], axis=1), N_EXPERTS - 1).astype(jnp.int32)
    n_act = (pend[-1] // MOE_BLOCK).astype(jnp.int32).reshape(1)
    x_pad = jnp.concatenate([xt.astype(BF16), jnp.zeros((1, dm), BF16)], axis=0)
    xs = x_pad[slot_tok]
    y = _moe_experts(xs, blk_exp, n_act, slot_gate, w_gate.astype(BF16), w_up.astype(BF16),
                     w_down.astype(BF16))
    pos = jnp.zeros((n_asg,), jnp.int32).at[order].set(dest).reshape(n_tok, TOP_K)
    out = y[pos[:, 0]] + y[pos[:, 1]]
    return out.reshape(bsz, seq, dm)


def kernel(x, positions, ln_in_g, ln_in_b, w_in, nsa_phi_pe, nsa_phi_w1, nsa_phi_w2, s5_lam_re, s5_lam_im, s5_log_dt, s5_b_re, s5_b_im, s5_c_re, s5_c_im, s5_d, s5_glu_w, s5_glu_b, w_out, ln_mix_g, ln_mix_b, router_w, router_bias, moe_w_gate, moe_w_up, moe_w_down, ln_ffn_g, ln_ffn_b):
    depth = w_in.shape[0]
    d_model = x.shape[-1]
    alpha = (2 * depth) ** 0.25
    a_w = d_model // 4
    g_w = B_KV_GROUPS * HEAD_DIM
    in_sizes = (a_w, HEAD_DIM, HEAD_DIM, A_IDX_HEADS * A_IDX_DIM, A_IDX_DIM, A_IDX_HEADS,
                a_w, g_w, g_w, g_w, g_w, g_w, g_w, (a_w // HEAD_DIM) * 3, d_model // 2)
    cos, sin = _rope_tables(positions)
    h = _layer_norm(x, ln_in_g, ln_in_b)
    for l in range(depth):
        mix = _mixing_sublayer(h, cos, sin, w_in[l], in_sizes, nsa_phi_pe[l], nsa_phi_w1[l], nsa_phi_w2[l],
                               s5_lam_re[l], s5_lam_im[l], s5_log_dt[l], s5_b_re[l], s5_b_im[l],
                               s5_c_re[l], s5_c_im[l], s5_d[l], s5_glu_w[l], s5_glu_b[l], w_out[l])
        h = _layer_norm(alpha * h + mix, ln_mix_g[l], ln_mix_b[l])
        ffn = _moe(h, router_w, router_bias, moe_w_gate[l], moe_w_up[l], moe_w_down[l])
        h = _layer_norm(alpha * h + ffn, ln_ffn_g[l], ln_ffn_b[l])
    return h
```

```python
import functools
import math

import numpy as np
import jax
import jax.numpy as jnp
from jax import lax
from jax.experimental import pallas as pl
from jax.experimental.pallas import tpu as pltpu

F32 = jnp.float32
BF16 = jnp.bfloat16

LANES = 128
HEAD_DIM = 128
ROT_DIM = HEAD_DIM // 4
ROPE_THETA = 500000.0
A_IDX_HEADS = 4
A_IDX_DIM = 128
A_TOPK_MAX = 256
A_QBLOCK = 128
B_KV_GROUPS = 2
B_CMP_LEN = 32
B_CMP_STRIDE = 16
B_SEL_LEN = 64
B_SEL_N = 16
B_WIN = 512
C_GROUP = 16
C_STATE = 64
N_EXPERTS = 32
N_EXPERT_GROUPS = 4
EXPERTS_PER_GROUP = N_EXPERTS // N_EXPERT_GROUPS
TOP_K = 2
MOE_BLOCK = 256
LN_EPS = 1e-5

VMEM_LIMIT_BYTES = 56 * 1024 * 1024
ROW_TILE = 1024


def _params(*semantics):
    return pltpu.CompilerParams(dimension_semantics=semantics, vmem_limit_bytes=VMEM_LIMIT_BYTES)


def _mm_kernel(a_ref, b_ref, o_ref):
    o_ref[...] = jnp.dot(a_ref[...], b_ref[...], preferred_element_type=F32).astype(o_ref.dtype)


def _mm_rope_kernel(a_ref, b_ref, c_ref, sa_ref, sb_ref, o_ref):
    y = jnp.dot(a_ref[...], b_ref[...], preferred_element_type=F32)
    half = ROT_DIM // 2
    for hb in range(y.shape[1] // HEAD_DIM):
        cols = slice(hb * HEAD_DIM, (hb + 1) * HEAD_DIM)
        yh = y[:, cols]
        out = (yh * c_ref[...] + pltpu.roll(yh, half, 1) * sa_ref[...]
               + pltpu.roll(yh, HEAD_DIM - half, 1) * sb_ref[...])
        o_ref[:, cols] = out.astype(o_ref.dtype)


def _matmul(a, b, tn, out_dtype, rope=None):
    m, k = a.shape
    n = b.shape[1]
    tm = min(ROW_TILE, m)
    assert m % tm == 0 and n % tn == 0
    in_specs = [pl.BlockSpec((tm, k), lambda i, j: (i, 0)),
                pl.BlockSpec((k, tn), lambda i, j: (0, j))]
    args = (a, b)
    if rope is not None:
        in_specs += [pl.BlockSpec((tm, HEAD_DIM), lambda i, j: (i, 0))] * 3
        args += tuple(rope)
    return pl.pallas_call(
        _mm_kernel if rope is None else _mm_rope_kernel,
        grid=(m // tm, n // tn),
        in_specs=in_specs,
        out_specs=pl.BlockSpec((tm, tn), lambda i, j: (i, j)),
        out_shape=jax.ShapeDtypeStruct((m, n), out_dtype),
        compiler_params=_params("parallel", "parallel"),
        name="in_proj" if rope is None else "in_proj_rope",
    )(*args)


def _out_proj_kernel(a0_ref, a1_ref, a2_ref, w_ref, o_ref):
    k0 = a0_ref.shape[1]
    k1 = k0 + a1_ref.shape[1]
    acc = jnp.dot(a0_ref[...], w_ref[:k0, :], preferred_element_type=F32)
    acc = acc + jnp.dot(a1_ref[...], w_ref[k0:k1, :], preferred_element_type=F32)
    o_ref[...] = acc + jnp.dot(a2_ref[...], w_ref[k1:, :], preferred_element_type=F32)


def _out_proj(a0, a1, a2, w, tn=512):
    m = a0.shape[0]
    k, n = w.shape
    tm = min(ROW_TILE, m)
    assert a0.shape[1] + a1.shape[1] + a2.shape[1] == k and m % tm == 0 and n % tn == 0
    amap = lambda i, j: (i, 0)
    return pl.pallas_call(
        _out_proj_kernel,
        grid=(m // tm, n // tn),
        in_specs=[pl.BlockSpec((tm, a0.shape[1]), amap), pl.BlockSpec((tm, a1.shape[1]), amap),
                  pl.BlockSpec((tm, a2.shape[1]), amap), pl.BlockSpec((k, tn), lambda i, j: (0, j))],
        out_specs=pl.BlockSpec((tm, tn), lambda i, j: (i, j)),
        out_shape=jax.ShapeDtypeStruct((m, n), F32),
        compiler_params=_params("parallel", "parallel"),
        name="out_proj",
    )(a0, a1, a2, w)


LN_ROWS = 256


def _ln_store(v, g_ref, b_ref, o_ref, ob_ref):
    mu = jnp.mean(v, axis=-1, keepdims=True)
    c = v - mu
    var = jnp.mean(c * c, axis=-1, keepdims=True)
    out = c * lax.rsqrt(var + LN_EPS) * g_ref[...] + b_ref[...]
    o_ref[...] = out
    ob_ref[...] = out.astype(BF16)


def _ln_kernel(x_ref, g_ref, b_ref, o_ref, ob_ref):
    _ln_store(x_ref[...], g_ref, b_ref, o_ref, ob_ref)


def _ln_residual_kernel(x_ref, r_ref, g_ref, b_ref, o_ref, ob_ref, *, alpha):
    _ln_store(alpha * x_ref[...] + r_ref[...], g_ref, b_ref, o_ref, ob_ref)


def _layer_norm(x, g, b, residual=None, alpha=1.0):
    m, d = x.shape
    rows = pl.BlockSpec((LN_ROWS, d), lambda i: (i, 0))
    vec = pl.BlockSpec((1, d), lambda i: (0, 0))
    assert m % LN_ROWS == 0
    if residual is None:
        body, ins, args = _ln_kernel, [rows, vec, vec], (x, g.reshape(1, d), b.reshape(1, d))
    else:
        body = functools.partial(_ln_residual_kernel, alpha=alpha)
        ins, args = [rows, rows, vec, vec], (x, residual, g.reshape(1, d), b.reshape(1, d))
    return pl.pallas_call(
        body, grid=(m // LN_ROWS,), in_specs=ins, out_specs=[rows, rows],
        out_shape=[jax.ShapeDtypeStruct((m, d), F32), jax.ShapeDtypeStruct((m, d), BF16)],
        compiler_params=_params("parallel"), name="layer_norm",
    )(*args)


def _moe_up_kernel(blk_exp_ref, n_act_ref, x_ref, wg_ref, wu_ref, hid_ref):
    i = pl.program_id(0)

    @pl.when(i < n_act_ref[0])
    def _():
        x = x_ref[...]
        g = jnp.dot(x, wg_ref[0], preferred_element_type=F32)
        u = jnp.dot(x, wu_ref[0], preferred_element_type=F32)
        hid_ref[...] = (g * jax.nn.sigmoid(g) * u).astype(hid_ref.dtype)

    @pl.when(i >= n_act_ref[0])
    def _():
        hid_ref[...] = jnp.zeros_like(hid_ref)


def _moe_down_kernel(blk_exp_ref, n_act_ref, hid_ref, wd_ref, gate_ref, y_ref):
    i = pl.program_id(0)

    @pl.when(i < n_act_ref[0])
    def _():
        y = jnp.dot(hid_ref[...], wd_ref[0], preferred_element_type=F32)
        y_ref[...] = y * gate_ref[...]

    @pl.when(i >= n_act_ref[0])
    def _():
        y_ref[...] = jnp.zeros_like(y_ref)


def _moe_experts(xs, blk_exp, n_act, slot_gate, wg, wu, wd):
    cap, dm = xs.shape
    de = wg.shape[-1]
    n_blocks = cap // MOE_BLOCK
    wmap = lambda i, be, na: (be[i], 0, 0)
    rmap = lambda i, be, na: (i, 0)
    hid = pl.pallas_call(
        _moe_up_kernel,
        grid_spec=pltpu.PrefetchScalarGridSpec(
            num_scalar_prefetch=2, grid=(n_blocks,),
            in_specs=[pl.BlockSpec((MOE_BLOCK, dm), rmap),
                      pl.BlockSpec((1, dm, de), wmap),
                      pl.BlockSpec((1, dm, de), wmap)],
            out_specs=pl.BlockSpec((MOE_BLOCK, de), rmap)),
        out_shape=jax.ShapeDtypeStruct((cap, de), BF16),
        compiler_params=_params("arbitrary"),
        name="moe_gate_up",
    )(blk_exp, n_act, xs, wg, wu)
    return pl.pallas_call(
        _moe_down_kernel,
        grid_spec=pltpu.PrefetchScalarGridSpec(
            num_scalar_prefetch=2, grid=(n_blocks,),
            in_specs=[pl.BlockSpec((MOE_BLOCK, de), rmap),
                      pl.BlockSpec((1, de, dm), wmap),
                      pl.BlockSpec((MOE_BLOCK, 1), rmap)],
            out_specs=pl.BlockSpec((MOE_BLOCK, dm), rmap)),
        out_shape=jax.ShapeDtypeStruct((cap, dm), F32),
        compiler_params=_params("arbitrary"),
        name="moe_down",
    )(blk_exp, n_act, hid, wd, slot_gate[:, None])


INT32_MIN = -2 ** 31


def _order_key(x):
    b = lax.bitcast_convert_type(x + 0.0, jnp.int32)
    return b ^ ((b >> 31) & 0x7FFFFFFF)


def _count(mask):
    return jnp.sum(jnp.where(mask, 1.0, 0.0), axis=-1, keepdims=True)


def _topk_mask(key, pos, k, n_pos_bits):
    kf = float(k)
    thr = jnp.where(_count(key >= 0) >= kf, 0, INT32_MIN).astype(jnp.int32)

    def value_step(i, thr):
        cand = thr + jnp.left_shift(jnp.int32(1), 30 - i)
        return jnp.where(_count(key >= cand) >= kf, cand, thr)

    thr = lax.fori_loop(0, 31, value_step, thr)
    above = key > thr
    tied = key == thr
    need = kf - _count(above)

    def pos_step(i, last):
        cand = last + jnp.left_shift(jnp.int32(1), n_pos_bits - 1 - i)
        return jnp.where(_count(tied & (pos < cand)) < need, cand, last)

    last = lax.fori_loop(0, n_pos_bits, pos_step, jnp.zeros_like(thr))
    return above | (tied & (pos <= last))


def _softmax_weights(s, mask):
    s = jnp.where(mask, s, -jnp.inf)
    m = jnp.max(s, axis=-1, keepdims=True)
    m = jnp.where(jnp.isfinite(m), m, 0.0)
    e = jnp.where(mask, jnp.exp(s - m), 0.0)
    return e, jnp.maximum(jnp.sum(e, axis=-1, keepdims=True), 1e-30)


def _dot_nt(a, b):
    return lax.dot_general(a, b, (((1,), (1,)), ((), ())), preferred_element_type=F32)


def _attend(q, k, v, mask, scale):
    e, denom = _softmax_weights(_dot_nt(q, k) * scale, mask)
    return jnp.dot(e.astype(BF16), v, preferred_element_type=F32) / denom


class _Packed:
    def __init__(self, d_model):
        self.qw = d_model // 4
        self.gw = B_KV_GROUPS * HEAD_DIM
        self.iqw = A_IDX_HEADS * A_IDX_DIM
        self.cw = d_model // 2
        self.n_bheads = self.qw // HEAD_DIM
        self.gate_cols = 3 * self.n_bheads // B_KV_GROUPS
        qw, gw = self.qw, self.gw
        self.in_sizes = (qw, HEAD_DIM, HEAD_DIM, self.iqw, A_IDX_DIM, A_IDX_HEADS, qw,
                         gw, gw, gw, gw, gw, gw, 3 * self.n_bheads, self.cw)
        self.ra_tile = 512
        o = 0
        self.a_q = o; o += qw
        self.b_q = o; o += qw
        self.a_iq = o; o += self.iqw
        self.b_ks = o; o += gw
        self.b_kw = o; o += gw
        self.a_k = o; o += HEAD_DIM
        self.a_ik = o; o += A_IDX_DIM
        self.ra_width = -(-o // self.ra_tile) * self.ra_tile
        self.b_vs, self.b_vw, self.a_v = 0, gw, 2 * gw
        self.pc_width = 2 * gw + HEAD_DIM
        self.c_u = 0
        self.b_vc = self.cw
        self.a_iw = self.b_vc + gw
        self.b_g = self.a_iw + LANES
        self.pd_width = self.b_g + B_KV_GROUPS * LANES
        self.pd_tile = next(t for t in (896, 768, 640, 512, 384, 256, 128) if self.pd_width % t == 0)
        assert qw % 512 == 0 and self.iqw % 512 == 0 and self.a_iq % 512 == 0

    def weights(self, w_in):
        cuts = np.cumsum(self.in_sizes)[:-1].tolist()
        (a_q, a_k, a_v, a_iq, a_ik, a_iw, b_q, b_kc, b_vc, b_ks, b_vs, b_kw, b_vw, b_g,
         c_u) = jnp.split(w_in.astype(BF16), cuts, axis=1)
        zeros = lambda n: jnp.zeros((w_in.shape[0], n), BF16)
        ra = [a_q, b_q, a_iq, b_ks, b_kw, a_k, a_ik]
        ra.append(zeros(self.ra_width - sum(t.shape[1] for t in ra)))
        pd = [c_u, b_vc, a_iw, zeros(LANES - a_iw.shape[1])]
        for g in range(B_KV_GROUPS):
            pd += [b_g[:, g * self.gate_cols:(g + 1) * self.gate_cols], zeros(LANES - self.gate_cols)]
        cat = lambda ts: jnp.concatenate(ts, axis=1)
        return cat(ra), b_kc, cat([b_vs, b_vw, a_v]), cat(pd)


def _rope_tables(positions):
    half = ROT_DIM // 2
    inv = ROPE_THETA ** (-jnp.arange(0, ROT_DIM, 2, dtype=F32) / ROT_DIM)
    ang = positions.astype(F32).reshape(-1, 1) * inv
    cos, sin = jnp.cos(ang), jnp.sin(ang)
    n = ang.shape[0]
    ones, zeros, z16 = jnp.ones((n, HEAD_DIM - ROT_DIM), F32), jnp.zeros((n, HEAD_DIM - ROT_DIM), F32), jnp.zeros((n, half), F32)
    return (jnp.concatenate([cos, cos, ones], axis=1),
            jnp.concatenate([z16, sin, zeros], axis=1),
            jnp.concatenate([-sin, z16, zeros], axis=1))


def _dsa_kernel(q_ref, iq_ref, iw_ref, k_ref, v_ref, ik_ref, o_ref, *, n_keep, n_heads, pos_bits):
    seq = k_ref.shape[1]
    start = pl.program_id(1) * A_QBLOCK
    qpos = start + lax.broadcasted_iota(jnp.int32, (A_QBLOCK, 1), 0)
    kpos = lax.broadcasted_iota(jnp.int32, (1, seq), 1)
    causal = kpos <= qpos
    ik = ik_ref[0]
    iw = iw_ref[0] * (A_IDX_HEADS * A_IDX_DIM) ** -0.5
    score = jnp.zeros((A_QBLOCK, seq), F32)
    for h in range(A_IDX_HEADS):
        rel = _dot_nt(iq_ref[0, :, h * A_IDX_DIM:(h + 1) * A_IDX_DIM], ik)
        score = score + iw[:, h:h + 1] * jnp.maximum(rel, 0.0)
    score = jnp.where(causal, score, -jnp.inf)
    sel = _topk_mask(_order_key(score), kpos, n_keep, pos_bits) & causal
    k = k_ref[0]
    v = v_ref[0]
    for h in range(n_heads):
        hs = slice(h * HEAD_DIM, (h + 1) * HEAD_DIM)
        o_ref[0, :, hs] = _attend(q_ref[0, :, hs], k, v, sel, HEAD_DIM ** -0.5).astype(o_ref.dtype)


def _dsa_mixer(ra, pc, pd, lay):
    bsz, seq, _ = ra.shape
    n_keep = min(A_TOPK_MAX, seq // 4)
    pos_bits = max(1, (seq - 1).bit_length())
    qb = lambda width, off: pl.BlockSpec((1, A_QBLOCK, width), lambda b, i: (b, i, off // width))
    kb = lambda width, off: pl.BlockSpec((1, seq, width), lambda b, i: (b, 0, off // width))
    return pl.pallas_call(
        functools.partial(_dsa_kernel, n_keep=n_keep, n_heads=lay.qw // HEAD_DIM, pos_bits=pos_bits),
        grid=(bsz, seq // A_QBLOCK),
        in_specs=[qb(lay.qw, lay.a_q), qb(lay.iqw, lay.a_iq), qb(LANES, lay.a_iw),
                  kb(HEAD_DIM, lay.a_k), kb(HEAD_DIM, lay.a_v), kb(A_IDX_DIM, lay.a_ik)],
        out_specs=pl.BlockSpec((1, A_QBLOCK, lay.qw), lambda b, i: (b, i, 0)),
        out_shape=jax.ShapeDtypeStruct((bsz, seq, lay.qw), BF16),
        compiler_params=_params("parallel", "arbitrary"),
        name="dsa_mixer",
    )(ra, ra, pd, ra, pc, ra)


NSA_QBLOCK = 128
GELU_C = math.sqrt(2.0 / math.pi)


def _gelu_tanh(x):
    return 0.5 * x * (1.0 + jnp.tanh(GELU_C * (x + 0.044715 * (x * x * x))))


def _nsa_compress_kernel(kc_ref, vc_ref, pe_ref, w1_ref, w2_ref, ko_ref, vo_ref):
    n_chunk = ko_ref.shape[2]
    half = B_CMP_LEN // 2
    assert B_CMP_STRIDE == half
    row = lax.broadcasted_iota(jnp.int32, (n_chunk, 1), 0)
    for t, (src, dst) in enumerate(((kc_ref, ko_ref), (vc_ref, vo_ref))):
        lo = jnp.zeros((n_chunk, w1_ref.shape[-1]), F32)
        hi = jnp.zeros((n_chunk, w1_ref.shape[-1]), F32)
        for l in range(half):
            x = src[0, pl.ds(l, n_chunk, stride=B_CMP_STRIDE), :]
            lo = lo + jnp.dot((x + pe_ref[t, l:l + 1, :]).astype(BF16),
                              w1_ref[t, l * HEAD_DIM:(l + 1) * HEAD_DIM, :], preferred_element_type=F32)
            hi = hi + jnp.dot((x + pe_ref[t, half + l:half + l + 1, :]).astype(BF16),
                              w1_ref[t, (half + l) * HEAD_DIM:(half + l + 1) * HEAD_DIM, :],
                              preferred_element_type=F32)
        pre = lo + pltpu.roll(hi, n_chunk - 1, 0)
        out = jnp.dot(_gelu_tanh(pre).astype(BF16), w2_ref[t], preferred_element_type=F32)
        dst[0, 0] = jnp.where(row < n_chunk - 1, out, 0.0).astype(dst.dtype)


def _nsa_compress(rb, pd, lay, phi_pe, phi_w1, phi_w2):
    bsz, seq, _ = rb.shape
    n_chunk = seq // B_CMP_STRIDE
    cmap = lambda b, g: (0, 0, 0)
    omap = lambda b, g: (b, g, 0, 0)
    vc_blk = lay.b_vc // HEAD_DIM
    out_sds = jax.ShapeDtypeStruct((bsz, B_KV_GROUPS, n_chunk, HEAD_DIM), BF16)
    return pl.pallas_call(
        _nsa_compress_kernel,
        grid=(bsz, B_KV_GROUPS),
        in_specs=[pl.BlockSpec((1, seq, HEAD_DIM), lambda b, g: (b, 0, g)),
                  pl.BlockSpec((1, seq, HEAD_DIM), lambda b, g: (b, 0, vc_blk + g)),
                  pl.BlockSpec(phi_pe.shape, cmap),
                  pl.BlockSpec(phi_w1.shape, cmap),
                  pl.BlockSpec(phi_w2.shape, cmap)],
        out_specs=[pl.BlockSpec((1, 1, n_chunk, HEAD_DIM), omap),
                   pl.BlockSpec((1, 1, n_chunk, HEAD_DIM), omap)],
        out_shape=[out_sds, out_sds],
        compiler_params=_params("parallel", "parallel"),
        name="nsa_compress",
    )(rb, pd, phi_pe, phi_w1.astype(BF16), phi_w2.astype(BF16))


def _nsa_kernel(q_ref, gate_ref, kcmp_ref, vcmp_ref, ks_ref, vs_ref, kw_ref, vw_ref, ovl_ref, expand_ref,
                o_ref, *, n_sel, heads_per_group):
    seq = ks_ref.shape[1]
    n_cmp = kcmp_ref.shape[2]
    n_blk = seq // B_SEL_LEN
    scale = HEAD_DIM ** -0.5
    start = pl.program_id(2) * NSA_QBLOCK
    qpos = start + lax.broadcasted_iota(jnp.int32, (NSA_QBLOCK, 1), 0)
    heads = [q_ref[0, :, r * HEAD_DIM:(r + 1) * HEAD_DIM] for r in range(heads_per_group)]

    cend = lax.broadcasted_iota(jnp.int32, (1, n_cmp), 1) * B_CMP_STRIDE + (B_CMP_LEN - 1)
    cmask = cend <= qpos
    kcmp = kcmp_ref[0, 0]
    vcmp = vcmp_ref[0, 0]
    o_cmp = []
    psum = jnp.zeros((NSA_QBLOCK, n_cmp), F32)
    for q in heads:
        e, denom = _softmax_weights(_dot_nt(q, kcmp) * scale, cmask)
        p = e / denom
        psum = psum + p
        o_cmp.append(jnp.dot(p.astype(BF16), vcmp, preferred_element_type=F32))
    p_hi = psum.astype(BF16)
    p_lo = (psum - p_hi.astype(F32)).astype(BF16)
    ovl = ovl_ref[...]
    imp = (jnp.dot(p_hi, ovl, preferred_element_type=F32) + jnp.dot(p_lo, ovl, preferred_element_type=F32))
    blk = lax.broadcasted_iota(jnp.int32, (1, imp.shape[1]), 1)
    cur = qpos // B_SEL_LEN
    imp = jnp.where((blk == 0) | (blk == cur) | (blk == cur - 1), jnp.inf,
                    jnp.where(blk > cur, -jnp.inf, imp))
    key = jnp.where(blk < n_blk, _order_key(imp), INT32_MIN)
    blk_sel = _topk_mask(key, blk, n_sel, max(1, (n_blk - 1).bit_length()))

    kpos = lax.broadcasted_iota(jnp.int32, (1, seq), 1)
    in_sel = jnp.dot(jnp.where(blk_sel, 1.0, 0.0).astype(BF16), expand_ref[...], preferred_element_type=F32)
    sel_mask = (in_sel > 0.5) & (kpos <= qpos)
    ks = ks_ref[0]
    vs = vs_ref[0]
    o_sel = [_attend(q, ks, vs, sel_mask, scale) for q in heads]

    n_win = B_WIN + NSA_QBLOCK
    kstart = pl.multiple_of(jnp.maximum(start - B_WIN, 0), NSA_QBLOCK)
    wpos = kstart + lax.broadcasted_iota(jnp.int32, (1, n_win), 1)
    win_mask = (wpos <= qpos) & (wpos > qpos - B_WIN)
    kw = kw_ref[0, pl.ds(kstart, n_win), :]
    vw = vw_ref[0, pl.ds(kstart, n_win), :]
    o_win = [_attend(q, kw, vw, win_mask, scale) for q in heads]

    gates = jax.nn.sigmoid(gate_ref[0])
    for r in range(heads_per_group):
        gr = gates[:, 3 * r:3 * r + 3]
        out = gr[:, 0:1] * o_cmp[r] + gr[:, 1:2] * o_sel[r] + gr[:, 2:3] * o_win[r]
        o_ref[0, :, r * HEAD_DIM:(r + 1) * HEAD_DIM] = out.astype(o_ref.dtype)


def _nsa_mixer(ra, rb, pc, pd, lay, phi_pe, phi_w1, phi_w2):
    bsz, seq, _ = ra.shape
    hpg = lay.n_bheads // B_KV_GROUPS
    gq = hpg * HEAD_DIM
    assert seq % NSA_QBLOCK == 0 and seq >= B_WIN + NSA_QBLOCK and seq % B_SEL_LEN == 0
    kcmp, vcmp = _nsa_compress(rb, pd, lay, phi_pe, phi_w1, phi_w2)
    n_cmp = seq // B_CMP_STRIDE
    n_blk = seq // B_SEL_LEN
    n_sel = min(B_SEL_N, n_blk)
    assert n_blk <= LANES
    c_lo = np.arange(n_cmp)[:, None] * B_CMP_STRIDE
    b_lo = np.arange(LANES)[None, :] * B_SEL_LEN
    overlap = ((c_lo < b_lo + B_SEL_LEN) & (c_lo + B_CMP_LEN - 1 >= b_lo) & (np.arange(LANES)[None, :] < n_blk))
    expand = (np.arange(seq)[None, :] // B_SEL_LEN) == np.arange(LANES)[:, None]
    qb = lambda width, off: pl.BlockSpec((1, NSA_QBLOCK, width), lambda b, g, i: (b, i, off // width + g))
    kb = lambda off: pl.BlockSpec((1, seq, HEAD_DIM), lambda b, g, i: (b, 0, off // HEAD_DIM + g))
    cmap = lambda b, g, i: (b, g, 0, 0)
    const = lambda b, g, i: (0, 0)
    return pl.pallas_call(
        functools.partial(_nsa_kernel, n_sel=n_sel, heads_per_group=hpg),
        grid=(bsz, B_KV_GROUPS, seq // NSA_QBLOCK),
        in_specs=[qb(gq, lay.b_q), qb(LANES, lay.b_g),
                  pl.BlockSpec((1, 1, n_cmp, HEAD_DIM), cmap),
                  pl.BlockSpec((1, 1, n_cmp, HEAD_DIM), cmap),
                  kb(lay.b_ks), kb(lay.b_vs), kb(lay.b_kw), kb(lay.b_vw),
                  pl.BlockSpec((n_cmp, LANES), const),
                  pl.BlockSpec((LANES, seq), const)],
        out_specs=pl.BlockSpec((1, NSA_QBLOCK, gq), lambda b, g, i: (b, i, g)),
        out_shape=jax.ShapeDtypeStruct((bsz, seq, lay.qw), BF16),
        compiler_params=_params("parallel", "parallel", "arbitrary"),
        name="nsa_mixer",
    )(ra, pd, kcmp, vcmp, ra, pc, ra, pc, jnp.asarray(overlap, BF16), jnp.asarray(expand, BF16))


S5_TIME_BLOCK = 256
S5_UNROLL = 8


def _s5_scan_kernel(u_ref, bcat_ref, ccat_ref, a_ref, d_ref, z_ref, xs_ref, state_ref):
    tt, bsz, lanes = u_ref.shape
    ns = a_ref.shape[-1] // 2

    @pl.when(pl.program_id(1) == 0)
    def _():
        state_ref[...] = jnp.zeros_like(state_ref)

    u = u_ref[...].reshape(tt * bsz, lanes)
    xs_ref[...] = jnp.dot(u.astype(BF16), bcat_ref[0], preferred_element_type=F32)
    a_re = jnp.broadcast_to(a_ref[0, :, :ns], (bsz, ns))
    a_im = jnp.broadcast_to(a_ref[0, :, ns:], (bsz, ns))

    def step(t, carry):
        s_re, s_im = carry
        rows = pl.ds(pl.multiple_of(t * bsz, bsz), bsz)
        n_re = a_re * s_re - a_im * s_im + xs_ref[rows, :ns]
        n_im = a_re * s_im + a_im * s_re + xs_ref[rows, ns:]
        xs_ref[rows, :ns] = n_re
        xs_ref[rows, ns:] = n_im
        return n_re, n_im

    s_re, s_im = lax.fori_loop(0, tt, step, (state_ref[:, :ns], state_ref[:, ns:]), unroll=S5_UNROLL)
    state_ref[:, :ns] = s_re
    state_ref[:, ns:] = s_im
    y = jnp.dot(xs_ref[...].astype(BF16), ccat_ref[0], preferred_element_type=F32) + d_ref[0] * u
    z_ref[...] = _gelu_tanh(y).reshape(tt, bsz, lanes)


def _s5_glu_kernel(za_ref, zt_ref, w_ref, b_ref, o_ref):
    zz = jnp.dot(za_ref[...].astype(BF16), w_ref[...], preferred_element_type=F32) + b_ref[...]
    o_ref[...] = (zt_ref[...] * jax.nn.sigmoid(zz)).astype(o_ref.dtype)


def _s5_mixer(u_tm, lam_re, lam_im, log_dt, b_re, b_im, c_re, c_im, d, glu_w, glu_b):
    seq, bsz, width = u_tm.shape
    gpc = LANES // C_GROUP
    n_col = width // LANES
    ns = gpc * C_STATE
    tt = min(S5_TIME_BLOCK, seq)
    assert width % LANES == 0 and seq % tt == 0 and bsz % 8 == 0
    dt = jnp.exp(log_dt)[:, None]
    mag = jnp.exp(lam_re * dt)
    a_re, a_im = mag * jnp.cos(lam_im * dt), mag * jnp.sin(lam_im * dt)
    den = lam_re * lam_re + lam_im * lam_im
    co_re = ((a_re - 1.0) * lam_re + a_im * lam_im) / den
    co_im = (a_im * lam_re - (a_re - 1.0) * lam_im) / den
    bb_re = co_re[..., None] * b_re - co_im[..., None] * b_im
    bb_im = co_re[..., None] * b_im + co_im[..., None] * b_re
    eye = jnp.eye(gpc, dtype=F32)

    def block_diag(m):
        r, c = m.shape[-2:]
        return jnp.einsum('ngrc,gk->ngrkc', m, eye).reshape(n_col, gpc * r, gpc * c)

    to_in = lambda m: block_diag(m.reshape(n_col, gpc, C_STATE, C_GROUP).transpose(0, 1, 3, 2))
    to_out = lambda m: block_diag(m.reshape(n_col, gpc, C_GROUP, C_STATE).transpose(0, 1, 3, 2))
    bcat = jnp.concatenate([to_in(bb_re), to_in(bb_im)], axis=-1).astype(BF16)
    ccat = jnp.concatenate([to_out(c_re), to_out(-c_im)], axis=-2).astype(BF16)
    acat = jnp.concatenate([a_re.reshape(n_col, 1, ns), a_im.reshape(n_col, 1, ns)], axis=-1)
    z = pl.pallas_call(
        _s5_scan_kernel,
        grid=(n_col, seq // tt),
        in_specs=[pl.BlockSpec((tt, bsz, LANES), lambda c, t: (t, 0, c)),
                  pl.BlockSpec((1, LANES, 2 * ns), lambda c, t: (c, 0, 0)),
                  pl.BlockSpec((1, 2 * ns, LANES), lambda c, t: (c, 0, 0)),
                  pl.BlockSpec((1, 1, 2 * ns), lambda c, t: (c, 0, 0)),
                  pl.BlockSpec((1, 1, LANES), lambda c, t: (c, 0, 0))],
        out_specs=pl.BlockSpec((tt, bsz, LANES), lambda c, t: (t, 0, c)),
        out_shape=jax.ShapeDtypeStruct((seq, bsz, width), F32),
        scratch_shapes=[pltpu.VMEM((tt * bsz, 2 * ns), F32), pltpu.VMEM((bsz, 2 * ns), F32)],
        compiler_params=_params("parallel", "arbitrary"),
        name="s5_scan",
    )(u_tm, bcat, ccat, acat, d.reshape(n_col, 1, LANES))
    z = z.reshape(seq * bsz, width)
    tm, tn = min(ROW_TILE, seq * bsz), min(512, width)
    return pl.pallas_call(
        _s5_glu_kernel,
        grid=(seq * bsz // tm, width // tn),
        in_specs=[pl.BlockSpec((tm, width), lambda i, j: (i, 0)),
                  pl.BlockSpec((tm, tn), lambda i, j: (i, j)),
                  pl.BlockSpec((width, tn), lambda i, j: (0, j)),
                  pl.BlockSpec((1, tn), lambda i, j: (0, j))],
        out_specs=pl.BlockSpec((tm, tn), lambda i, j: (i, j)),
        out_shape=jax.ShapeDtypeStruct((seq * bsz, width), BF16),
        compiler_params=_params("parallel", "parallel"),
        name="s5_glu",
    )(z, z, glu_w.astype(BF16), glu_b.reshape(1, width))


def _mixing_sublayer(hb, bsz, seq, rope, lay, w_in, phi_pe, phi_w1, phi_w2, lam_re, lam_im, log_dt,
                     b_re, b_im, c_re, c_im, s5_d, glu_w, glu_b, w_out):
    w_ra, w_rb, w_pc, w_pd = lay.weights(w_in)
    view = lambda t: t.reshape(bsz, seq, t.shape[-1])
    ra = view(_matmul(hb, w_ra, lay.ra_tile, BF16, rope))
    rb = view(_matmul(hb, w_rb, w_rb.shape[1], F32, rope))
    pc = view(_matmul(hb, w_pc, w_pc.shape[1], BF16))
    pd = view(_matmul(hb, w_pd, lay.pd_tile, F32))
    o_a = _dsa_mixer(ra, pc, pd, lay)
    o_b = _nsa_mixer(ra, rb, pc, pd, lay, phi_pe, phi_w1, phi_w2)
    u_tm = pd[:, :, lay.c_u:lay.c_u + lay.cw].transpose(1, 0, 2)
    o_c = _s5_mixer(u_tm, lam_re, lam_im, log_dt, b_re, b_im, c_re, c_im, s5_d, glu_w, glu_b)
    o_c = o_c.reshape(seq, bsz, lay.cw).transpose(1, 0, 2).reshape(bsz * seq, lay.cw)
    return _out_proj(o_a.reshape(bsz * seq, -1), o_b.reshape(bsz * seq, -1), o_c, w_out.astype(BF16))


def _moe(xt, xb, router_w, router_bias, w_gate, w_up, w_down):
    n_tok, dm = xt.shape
    aff = jax.nn.sigmoid(xt @ router_w)
    sel = (aff + router_bias).reshape(n_tok, N_EXPERT_GROUPS, EXPERTS_PER_GROUP)
    grp_score = jnp.sum(lax.top_k(sel, TOP_K)[0], axis=-1)
    _, grp = lax.top_k(grp_score, 1)
    grp_idx = jnp.broadcast_to(grp[:, :, None], (n_tok, 1, EXPERTS_PER_GROUP))
    sel_in = jnp.take_along_axis(sel, grp_idx, axis=1)[:, 0]
    _, loc = lax.top_k(sel_in, TOP_K)
    eid = grp * EXPERTS_PER_GROUP + loc
    w = jnp.take_along_axis(aff, eid, axis=1)
    w = w / jnp.sum(w, axis=-1, keepdims=True)

    n_asg = n_tok * TOP_K
    cap = -(-n_asg // MOE_BLOCK) * MOE_BLOCK + N_EXPERTS * MOE_BLOCK
    e_flat = eid.reshape(-1)
    tok_flat = jnp.arange(n_asg, dtype=jnp.int32) // TOP_K
    order = jnp.argsort(e_flat)
    e_sorted = e_flat[order]
    counts = jnp.bincount(e_flat, length=N_EXPERTS)
    padded = (counts + MOE_BLOCK - 1) // MOE_BLOCK * MOE_BLOCK
    start = jnp.cumsum(counts) - counts
    pstart = jnp.cumsum(padded) - padded
    pend = pstart + padded
    dest = (pstart[e_sorted] + (jnp.arange(n_asg) - start[e_sorted])).astype(jnp.int32)
    slot_tok = jnp.full((cap,), n_tok, jnp.int32).at[dest].set(tok_flat[order])
    slot_gate = jnp.zeros((cap,), F32).at[dest].set(w.reshape(-1)[order])
    n_blocks = cap // MOE_BLOCK
    blk_start = jnp.arange(n_blocks) * MOE_BLOCK
    blk_exp = jnp.minimum(jnp.sum(pend[None, :] <= blk_start[:, None], axis=1), N_EXPERTS - 1).astype(jnp.int32)
    n_act = (pend[-1] // MOE_BLOCK).astype(jnp.int32).reshape(1)
    x_pad = jnp.concatenate([xb, jnp.zeros((1, dm), BF16)], axis=0)
    xs = x_pad[slot_tok]
    y = _moe_experts(xs, blk_exp, n_act, slot_gate, w_gate.astype(BF16), w_up.astype(BF16),
                     w_down.astype(BF16))
    pos = jnp.zeros((n_asg,), jnp.int32).at[order].set(dest).reshape(n_tok, TOP_K)
    return y[pos[:, 0]] + y[pos[:, 1]]


def kernel(x, positions, ln_in_g, ln_in_b, w_in, nsa_phi_pe, nsa_phi_w1, nsa_phi_w2, s5_lam_re, s5_lam_im, s5_log_dt, s5_b_re, s5_b_im, s5_c_re, s5_c_im, s5_d, s5_glu_w, s5_glu_b, w_out, ln_mix_g, ln_mix_b, router_w, router_bias, moe_w_gate, moe_w_up, moe_w_down, ln_ffn_g, ln_ffn_b):
    depth = w_in.shape[0]
    bsz, seq, d_model = x.shape
    alpha = (2 * depth) ** 0.25
    lay = _Packed(d_model)
    rope = _rope_tables(positions)
    h, hb = _layer_norm(x.reshape(bsz * seq, d_model), ln_in_g, ln_in_b)
    for l in range(depth):
        mix = _mixing_sublayer(hb, bsz, seq, rope, lay, w_in[l], nsa_phi_pe[l], nsa_phi_w1[l], nsa_phi_w2[l],
                               s5_lam_re[l], s5_lam_im[l], s5_log_dt[l], s5_b_re[l], s5_b_im[l],
                               s5_c_re[l], s5_c_im[l], s5_d[l], s5_glu_w[l], s5_glu_b[l], w_out[l])
        h, hb = _layer_norm(h, ln_mix_g[l], ln_mix_b[l], residual=mix, alpha=alpha)
        ffn = _moe(h, hb, router_w, router_bias, moe_w_gate[l], moe_w_up[l], moe_w_down[l])
        h, hb = _layer_norm(h, ln_ffn_g[l], ln_ffn_b[l], residual=ffn, alpha=alpha)
    return h.reshape(bsz, seq, d_model)
```

```python
import functools
import math

import numpy as np
import jax
import jax.numpy as jnp
from jax import lax
from jax.experimental import pallas as pl
from jax.experimental.pallas import tpu as pltpu

F32 = jnp.float32
BF16 = jnp.bfloat16

LANES = 128
HEAD_DIM = 128
ROT_DIM = HEAD_DIM // 4
ROPE_THETA = 500000.0
A_IDX_HEADS = 4
A_IDX_DIM = 128
A_TOPK_MAX = 256
A_QBLOCK = 128
B_KV_GROUPS = 2
B_CMP_LEN = 32
B_CMP_STRIDE = 16
B_SEL_LEN = 64
B_SEL_N = 16
B_WIN = 512
C_GROUP = 16
C_STATE = 64
N_EXPERTS = 32
N_EXPERT_GROUPS = 4
EXPERTS_PER_GROUP = N_EXPERTS // N_EXPERT_GROUPS
TOP_K = 2
MOE_BLOCK = 256
LN_EPS = 1e-5

VMEM_LIMIT_BYTES = 56 * 1024 * 1024
ROW_TILE = 1024


def _params(*semantics):
    return pltpu.CompilerParams(dimension_semantics=semantics, vmem_limit_bytes=VMEM_LIMIT_BYTES)


def _mm_kernel(a_ref, b_ref, o_ref):
    o_ref[...] = jnp.dot(a_ref[...], b_ref[...], preferred_element_type=F32).astype(o_ref.dtype)


def _mm_rope_kernel(a_ref, b_ref, c_ref, sa_ref, sb_ref, o_ref):
    y = jnp.dot(a_ref[...], b_ref[...], preferred_element_type=F32)
    half = ROT_DIM // 2
    for hb in range(y.shape[1] // HEAD_DIM):
        cols = slice(hb * HEAD_DIM, (hb + 1) * HEAD_DIM)
        yh = y[:, cols]
        out = (yh * c_ref[...] + pltpu.roll(yh, half, 1) * sa_ref[...]
               + pltpu.roll(yh, HEAD_DIM - half, 1) * sb_ref[...])
        o_ref[:, cols] = out.astype(o_ref.dtype)


def _matmul(a, b, tn, out_dtype, rope=None):
    m, k = a.shape
    n = b.shape[1]
    tm = min(ROW_TILE, m)
    assert m % tm == 0 and n % tn == 0
    in_specs = [pl.BlockSpec((tm, k), lambda i, j: (i, 0)),
                pl.BlockSpec((k, tn), lambda i, j: (0, j))]
    args = (a, b)
    if rope is not None:
        in_specs += [pl.BlockSpec((tm, HEAD_DIM), lambda i, j: (i, 0))] * 3
        args += tuple(rope)
    return pl.pallas_call(
        _mm_kernel if rope is None else _mm_rope_kernel,
        grid=(m // tm, n // tn),
        in_specs=in_specs,
        out_specs=pl.BlockSpec((tm, tn), lambda i, j: (i, j)),
        out_shape=jax.ShapeDtypeStruct((m, n), out_dtype),
        compiler_params=_params("parallel", "parallel"),
        name="in_proj" if rope is None else "in_proj_rope",
    )(*args)


def _out_proj_kernel(a0_ref, a1_ref, a2_ref, w_ref, o_ref):
    k0 = a0_ref.shape[1]
    k1 = k0 + a1_ref.shape[1]
    acc = jnp.dot(a0_ref[...], w_ref[:k0, :], preferred_element_type=F32)
    acc = acc + jnp.dot(a1_ref[...], w_ref[k0:k1, :], preferred_element_type=F32)
    o_ref[...] = acc + jnp.dot(a2_ref[...], w_ref[k1:, :], preferred_element_type=F32)


def _out_proj(a0, a1, a2, w, tn=512):
    m = a0.shape[0]
    k, n = w.shape
    tm = min(ROW_TILE, m)
    assert a0.shape[1] + a1.shape[1] + a2.shape[1] == k and m % tm == 0 and n % tn == 0
    amap = lambda i, j: (i, 0)
    return pl.pallas_call(
        _out_proj_kernel,
        grid=(m // tm, n // tn),
        in_specs=[pl.BlockSpec((tm, a0.shape[1]), amap), pl.BlockSpec((tm, a1.shape[1]), amap),
                  pl.BlockSpec((tm, a2.shape[1]), amap), pl.BlockSpec((k, tn), lambda i, j: (0, j))],
        out_specs=pl.BlockSpec((tm, tn), lambda i, j: (i, j)),
        out_shape=jax.ShapeDtypeStruct((m, n), F32),
        compiler_params=_params("parallel", "parallel"),
        name="out_proj",
    )(a0, a1, a2, w)


LN_ROWS = 256


def _ln_store(v, g_ref, b_ref, o_ref, ob_ref):
    mu = jnp.mean(v, axis=-1, keepdims=True)
    c = v - mu
    var = jnp.mean(c * c, axis=-1, keepdims=True)
    out = c * lax.rsqrt(var + LN_EPS) * g_ref[...] + b_ref[...]
    o_ref[...] = out
    ob_ref[...] = out.astype(BF16)


def _ln_kernel(x_ref, g_ref, b_ref, o_ref, ob_ref):
    _ln_store(x_ref[...], g_ref, b_ref, o_ref, ob_ref)


def _ln_residual_kernel(x_ref, r_ref, g_ref, b_ref, o_ref, ob_ref, *, alpha):
    _ln_store(alpha * x_ref[...] + r_ref[...], g_ref, b_ref, o_ref, ob_ref)


def _layer_norm(x, g, b, residual=None, alpha=1.0):
    m, d = x.shape
    rows = pl.BlockSpec((LN_ROWS, d), lambda i: (i, 0))
    vec = pl.BlockSpec((1, d), lambda i: (0, 0))
    assert m % LN_ROWS == 0
    if residual is None:
        body, ins, args = _ln_kernel, [rows, vec, vec], (x, g.reshape(1, d), b.reshape(1, d))
    else:
        body = functools.partial(_ln_residual_kernel, alpha=alpha)
        ins, args = [rows, rows, vec, vec], (x, residual, g.reshape(1, d), b.reshape(1, d))
    return pl.pallas_call(
        body, grid=(m // LN_ROWS,), in_specs=ins, out_specs=[rows, rows],
        out_shape=[jax.ShapeDtypeStruct((m, d), F32), jax.ShapeDtypeStruct((m, d), BF16)],
        compiler_params=_params("parallel"), name="layer_norm",
    )(*args)


def _moe_up_kernel(blk_exp_ref, n_act_ref, x_ref, wg_ref, wu_ref, hid_ref):
    i = pl.program_id(0)

    @pl.when(i < n_act_ref[0])
    def _():
        x = x_ref[...]
        g = jnp.dot(x, wg_ref[0, 0], preferred_element_type=F32)
        u = jnp.dot(x, wu_ref[0, 0], preferred_element_type=F32)
        hid_ref[...] = (g * jax.nn.sigmoid(g) * u).astype(hid_ref.dtype)

    @pl.when(i >= n_act_ref[0])
    def _():
        hid_ref[...] = jnp.zeros_like(hid_ref)


def _moe_down_kernel(blk_exp_ref, n_act_ref, hid_ref, wd_ref, gate_ref, y_ref):
    i = pl.program_id(0)

    @pl.when(i < n_act_ref[0])
    def _():
        y = jnp.dot(hid_ref[...], wd_ref[0, 0], preferred_element_type=F32)
        y_ref[...] = y * gate_ref[...]

    @pl.when(i >= n_act_ref[0])
    def _():
        y_ref[...] = jnp.zeros_like(y_ref)


CAST_TILE_BYTES = 4 * 1024 * 1024


def _cast_kernel(x_ref, o_ref):
    o_ref[...] = x_ref[...].astype(o_ref.dtype)


def _to_bf16(w):
    k, n = w.shape[-2:]
    w3 = w.reshape(-1, k, n)
    tk = next(t for t in range(k, 0, -16) if k % t == 0 and t * n * 4 <= CAST_TILE_BYTES)
    tile = pl.BlockSpec((1, tk, n), lambda e, j: (e, j, 0))
    out = pl.pallas_call(
        _cast_kernel, grid=(w3.shape[0], k // tk), in_specs=[tile], out_specs=tile,
        out_shape=jax.ShapeDtypeStruct(w3.shape, BF16),
        compiler_params=_params("parallel", "parallel"), name="cast_bf16",
    )(w3)
    return out.reshape(w.shape)


def _moe_experts(xs, blk_exp, n_act, slot_gate, layer, wg, wu, wd):
    cap, dm = xs.shape
    de = wg.shape[-1]
    n_blocks = cap // MOE_BLOCK
    wmap = lambda i, be, na: (layer, be[i], 0, 0)
    rmap = lambda i, be, na: (i, 0)
    hid = pl.pallas_call(
        _moe_up_kernel,
        grid_spec=pltpu.PrefetchScalarGridSpec(
            num_scalar_prefetch=2, grid=(n_blocks,),
            in_specs=[pl.BlockSpec((MOE_BLOCK, dm), rmap),
                      pl.BlockSpec((1, 1, dm, de), wmap),
                      pl.BlockSpec((1, 1, dm, de), wmap)],
            out_specs=pl.BlockSpec((MOE_BLOCK, de), rmap)),
        out_shape=jax.ShapeDtypeStruct((cap, de), BF16),
        compiler_params=_params("arbitrary"),
        name="moe_gate_up",
    )(blk_exp, n_act, xs, wg, wu)
    return pl.pallas_call(
        _moe_down_kernel,
        grid_spec=pltpu.PrefetchScalarGridSpec(
            num_scalar_prefetch=2, grid=(n_blocks,),
            in_specs=[pl.BlockSpec((MOE_BLOCK, de), rmap),
                      pl.BlockSpec((1, 1, de, dm), wmap),
                      pl.BlockSpec((MOE_BLOCK, 1), rmap)],
            out_specs=pl.BlockSpec((MOE_BLOCK, dm), rmap)),
        out_shape=jax.ShapeDtypeStruct((cap, dm), F32),
        compiler_params=_params("arbitrary"),
        name="moe_down",
    )(blk_exp, n_act, hid, wd, slot_gate[:, None])


INT32_MIN = -2 ** 31


def _order_key(x):
    b = lax.bitcast_convert_type(x + 0.0, jnp.int32)
    return b ^ ((b >> 31) & 0x7FFFFFFF)


def _count(mask):
    return jnp.sum(jnp.where(mask, 1.0, 0.0), axis=-1, keepdims=True)


def _topk_mask(key, pos, k, n_pos_bits):
    kf = float(k)
    thr = jnp.where(_count(key >= 0) >= kf, 0, INT32_MIN).astype(jnp.int32)

    def value_step(i, thr):
        cand = thr + jnp.left_shift(jnp.int32(1), 30 - i)
        return jnp.where(_count(key >= cand) >= kf, cand, thr)

    thr = lax.fori_loop(0, 31, value_step, thr)
    above = key > thr
    tied = key == thr
    need = kf - _count(above)

    def pos_step(i, last):
        cand = last + jnp.left_shift(jnp.int32(1), n_pos_bits - 1 - i)
        return jnp.where(_count(tied & (pos < cand)) < need, cand, last)

    last = lax.fori_loop(0, n_pos_bits, pos_step, jnp.zeros_like(thr))
    return above | (tied & (pos <= last))


def _mask_bias(mask):
    return jnp.where(mask, 0.0, -jnp.inf)


def _softmax_weights(s, bias):
    s = s + bias
    m = jnp.max(s, axis=-1, keepdims=True)
    m = jnp.where(jnp.isfinite(m), m, 0.0)
    e = jnp.exp(s - m)
    return e, jnp.maximum(jnp.sum(e, axis=-1, keepdims=True), 1e-30)


def _dot_nt(a, b):
    return lax.dot_general(a, b, (((1,), (1,)), ((), ())), preferred_element_type=F32)


def _attend(q, k, v, bias, scale):
    e, denom = _softmax_weights(_dot_nt(q, k) * scale, bias)
    return jnp.dot(e.astype(BF16), v, preferred_element_type=F32) / denom


KEY_TILE = 512


def _for_causal_extent(q_end, seq, body):
    for klen in range(KEY_TILE, seq + 1, KEY_TILE):
        pl.when((q_end > klen - KEY_TILE) & (q_end <= klen))(functools.partial(body, klen))


class _Packed:
    def __init__(self, d_model):
        self.qw = d_model // 4
        self.gw = B_KV_GROUPS * HEAD_DIM
        self.iqw = A_IDX_HEADS * A_IDX_DIM
        self.cw = d_model // 2
        self.n_bheads = self.qw // HEAD_DIM
        self.gate_cols = 3 * self.n_bheads // B_KV_GROUPS
        qw, gw = self.qw, self.gw
        self.in_sizes = (qw, HEAD_DIM, HEAD_DIM, self.iqw, A_IDX_DIM, A_IDX_HEADS, qw,
                         gw, gw, gw, gw, gw, gw, 3 * self.n_bheads, self.cw)
        self.ra_tile = 512
        o = 0
        self.a_q = o; o += qw
        self.b_q = o; o += qw
        self.a_iq = o; o += self.iqw
        self.b_ks = o; o += gw
        self.b_kw = o; o += gw
        self.a_k = o; o += HEAD_DIM
        self.a_ik = o; o += A_IDX_DIM
        self.ra_width = -(-o // self.ra_tile) * self.ra_tile
        self.b_vs, self.b_vw, self.a_v = 0, gw, 2 * gw
        self.pc_width = 2 * gw + HEAD_DIM
        self.c_u = 0
        self.b_vc = self.cw
        self.a_iw = self.b_vc + gw
        self.b_g = self.a_iw + LANES
        self.pd_width = self.b_g + B_KV_GROUPS * LANES
        self.pd_tile = next(t for t in (896, 768, 640, 512, 384, 256, 128) if self.pd_width % t == 0)
        assert qw % 512 == 0 and self.iqw % 512 == 0 and self.a_iq % 512 == 0

    def weights(self, w_in):
        cuts = np.cumsum(self.in_sizes)[:-1].tolist()
        (a_q, a_k, a_v, a_iq, a_ik, a_iw, b_q, b_kc, b_vc, b_ks, b_vs, b_kw, b_vw, b_g,
         c_u) = jnp.split(w_in.astype(BF16), cuts, axis=1)
        zeros = lambda n: jnp.zeros((w_in.shape[0], n), BF16)
        ra = [a_q, b_q, a_iq, b_ks, b_kw, a_k, a_ik]
        ra.append(zeros(self.ra_width - sum(t.shape[1] for t in ra)))
        pd = [c_u, b_vc, a_iw, zeros(LANES - a_iw.shape[1])]
        for g in range(B_KV_GROUPS):
            pd += [b_g[:, g * self.gate_cols:(g + 1) * self.gate_cols], zeros(LANES - self.gate_cols)]
        cat = lambda ts: jnp.concatenate(ts, axis=1)
        return cat(ra), b_kc, cat([b_vs, b_vw, a_v]), cat(pd)


def _rope_tables(positions):
    half = ROT_DIM // 2
    inv = ROPE_THETA ** (-jnp.arange(0, ROT_DIM, 2, dtype=F32) / ROT_DIM)
    ang = positions.astype(F32).reshape(-1, 1) * inv
    cos, sin = jnp.cos(ang), jnp.sin(ang)
    n = ang.shape[0]
    ones, zeros, z16 = jnp.ones((n, HEAD_DIM - ROT_DIM), F32), jnp.zeros((n, HEAD_DIM - ROT_DIM), F32), jnp.zeros((n, half), F32)
    return (jnp.concatenate([cos, cos, ones], axis=1),
            jnp.concatenate([z16, sin, zeros], axis=1),
            jnp.concatenate([-sin, z16, zeros], axis=1))


def _dsa_kernel(q_ref, iq_ref, iw_ref, k_ref, v_ref, ik_ref, o_ref, *, n_keep, n_heads):
    seq = k_ref.shape[1]
    start = pl.program_id(1) * A_QBLOCK
    qpos = start + lax.broadcasted_iota(jnp.int32, (A_QBLOCK, 1), 0)

    def block(klen):
        kpos = lax.broadcasted_iota(jnp.int32, (1, klen), 1)
        causal = kpos <= qpos
        ik = ik_ref[0, :klen, :]
        iw = iw_ref[0] * (A_IDX_HEADS * A_IDX_DIM) ** -0.5
        score = jnp.zeros((A_QBLOCK, klen), F32)
        for h in range(A_IDX_HEADS):
            rel = _dot_nt(iq_ref[0, :, h * A_IDX_DIM:(h + 1) * A_IDX_DIM], ik)
            score = score + iw[:, h:h + 1] * jnp.maximum(rel, 0.0)
        score = jnp.where(causal, score, -jnp.inf)
        sel = _topk_mask(_order_key(score), kpos, n_keep, max(1, (klen - 1).bit_length())) & causal
        bias = _mask_bias(sel)
        k = k_ref[0, :klen, :]
        v = v_ref[0, :klen, :]
        for h in range(n_heads):
            hs = slice(h * HEAD_DIM, (h + 1) * HEAD_DIM)
            o_ref[0, :, hs] = _attend(q_ref[0, :, hs], k, v, bias, HEAD_DIM ** -0.5).astype(o_ref.dtype)

    _for_causal_extent(start + A_QBLOCK, seq, block)


def _dsa_mixer(ra, pc, pd, lay):
    bsz, seq, _ = ra.shape
    n_keep = min(A_TOPK_MAX, seq // 4)
    assert seq % KEY_TILE == 0 and KEY_TILE >= n_keep
    qb = lambda width, off: pl.BlockSpec((1, A_QBLOCK, width), lambda b, i: (b, i, off // width))
    kb = lambda width, off: pl.BlockSpec((1, seq, width), lambda b, i: (b, 0, off // width))
    return pl.pallas_call(
        functools.partial(_dsa_kernel, n_keep=n_keep, n_heads=lay.qw // HEAD_DIM),
        grid=(bsz, seq // A_QBLOCK),
        in_specs=[qb(lay.qw, lay.a_q), qb(lay.iqw, lay.a_iq), qb(LANES, lay.a_iw),
                  kb(HEAD_DIM, lay.a_k), kb(HEAD_DIM, lay.a_v), kb(A_IDX_DIM, lay.a_ik)],
        out_specs=pl.BlockSpec((1, A_QBLOCK, lay.qw), lambda b, i: (b, i, 0)),
        out_shape=jax.ShapeDtypeStruct((bsz, seq, lay.qw), BF16),
        compiler_params=_params("parallel", "arbitrary"),
        name="dsa_mixer",
    )(ra, ra, pd, ra, pc, ra)


NSA_QBLOCK = 128
GELU_C = math.sqrt(2.0 / math.pi)


def _gelu_tanh(x):
    return 0.5 * x * (1.0 + jnp.tanh(GELU_C * (x + 0.044715 * (x * x * x))))


def _nsa_compress_kernel(kc_ref, vc_ref, pe_ref, w1_ref, w2_ref, ko_ref, vo_ref):
    n_chunk = ko_ref.shape[2]
    half = B_CMP_LEN // 2
    assert B_CMP_STRIDE == half
    row = lax.broadcasted_iota(jnp.int32, (n_chunk, 1), 0)
    for t, (src, dst) in enumerate(((kc_ref, ko_ref), (vc_ref, vo_ref))):
        lo = jnp.zeros((n_chunk, w1_ref.shape[-1]), F32)
        hi = jnp.zeros((n_chunk, w1_ref.shape[-1]), F32)
        for l in range(half):
            x = src[0, pl.ds(l, n_chunk, stride=B_CMP_STRIDE), :]
            lo = lo + jnp.dot((x + pe_ref[t, l:l + 1, :]).astype(BF16),
                              w1_ref[t, l * HEAD_DIM:(l + 1) * HEAD_DIM, :], preferred_element_type=F32)
            hi = hi + jnp.dot((x + pe_ref[t, half + l:half + l + 1, :]).astype(BF16),
                              w1_ref[t, (half + l) * HEAD_DIM:(half + l + 1) * HEAD_DIM, :],
                              preferred_element_type=F32)
        pre = lo + pltpu.roll(hi, n_chunk - 1, 0)
        out = jnp.dot(_gelu_tanh(pre).astype(BF16), w2_ref[t], preferred_element_type=F32)
        dst[0, 0] = jnp.where(row < n_chunk - 1, out, 0.0).astype(dst.dtype)


def _nsa_compress(rb, pd, lay, phi_pe, phi_w1, phi_w2):
    bsz, seq, _ = rb.shape
    n_chunk = seq // B_CMP_STRIDE
    cmap = lambda b, g: (0, 0, 0)
    omap = lambda b, g: (b, g, 0, 0)
    vc_blk = lay.b_vc // HEAD_DIM
    out_sds = jax.ShapeDtypeStruct((bsz, B_KV_GROUPS, n_chunk, HEAD_DIM), BF16)
    return pl.pallas_call(
        _nsa_compress_kernel,
        grid=(bsz, B_KV_GROUPS),
        in_specs=[pl.BlockSpec((1, seq, HEAD_DIM), lambda b, g: (b, 0, g)),
                  pl.BlockSpec((1, seq, HEAD_DIM), lambda b, g: (b, 0, vc_blk + g)),
                  pl.BlockSpec(phi_pe.shape, cmap),
                  pl.BlockSpec(phi_w1.shape, cmap),
                  pl.BlockSpec(phi_w2.shape, cmap)],
        out_specs=[pl.BlockSpec((1, 1, n_chunk, HEAD_DIM), omap),
                   pl.BlockSpec((1, 1, n_chunk, HEAD_DIM), omap)],
        out_shape=[out_sds, out_sds],
        compiler_params=_params("parallel", "parallel"),
        name="nsa_compress",
    )(rb, pd, phi_pe, phi_w1.astype(BF16), phi_w2.astype(BF16))


def _nsa_kernel(q_ref, gate_ref, kcmp_ref, vcmp_ref, ks_ref, vs_ref, kw_ref, vw_ref, ovl_ref, expand_ref,
                o_ref, *, n_sel, heads_per_group):
    seq = ks_ref.shape[1]
    n_cmp = kcmp_ref.shape[2]
    n_blk = seq // B_SEL_LEN
    scale = HEAD_DIM ** -0.5
    start = pl.program_id(2) * NSA_QBLOCK
    qpos = start + lax.broadcasted_iota(jnp.int32, (NSA_QBLOCK, 1), 0)
    heads = [q_ref[0, :, r * HEAD_DIM:(r + 1) * HEAD_DIM] for r in range(heads_per_group)]

    cend = lax.broadcasted_iota(jnp.int32, (1, n_cmp), 1) * B_CMP_STRIDE + (B_CMP_LEN - 1)
    cmask = _mask_bias(cend <= qpos)
    kcmp = kcmp_ref[0, 0]
    vcmp = vcmp_ref[0, 0]
    o_cmp = []
    psum = jnp.zeros((NSA_QBLOCK, n_cmp), F32)
    for q in heads:
        e, denom = _softmax_weights(_dot_nt(q, kcmp) * scale, cmask)
        p = e / denom
        psum = psum + p
        o_cmp.append(jnp.dot(p.astype(BF16), vcmp, preferred_element_type=F32))
    p_hi = psum.astype(BF16)
    p_lo = (psum - p_hi.astype(F32)).astype(BF16)
    ovl = ovl_ref[...]
    imp = (jnp.dot(p_hi, ovl, preferred_element_type=F32) + jnp.dot(p_lo, ovl, preferred_element_type=F32))
    blk = lax.broadcasted_iota(jnp.int32, (1, imp.shape[1]), 1)
    cur = qpos // B_SEL_LEN
    imp = jnp.where((blk == 0) | (blk == cur) | (blk == cur - 1), jnp.inf,
                    jnp.where(blk > cur, -jnp.inf, imp))
    key = jnp.where(blk < n_blk, _order_key(imp), INT32_MIN)
    blk_sel = _topk_mask(key, blk, n_sel, max(1, (n_blk - 1).bit_length()))

    n_win = B_WIN + NSA_QBLOCK
    kstart = pl.multiple_of(jnp.maximum(start - B_WIN, 0), NSA_QBLOCK)
    wpos = kstart + lax.broadcasted_iota(jnp.int32, (1, n_win), 1)
    win_bias = _mask_bias((wpos <= qpos) & (wpos > qpos - B_WIN))
    kw = kw_ref[0, pl.ds(kstart, n_win), :]
    vw = vw_ref[0, pl.ds(kstart, n_win), :]
    o_win = [_attend(q, kw, vw, win_bias, scale) for q in heads]

    gates = jax.nn.sigmoid(gate_ref[0])
    sel_rows = jnp.where(blk_sel, 1.0, 0.0).astype(BF16)

    def selected_and_combine(klen):
        kpos = lax.broadcasted_iota(jnp.int32, (1, klen), 1)
        in_sel = jnp.dot(sel_rows, expand_ref[:, :klen], preferred_element_type=F32)
        sel_bias = _mask_bias((in_sel > 0.5) & (kpos <= qpos))
        ks = ks_ref[0, :klen, :]
        vs = vs_ref[0, :klen, :]
        for r, q in enumerate(heads):
            gr = gates[:, 3 * r:3 * r + 3]
            out = (gr[:, 0:1] * o_cmp[r] + gr[:, 1:2] * _attend(q, ks, vs, sel_bias, scale)
                   + gr[:, 2:3] * o_win[r])
            o_ref[0, :, r * HEAD_DIM:(r + 1) * HEAD_DIM] = out.astype(o_ref.dtype)

    _for_causal_extent(start + NSA_QBLOCK, seq, selected_and_combine)


def _nsa_mixer(ra, rb, pc, pd, lay, phi_pe, phi_w1, phi_w2):
    bsz, seq, _ = ra.shape
    hpg = lay.n_bheads // B_KV_GROUPS
    gq = hpg * HEAD_DIM
    assert seq % KEY_TILE == 0 and seq >= B_WIN + NSA_QBLOCK and KEY_TILE % NSA_QBLOCK == 0
    kcmp, vcmp = _nsa_compress(rb, pd, lay, phi_pe, phi_w1, phi_w2)
    n_cmp = seq // B_CMP_STRIDE
    n_blk = seq // B_SEL_LEN
    n_sel = min(B_SEL_N, n_blk)
    assert n_blk <= LANES
    c_lo = np.arange(n_cmp)[:, None] * B_CMP_STRIDE
    b_lo = np.arange(LANES)[None, :] * B_SEL_LEN
    overlap = ((c_lo < b_lo + B_SEL_LEN) & (c_lo + B_CMP_LEN - 1 >= b_lo) & (np.arange(LANES)[None, :] < n_blk))
    expand = (np.arange(seq)[None, :] // B_SEL_LEN) == np.arange(LANES)[:, None]
    qb = lambda width, off: pl.BlockSpec((1, NSA_QBLOCK, width), lambda b, g, i: (b, i, off // width + g))
    kb = lambda off: pl.BlockSpec((1, seq, HEAD_DIM), lambda b, g, i: (b, 0, off // HEAD_DIM + g))
    cmap = lambda b, g, i: (b, g, 0, 0)
    const = lambda b, g, i: (0, 0)
    return pl.pallas_call(
        functools.partial(_nsa_kernel, n_sel=n_sel, heads_per_group=hpg),
        grid=(bsz, B_KV_GROUPS, seq // NSA_QBLOCK),
        in_specs=[qb(gq, lay.b_q), qb(LANES, lay.b_g),
                  pl.BlockSpec((1, 1, n_cmp, HEAD_DIM), cmap),
                  pl.BlockSpec((1, 1, n_cmp, HEAD_DIM), cmap),
                  kb(lay.b_ks), kb(lay.b_vs), kb(lay.b_kw), kb(lay.b_vw),
                  pl.BlockSpec((n_cmp, LANES), const),
                  pl.BlockSpec((LANES, seq), const)],
        out_specs=pl.BlockSpec((1, NSA_QBLOCK, gq), lambda b, g, i: (b, i, g)),
        out_shape=jax.ShapeDtypeStruct((bsz, seq, lay.qw), BF16),
        compiler_params=_params("parallel", "parallel", "arbitrary"),
        name="nsa_mixer",
    )(ra, pd, kcmp, vcmp, ra, pc, ra, pc, jnp.asarray(overlap, BF16), jnp.asarray(expand, BF16))


S5_TIME_BLOCK = 256
S5_UNROLL = 8


def _s5_scan_kernel(u_ref, bcat_ref, ccat_ref, a_ref, d_ref, z_ref, xs_ref, state_ref):
    tt, bsz, lanes = u_ref.shape
    ns = a_ref.shape[-1] // 2

    @pl.when(pl.program_id(1) == 0)
    def _():
        state_ref[...] = jnp.zeros_like(state_ref)

    u = u_ref[...].reshape(tt * bsz, lanes)
    xs_ref[...] = jnp.dot(u.astype(BF16), bcat_ref[0], preferred_element_type=F32)
    a_re = jnp.broadcast_to(a_ref[0, :, :ns], (bsz, ns))
    a_im = jnp.broadcast_to(a_ref[0, :, ns:], (bsz, ns))

    def step(t, carry):
        s_re, s_im = carry
        rows = pl.ds(pl.multiple_of(t * bsz, bsz), bsz)
        n_re = a_re * s_re - a_im * s_im + xs_ref[rows, :ns]
        n_im = a_re * s_im + a_im * s_re + xs_ref[rows, ns:]
        xs_ref[rows, :ns] = n_re
        xs_ref[rows, ns:] = n_im
        return n_re, n_im

    s_re, s_im = lax.fori_loop(0, tt, step, (state_ref[:, :ns], state_ref[:, ns:]), unroll=S5_UNROLL)
    state_ref[:, :ns] = s_re
    state_ref[:, ns:] = s_im
    y = jnp.dot(xs_ref[...].astype(BF16), ccat_ref[0], preferred_element_type=F32) + d_ref[0] * u
    z_ref[...] = _gelu_tanh(y).reshape(tt, bsz, lanes)


def _s5_glu_kernel(za_ref, zt_ref, w_ref, b_ref, o_ref):
    zz = jnp.dot(za_ref[...].astype(BF16), w_ref[...], preferred_element_type=F32) + b_ref[...]
    o_ref[...] = (zt_ref[...] * jax.nn.sigmoid(zz)).astype(o_ref.dtype)


def _s5_mixer(u_tm, lam_re, lam_im, log_dt, b_re, b_im, c_re, c_im, d, glu_w, glu_b):
    seq, bsz, width = u_tm.shape
    gpc = LANES // C_GROUP
    n_col = width // LANES
    ns = gpc * C_STATE
    tt = min(S5_TIME_BLOCK, seq)
    assert width % LANES == 0 and seq % tt == 0 and bsz % 8 == 0
    dt = jnp.exp(log_dt)[:, None]
    mag = jnp.exp(lam_re * dt)
    a_re, a_im = mag * jnp.cos(lam_im * dt), mag * jnp.sin(lam_im * dt)
    den = lam_re * lam_re + lam_im * lam_im
    co_re = ((a_re - 1.0) * lam_re + a_im * lam_im) / den
    co_im = (a_im * lam_re - (a_re - 1.0) * lam_im) / den
    bb_re = co_re[..., None] * b_re - co_im[..., None] * b_im
    bb_im = co_re[..., None] * b_im + co_im[..., None] * b_re
    eye = jnp.eye(gpc, dtype=F32)

    def block_diag(m):
        r, c = m.shape[-2:]
        return jnp.einsum('ngrc,gk->ngrkc', m, eye).reshape(n_col, gpc * r, gpc * c)

    to_in = lambda m: block_diag(m.reshape(n_col, gpc, C_STATE, C_GROUP).transpose(0, 1, 3, 2))
    to_out = lambda m: block_diag(m.reshape(n_col, gpc, C_GROUP, C_STATE).transpose(0, 1, 3, 2))
    bcat = jnp.concatenate([to_in(bb_re), to_in(bb_im)], axis=-1).astype(BF16)
    ccat = jnp.concatenate([to_out(c_re), to_out(-c_im)], axis=-2).astype(BF16)
    acat = jnp.concatenate([a_re.reshape(n_col, 1, ns), a_im.reshape(n_col, 1, ns)], axis=-1)
    z = pl.pallas_call(
        _s5_scan_kernel,
        grid=(n_col, seq // tt),
        in_specs=[pl.BlockSpec((tt, bsz, LANES), lambda c, t: (t, 0, c)),
                  pl.BlockSpec((1, LANES, 2 * ns), lambda c, t: (c, 0, 0)),
                  pl.BlockSpec((1, 2 * ns, LANES), lambda c, t: (c, 0, 0)),
                  pl.BlockSpec((1, 1, 2 * ns), lambda c, t: (c, 0, 0)),
                  pl.BlockSpec((1, 1, LANES), lambda c, t: (c, 0, 0))],
        out_specs=pl.BlockSpec((tt, bsz, LANES), lambda c, t: (t, 0, c)),
        out_shape=jax.ShapeDtypeStruct((seq, bsz, width), F32),
        scratch_shapes=[pltpu.VMEM((tt * bsz, 2 * ns), F32), pltpu.VMEM((bsz, 2 * ns), F32)],
        compiler_params=_params("parallel", "arbitrary"),
        name="s5_scan",
    )(u_tm, bcat, ccat, acat, d.reshape(n_col, 1, LANES))
    z = z.reshape(seq * bsz, width)
    tm, tn = min(ROW_TILE, seq * bsz), min(512, width)
    return pl.pallas_call(
        _s5_glu_kernel,
        grid=(seq * bsz // tm, width // tn),
        in_specs=[pl.BlockSpec((tm, width), lambda i, j: (i, 0)),
                  pl.BlockSpec((tm, tn), lambda i, j: (i, j)),
                  pl.BlockSpec((width, tn), lambda i, j: (0, j)),
                  pl.BlockSpec((1, tn), lambda i, j: (0, j))],
        out_specs=pl.BlockSpec((tm, tn), lambda i, j: (i, j)),
        out_shape=jax.ShapeDtypeStruct((seq * bsz, width), BF16),
        compiler_params=_params("parallel", "parallel"),
        name="s5_glu",
    )(z, z, glu_w.astype(BF16), glu_b.reshape(1, width))


def _mixing_sublayer(hb, bsz, seq, rope, lay, w_in, phi_pe, phi_w1, phi_w2, lam_re, lam_im, log_dt,
                     b_re, b_im, c_re, c_im, s5_d, glu_w, glu_b, w_out):
    w_ra, w_rb, w_pc, w_pd = lay.weights(w_in)
    view = lambda t: t.reshape(bsz, seq, t.shape[-1])
    ra = view(_matmul(hb, w_ra, lay.ra_tile, BF16, rope))
    rb = view(_matmul(hb, w_rb, w_rb.shape[1], F32, rope))
    pc = view(_matmul(hb, w_pc, w_pc.shape[1], BF16))
    pd = view(_matmul(hb, w_pd, lay.pd_tile, F32))
    o_a = _dsa_mixer(ra, pc, pd, lay)
    o_b = _nsa_mixer(ra, rb, pc, pd, lay, phi_pe, phi_w1, phi_w2)
    u_tm = pd[:, :, lay.c_u:lay.c_u + lay.cw].transpose(1, 0, 2)
    o_c = _s5_mixer(u_tm, lam_re, lam_im, log_dt, b_re, b_im, c_re, c_im, s5_d, glu_w, glu_b)
    o_c = o_c.reshape(seq, bsz, lay.cw).transpose(1, 0, 2).reshape(bsz * seq, lay.cw)
    return _out_proj(o_a.reshape(bsz * seq, -1), o_b.reshape(bsz * seq, -1), o_c, w_out.astype(BF16))


def _moe(xt, xb, router_w, router_bias, layer, w_gate, w_up, w_down):
    n_tok, dm = xt.shape
    aff = jax.nn.sigmoid(xt @ router_w)
    sel = (aff + router_bias).reshape(n_tok, N_EXPERT_GROUPS, EXPERTS_PER_GROUP)
    grp_score = jnp.sum(lax.top_k(sel, TOP_K)[0], axis=-1)
    _, grp = lax.top_k(grp_score, 1)
    grp_idx = jnp.broadcast_to(grp[:, :, None], (n_tok, 1, EXPERTS_PER_GROUP))
    sel_in = jnp.take_along_axis(sel, grp_idx, axis=1)[:, 0]
    _, loc = lax.top_k(sel_in, TOP_K)
    eid = grp * EXPERTS_PER_GROUP + loc
    w = jnp.take_along_axis(aff, eid, axis=1)
    w = w / jnp.sum(w, axis=-1, keepdims=True)

    n_asg = n_tok * TOP_K
    cap = -(-n_asg // MOE_BLOCK) * MOE_BLOCK + N_EXPERTS * MOE_BLOCK
    e_flat = eid.reshape(-1)
    tok_flat = jnp.arange(n_asg, dtype=jnp.int32) // TOP_K
    order = jnp.argsort(e_flat)
    e_sorted = e_flat[order]
    counts = jnp.bincount(e_flat, length=N_EXPERTS)
    padded = (counts + MOE_BLOCK - 1) // MOE_BLOCK * MOE_BLOCK
    start = jnp.cumsum(counts) - counts
    pstart = jnp.cumsum(padded) - padded
    pend = pstart + padded
    dest = (pstart[e_sorted] + (jnp.arange(n_asg) - start[e_sorted])).astype(jnp.int32)
    slot_tok = jnp.full((cap,), n_tok, jnp.int32).at[dest].set(tok_flat[order])
    slot_gate = jnp.zeros((cap,), F32).at[dest].set(w.reshape(-1)[order])
    n_blocks = cap // MOE_BLOCK
    blk_start = jnp.arange(n_blocks) * MOE_BLOCK
    blk_exp = jnp.minimum(jnp.sum(pend[None, :] <= blk_start[:, None], axis=1), N_EXPERTS - 1).astype(jnp.int32)
    n_act = (pend[-1] // MOE_BLOCK).astype(jnp.int32).reshape(1)
    x_pad = jnp.concatenate([xb, jnp.zeros((1, dm), BF16)], axis=0)
    xs = x_pad[slot_tok]
    y = _moe_experts(xs, blk_exp, n_act, slot_gate, layer, w_gate, w_up, w_down)
    pos = jnp.zeros((n_asg,), jnp.int32).at[order].set(dest).reshape(n_tok, TOP_K)
    return y[pos[:, 0]] + y[pos[:, 1]]


def kernel(x, positions, ln_in_g, ln_in_b, w_in, nsa_phi_pe, nsa_phi_w1, nsa_phi_w2, s5_lam_re, s5_lam_im, s5_log_dt, s5_b_re, s5_b_im, s5_c_re, s5_c_im, s5_d, s5_glu_w, s5_glu_b, w_out, ln_mix_g, ln_mix_b, router_w, router_bias, moe_w_gate, moe_w_up, moe_w_down, ln_ffn_g, ln_ffn_b):
    depth = w_in.shape[0]
    bsz, seq, d_model = x.shape
    alpha = (2 * depth) ** 0.25
    lay = _Packed(d_model)
    rope = _rope_tables(positions)
    wg_b, wu_b, wd_b = _to_bf16(moe_w_gate), _to_bf16(moe_w_up), _to_bf16(moe_w_down)
    h, hb = _layer_norm(x.reshape(bsz * seq, d_model), ln_in_g, ln_in_b)
    for l in range(depth):
        mix = _mixing_sublayer(hb, bsz, seq, rope, lay, w_in[l], nsa_phi_pe[l], nsa_phi_w1[l], nsa_phi_w2[l],
                               s5_lam_re[l], s5_lam_im[l], s5_log_dt[l], s5_b_re[l], s5_b_im[l],
                               s5_c_re[l], s5_c_im[l], s5_d[l], s5_glu_w[l], s5_glu_b[l], w_out[l])
        h, hb = _layer_norm(h, ln_mix_g[l], ln_mix_b[l], residual=mix, alpha=alpha)
        ffn = _moe(h, hb, router_w, router_bias, l, wg_b, wu_b, wd_b)
        h, hb = _layer_norm(h, ln_ffn_g[l], ln_ffn_b[l], residual=ffn, alpha=alpha)
    return h.reshape(bsz, seq, d_model)
```

```python
import functools
import math

import numpy as np
import jax
import jax.numpy as jnp
from jax import lax
from jax.experimental import pallas as pl
from jax.experimental.pallas import tpu as pltpu

F32 = jnp.float32
BF16 = jnp.bfloat16

LANES = 128
HEAD_DIM = 128
ROT_DIM = HEAD_DIM // 4
ROPE_THETA = 500000.0
A_IDX_HEADS = 4
A_IDX_DIM = 128
A_TOPK_MAX = 256
A_QBLOCK = 128
B_KV_GROUPS = 2
B_CMP_LEN = 32
B_CMP_STRIDE = 16
B_SEL_LEN = 64
B_SEL_N = 16
B_WIN = 512
C_GROUP = 16
C_STATE = 64
N_EXPERTS = 32
N_EXPERT_GROUPS = 4
EXPERTS_PER_GROUP = N_EXPERTS // N_EXPERT_GROUPS
TOP_K = 2
MOE_BLOCK = 256
LN_EPS = 1e-5

VMEM_LIMIT_BYTES = 56 * 1024 * 1024
ROW_TILE = 1024


def _params(*semantics):
    return pltpu.CompilerParams(dimension_semantics=semantics, vmem_limit_bytes=VMEM_LIMIT_BYTES)


def _mm_kernel(a_ref, b_ref, o_ref):
    o_ref[...] = jnp.dot(a_ref[...], b_ref[...], preferred_element_type=F32).astype(o_ref.dtype)


def _mm_rope_kernel(a_ref, b_ref, c_ref, sa_ref, sb_ref, o_ref):
    y = jnp.dot(a_ref[...], b_ref[...], preferred_element_type=F32)
    half = ROT_DIM // 2
    for hb in range(y.shape[1] // HEAD_DIM):
        cols = slice(hb * HEAD_DIM, (hb + 1) * HEAD_DIM)
        yh = y[:, cols]
        out = (yh * c_ref[...] + pltpu.roll(yh, half, 1) * sa_ref[...]
               + pltpu.roll(yh, HEAD_DIM - half, 1) * sb_ref[...])
        o_ref[:, cols] = out.astype(o_ref.dtype)


def _matmul(a, b, tn, out_dtype, rope=None):
    m, k = a.shape
    n = b.shape[1]
    tm = min(ROW_TILE, m)
    assert m % tm == 0 and n % tn == 0
    in_specs = [pl.BlockSpec((tm, k), lambda i, j: (i, 0)),
                pl.BlockSpec((k, tn), lambda i, j: (0, j))]
    args = (a, b)
    if rope is not None:
        in_specs += [pl.BlockSpec((tm, HEAD_DIM), lambda i, j: (i, 0))] * 3
        args += tuple(rope)
    return pl.pallas_call(
        _mm_kernel if rope is None else _mm_rope_kernel,
        grid=(m // tm, n // tn),
        in_specs=in_specs,
        out_specs=pl.BlockSpec((tm, tn), lambda i, j: (i, j)),
        out_shape=jax.ShapeDtypeStruct((m, n), out_dtype),
        compiler_params=_params("parallel", "parallel"),
        name="in_proj" if rope is None else "in_proj_rope",
    )(*args)


def _out_proj_kernel(a0_ref, a1_ref, a2_ref, w_ref, o_ref):
    k0 = a0_ref.shape[1]
    k1 = k0 + a1_ref.shape[1]
    acc = jnp.dot(a0_ref[...], w_ref[:k0, :], preferred_element_type=F32)
    acc = acc + jnp.dot(a1_ref[...], w_ref[k0:k1, :], preferred_element_type=F32)
    o_ref[...] = acc + jnp.dot(a2_ref[...], w_ref[k1:, :], preferred_element_type=F32)


def _out_proj(a0, a1, a2, w, tn=512):
    m = a0.shape[0]
    k, n = w.shape
    tm = min(ROW_TILE, m)
    assert a0.shape[1] + a1.shape[1] + a2.shape[1] == k and m % tm == 0 and n % tn == 0
    amap = lambda i, j: (i, 0)
    return pl.pallas_call(
        _out_proj_kernel,
        grid=(m // tm, n // tn),
        in_specs=[pl.BlockSpec((tm, a0.shape[1]), amap), pl.BlockSpec((tm, a1.shape[1]), amap),
                  pl.BlockSpec((tm, a2.shape[1]), amap), pl.BlockSpec((k, tn), lambda i, j: (0, j))],
        out_specs=pl.BlockSpec((tm, tn), lambda i, j: (i, j)),
        out_shape=jax.ShapeDtypeStruct((m, n), F32),
        compiler_params=_params("parallel", "parallel"),
        name="out_proj",
    )(a0, a1, a2, w)


LN_ROWS = 256


def _ln_store(v, g_ref, b_ref, o_ref, ob_ref):
    mu = jnp.mean(v, axis=-1, keepdims=True)
    c = v - mu
    var = jnp.mean(c * c, axis=-1, keepdims=True)
    out = c * lax.rsqrt(var + LN_EPS) * g_ref[...] + b_ref[...]
    o_ref[...] = out
    ob_ref[...] = out.astype(BF16)


def _ln_kernel(x_ref, g_ref, b_ref, o_ref, ob_ref):
    _ln_store(x_ref[...], g_ref, b_ref, o_ref, ob_ref)


def _ln_residual_kernel(x_ref, r_ref, g_ref, b_ref, o_ref, ob_ref, *, alpha):
    _ln_store(alpha * x_ref[...] + r_ref[...], g_ref, b_ref, o_ref, ob_ref)


def _layer_norm(x, g, b, residual=None, alpha=1.0):
    m, d = x.shape
    rows = pl.BlockSpec((LN_ROWS, d), lambda i: (i, 0))
    vec = pl.BlockSpec((1, d), lambda i: (0, 0))
    assert m % LN_ROWS == 0
    if residual is None:
        body, ins, args = _ln_kernel, [rows, vec, vec], (x, g.reshape(1, d), b.reshape(1, d))
    else:
        body = functools.partial(_ln_residual_kernel, alpha=alpha)
        ins, args = [rows, rows, vec, vec], (x, residual, g.reshape(1, d), b.reshape(1, d))
    return pl.pallas_call(
        body, grid=(m // LN_ROWS,), in_specs=ins, out_specs=[rows, rows],
        out_shape=[jax.ShapeDtypeStruct((m, d), F32), jax.ShapeDtypeStruct((m, d), BF16)],
        compiler_params=_params("parallel"), name="layer_norm",
    )(*args)


def _moe_up_kernel(blk_exp_ref, n_act_ref, x_ref, wg_ref, wu_ref, hid_ref):
    i = pl.program_id(0)

    @pl.when(i < n_act_ref[0])
    def _():
        x = x_ref[...]
        g = jnp.dot(x, wg_ref[0, 0], preferred_element_type=F32)
        u = jnp.dot(x, wu_ref[0, 0], preferred_element_type=F32)
        hid_ref[...] = (g * jax.nn.sigmoid(g) * u).astype(hid_ref.dtype)

    @pl.when(i >= n_act_ref[0])
    def _():
        hid_ref[...] = jnp.zeros_like(hid_ref)


def _moe_down_kernel(blk_exp_ref, n_act_ref, hid_ref, wd_ref, gate_ref, y_ref):
    i = pl.program_id(0)

    @pl.when(i < n_act_ref[0])
    def _():
        y = jnp.dot(hid_ref[...], wd_ref[0, 0], preferred_element_type=F32)
        y_ref[...] = y * gate_ref[...]

    @pl.when(i >= n_act_ref[0])
    def _():
        y_ref[...] = jnp.zeros_like(y_ref)


CAST_TILE_BYTES = 4 * 1024 * 1024


def _cast_kernel(x_ref, o_ref):
    o_ref[...] = x_ref[...].astype(o_ref.dtype)


def _to_bf16(w):
    k, n = w.shape[-2:]
    w3 = w.reshape(-1, k, n)
    tk = next(t for t in range(k, 0, -16) if k % t == 0 and t * n * 4 <= CAST_TILE_BYTES)
    tile = pl.BlockSpec((1, tk, n), lambda e, j: (e, j, 0))
    out = pl.pallas_call(
        _cast_kernel, grid=(w3.shape[0], k // tk), in_specs=[tile], out_specs=tile,
        out_shape=jax.ShapeDtypeStruct(w3.shape, BF16),
        compiler_params=_params("parallel", "parallel"), name="cast_bf16",
    )(w3)
    return out.reshape(w.shape)


def _moe_experts(xs, blk_exp, n_act, slot_gate, layer, wg, wu, wd):
    cap, dm = xs.shape
    de = wg.shape[-1]
    n_blocks = cap // MOE_BLOCK
    wmap = lambda i, be, na: (layer, be[i], 0, 0)
    rmap = lambda i, be, na: (i, 0)
    hid = pl.pallas_call(
        _moe_up_kernel,
        grid_spec=pltpu.PrefetchScalarGridSpec(
            num_scalar_prefetch=2, grid=(n_blocks,),
            in_specs=[pl.BlockSpec((MOE_BLOCK, dm), rmap),
                      pl.BlockSpec((1, 1, dm, de), wmap),
                      pl.BlockSpec((1, 1, dm, de), wmap)],
            out_specs=pl.BlockSpec((MOE_BLOCK, de), rmap)),
        out_shape=jax.ShapeDtypeStruct((cap, de), BF16),
        compiler_params=_params("arbitrary"),
        name="moe_gate_up",
    )(blk_exp, n_act, xs, wg, wu)
    return pl.pallas_call(
        _moe_down_kernel,
        grid_spec=pltpu.PrefetchScalarGridSpec(
            num_scalar_prefetch=2, grid=(n_blocks,),
            in_specs=[pl.BlockSpec((MOE_BLOCK, de), rmap),
                      pl.BlockSpec((1, 1, de, dm), wmap),
                      pl.BlockSpec((MOE_BLOCK, 1), rmap)],
            out_specs=pl.BlockSpec((MOE_BLOCK, dm), rmap)),
        out_shape=jax.ShapeDtypeStruct((cap, dm), F32),
        compiler_params=_params("arbitrary"),
        name="moe_down",
    )(blk_exp, n_act, hid, wd, slot_gate[:, None])


INT32_MIN = -2 ** 31
BISECT_UNROLL = 8


def _order_key(x):
    b = lax.bitcast_convert_type(x + 0.0, jnp.int32)
    return b ^ ((b >> 31) & 0x7FFFFFFF)


def _count(mask):
    return jnp.sum(jnp.where(mask, 1.0, 0.0), axis=-1, keepdims=True)


def _topk_mask(key, pos, k, n_pos_bits):
    kf = float(k)
    thr = jnp.where(_count(key >= 0) >= kf, 0, INT32_MIN).astype(jnp.int32)

    def value_step(i, thr):
        cand = thr + jnp.left_shift(jnp.int32(1), 30 - i)
        return jnp.where(_count(key >= cand) >= kf, cand, thr)

    thr = lax.fori_loop(0, 31, value_step, thr, unroll=BISECT_UNROLL)
    above = key > thr
    n_pos = 1 << n_pos_bits
    tied_pos = jnp.where(key == thr, pos, n_pos)
    need = kf - _count(above)

    def take_lowest(_):
        def pos_step(i, last):
            cand = last + jnp.left_shift(jnp.int32(1), n_pos_bits - 1 - i)
            return jnp.where(_count(tied_pos < cand) < need, cand, last)

        return lax.fori_loop(0, n_pos_bits, pos_step, jnp.zeros_like(thr), unroll=BISECT_UNROLL)

    def take_all(_):
        return jnp.full_like(thr, n_pos - 1)

    surplus = jnp.max(_count(tied_pos < n_pos) - need)
    last = lax.cond(surplus > 0.0, take_lowest, take_all, None)
    return above | (tied_pos <= last)


def _topk_mask_by_rank(key, pos, k, n):
    rank = jnp.zeros(key.shape, F32)
    for m in range(n):
        rank = rank + jnp.where(key[:, m:m + 1] > key - jnp.where(pos > m, 1, 0), 1.0, 0.0)
    return (rank < float(k)) & (pos < n)


def _mask_bias(mask):
    return jnp.where(mask, 0.0, -jnp.inf)


def _softmax_weights(s, bias):
    s = s + bias
    m = jnp.max(s, axis=-1, keepdims=True)
    m = jnp.where(jnp.isfinite(m), m, 0.0)
    e = jnp.exp(s - m)
    return e, jnp.maximum(jnp.sum(e, axis=-1, keepdims=True), 1e-30)


def _dot_nt(a, b):
    return lax.dot_general(a, b, (((1,), (1,)), ((), ())), preferred_element_type=F32)


def _attend(q, k, v, bias, scale):
    e, denom = _softmax_weights(_dot_nt(q, k) * scale, bias)
    return jnp.dot(e.astype(BF16), v, preferred_element_type=F32) / denom


KEY_TILE = 512


def _for_causal_extent(q_end, seq, body):
    for klen in range(KEY_TILE, seq + 1, KEY_TILE):
        pl.when((q_end > klen - KEY_TILE) & (q_end <= klen))(functools.partial(body, klen))


class _Packed:
    def __init__(self, d_model):
        self.qw = d_model // 4
        self.gw = B_KV_GROUPS * HEAD_DIM
        self.iqw = A_IDX_HEADS * A_IDX_DIM
        self.cw = d_model // 2
        self.n_bheads = self.qw // HEAD_DIM
        self.gate_cols = 3 * self.n_bheads // B_KV_GROUPS
        qw, gw = self.qw, self.gw
        self.in_sizes = (qw, HEAD_DIM, HEAD_DIM, self.iqw, A_IDX_DIM, A_IDX_HEADS, qw,
                         gw, gw, gw, gw, gw, gw, 3 * self.n_bheads, self.cw)
        self.ra_tile = 512
        o = 0
        self.a_q = o; o += qw
        self.b_q = o; o += qw
        self.a_iq = o; o += self.iqw
        self.b_ks = o; o += gw
        self.b_kw = o; o += gw
        self.a_k = o; o += HEAD_DIM
        self.a_ik = o; o += A_IDX_DIM
        self.ra_width = -(-o // self.ra_tile) * self.ra_tile
        self.b_vs, self.b_vw, self.a_v = 0, gw, 2 * gw
        self.pc_width = 2 * gw + HEAD_DIM
        self.c_u = 0
        self.b_vc = self.cw
        self.a_iw = self.b_vc + gw
        self.b_g = self.a_iw + LANES
        self.pd_width = self.b_g + B_KV_GROUPS * LANES
        self.pd_tile = next(t for t in (896, 768, 640, 512, 384, 256, 128) if self.pd_width % t == 0)
        assert qw % 512 == 0 and self.iqw % 512 == 0 and self.a_iq % 512 == 0

    def weights(self, w_in):
        cuts = np.cumsum(self.in_sizes)[:-1].tolist()
        (a_q, a_k, a_v, a_iq, a_ik, a_iw, b_q, b_kc, b_vc, b_ks, b_vs, b_kw, b_vw, b_g,
         c_u) = jnp.split(w_in.astype(BF16), cuts, axis=1)
        zeros = lambda n: jnp.zeros((w_in.shape[0], n), BF16)
        ra = [a_q, b_q, a_iq, b_ks, b_kw, a_k, a_ik]
        ra.append(zeros(self.ra_width - sum(t.shape[1] for t in ra)))
        pd = [c_u, b_vc, a_iw, zeros(LANES - a_iw.shape[1])]
        for g in range(B_KV_GROUPS):
            pd += [b_g[:, g * self.gate_cols:(g + 1) * self.gate_cols], zeros(LANES - self.gate_cols)]
        cat = lambda ts: jnp.concatenate(ts, axis=1)
        return cat(ra), b_kc, cat([b_vs, b_vw, a_v]), cat(pd)


def _rope_tables(positions):
    half = ROT_DIM // 2
    inv = ROPE_THETA ** (-jnp.arange(0, ROT_DIM, 2, dtype=F32) / ROT_DIM)
    ang = positions.astype(F32).reshape(-1, 1) * inv
    cos, sin = jnp.cos(ang), jnp.sin(ang)
    n = ang.shape[0]
    ones, zeros, z16 = jnp.ones((n, HEAD_DIM - ROT_DIM), F32), jnp.zeros((n, HEAD_DIM - ROT_DIM), F32), jnp.zeros((n, half), F32)
    return (jnp.concatenate([cos, cos, ones], axis=1),
            jnp.concatenate([z16, sin, zeros], axis=1),
            jnp.concatenate([-sin, z16, zeros], axis=1))


def _dsa_kernel(q_ref, iq_ref, iw_ref, k_ref, v_ref, ik_ref, o_ref, *, n_keep, n_heads):
    seq = k_ref.shape[1]
    start = pl.program_id(1) * A_QBLOCK
    qpos = start + lax.broadcasted_iota(jnp.int32, (A_QBLOCK, 1), 0)

    def block(klen):
        kpos = lax.broadcasted_iota(jnp.int32, (1, klen), 1)
        causal = kpos <= qpos
        ik = ik_ref[0, :klen, :]
        iw = iw_ref[0] * (A_IDX_HEADS * A_IDX_DIM) ** -0.5
        score = jnp.zeros((A_QBLOCK, klen), F32)
        for h in range(A_IDX_HEADS):
            rel = _dot_nt(iq_ref[0, :, h * A_IDX_DIM:(h + 1) * A_IDX_DIM], ik)
            score = score + iw[:, h:h + 1] * jnp.maximum(rel, 0.0)
        score = jnp.where(causal, score, -jnp.inf)
        sel = _topk_mask(_order_key(score), kpos, n_keep, max(1, (klen - 1).bit_length())) & causal
        bias = _mask_bias(sel)
        k = k_ref[0, :klen, :]
        v = v_ref[0, :klen, :]
        for h in range(n_heads):
            hs = slice(h * HEAD_DIM, (h + 1) * HEAD_DIM)
            o_ref[0, :, hs] = _attend(q_ref[0, :, hs], k, v, bias, HEAD_DIM ** -0.5).astype(o_ref.dtype)

    _for_causal_extent(start + A_QBLOCK, seq, block)


def _dsa_mixer(ra, pc, pd, lay):
    bsz, seq, _ = ra.shape
    n_keep = min(A_TOPK_MAX, seq // 4)
    assert seq % KEY_TILE == 0 and KEY_TILE >= n_keep
    qb = lambda width, off: pl.BlockSpec((1, A_QBLOCK, width), lambda b, i: (b, i, off // width))
    kb = lambda width, off: pl.BlockSpec((1, seq, width), lambda b, i: (b, 0, off // width))
    return pl.pallas_call(
        functools.partial(_dsa_kernel, n_keep=n_keep, n_heads=lay.qw // HEAD_DIM),
        grid=(bsz, seq // A_QBLOCK),
        in_specs=[qb(lay.qw, lay.a_q), qb(lay.iqw, lay.a_iq), qb(LANES, lay.a_iw),
                  kb(HEAD_DIM, lay.a_k), kb(HEAD_DIM, lay.a_v), kb(A_IDX_DIM, lay.a_ik)],
        out_specs=pl.BlockSpec((1, A_QBLOCK, lay.qw), lambda b, i: (b, i, 0)),
        out_shape=jax.ShapeDtypeStruct((bsz, seq, lay.qw), BF16),
        compiler_params=_params("parallel", "arbitrary"),
        name="dsa_mixer",
    )(ra, ra, pd, ra, pc, ra)


NSA_QBLOCK = 128
GELU_C = math.sqrt(2.0 / math.pi)


def _gelu_tanh(x):
    return 0.5 * x * (1.0 + jnp.tanh(GELU_C * (x + 0.044715 * (x * x * x))))


def _nsa_compress_kernel(kc_ref, vc_ref, pe_ref, w1_ref, w2_ref, ko_ref, vo_ref):
    n_chunk = ko_ref.shape[2]
    half = B_CMP_LEN // 2
    assert B_CMP_STRIDE == half
    row = lax.broadcasted_iota(jnp.int32, (n_chunk, 1), 0)
    for t, (src, dst) in enumerate(((kc_ref, ko_ref), (vc_ref, vo_ref))):
        lo = jnp.zeros((n_chunk, w1_ref.shape[-1]), F32)
        hi = jnp.zeros((n_chunk, w1_ref.shape[-1]), F32)
        for l in range(half):
            x = src[0, pl.ds(l, n_chunk, stride=B_CMP_STRIDE), :]
            lo = lo + jnp.dot((x + pe_ref[t, l:l + 1, :]).astype(BF16),
                              w1_ref[t, l * HEAD_DIM:(l + 1) * HEAD_DIM, :], preferred_element_type=F32)
            hi = hi + jnp.dot((x + pe_ref[t, half + l:half + l + 1, :]).astype(BF16),
                              w1_ref[t, (half + l) * HEAD_DIM:(half + l + 1) * HEAD_DIM, :],
                              preferred_element_type=F32)
        pre = lo + pltpu.roll(hi, n_chunk - 1, 0)
        out = jnp.dot(_gelu_tanh(pre).astype(BF16), w2_ref[t], preferred_element_type=F32)
        dst[0, 0] = jnp.where(row < n_chunk - 1, out, 0.0).astype(dst.dtype)


def _nsa_compress(rb, pd, lay, phi_pe, phi_w1, phi_w2):
    bsz, seq, _ = rb.shape
    n_chunk = seq // B_CMP_STRIDE
    cmap = lambda b, g: (0, 0, 0)
    omap = lambda b, g: (b, g, 0, 0)
    vc_blk = lay.b_vc // HEAD_DIM
    out_sds = jax.ShapeDtypeStruct((bsz, B_KV_GROUPS, n_chunk, HEAD_DIM), BF16)
    return pl.pallas_call(
        _nsa_compress_kernel,
        grid=(bsz, B_KV_GROUPS),
        in_specs=[pl.BlockSpec((1, seq, HEAD_DIM), lambda b, g: (b, 0, g)),
                  pl.BlockSpec((1, seq, HEAD_DIM), lambda b, g: (b, 0, vc_blk + g)),
                  pl.BlockSpec(phi_pe.shape, cmap),
                  pl.BlockSpec(phi_w1.shape, cmap),
                  pl.BlockSpec(phi_w2.shape, cmap)],
        out_specs=[pl.BlockSpec((1, 1, n_chunk, HEAD_DIM), omap),
                   pl.BlockSpec((1, 1, n_chunk, HEAD_DIM), omap)],
        out_shape=[out_sds, out_sds],
        compiler_params=_params("parallel", "parallel"),
        name="nsa_compress",
    )(rb, pd, phi_pe, phi_w1.astype(BF16), phi_w2.astype(BF16))


def _nsa_kernel(q_ref, gate_ref, kcmp_ref, vcmp_ref, ks_ref, vs_ref, kw_ref, vw_ref, ovl_ref, expand_ref,
                o_ref, *, n_sel, heads_per_group):
    seq = ks_ref.shape[1]
    n_cmp = kcmp_ref.shape[2]
    n_blk = seq // B_SEL_LEN
    scale = HEAD_DIM ** -0.5
    start = pl.program_id(2) * NSA_QBLOCK
    qpos = start + lax.broadcasted_iota(jnp.int32, (NSA_QBLOCK, 1), 0)
    heads = [q_ref[0, :, r * HEAD_DIM:(r + 1) * HEAD_DIM] for r in range(heads_per_group)]

    cend = lax.broadcasted_iota(jnp.int32, (1, n_cmp), 1) * B_CMP_STRIDE + (B_CMP_LEN - 1)
    cmask = _mask_bias(cend <= qpos)
    kcmp = kcmp_ref[0, 0]
    vcmp = vcmp_ref[0, 0]
    o_cmp = []
    psum = jnp.zeros((NSA_QBLOCK, n_cmp), F32)
    for q in heads:
        e, denom = _softmax_weights(_dot_nt(q, kcmp) * scale, cmask)
        p = e / denom
        psum = psum + p
        o_cmp.append(jnp.dot(p.astype(BF16), vcmp, preferred_element_type=F32))
    p_hi = psum.astype(BF16)
    p_lo = (psum - p_hi.astype(F32)).astype(BF16)
    ovl = ovl_ref[...]
    imp = (jnp.dot(p_hi, ovl, preferred_element_type=F32) + jnp.dot(p_lo, ovl, preferred_element_type=F32))
    blk = lax.broadcasted_iota(jnp.int32, (1, imp.shape[1]), 1)
    cur = qpos // B_SEL_LEN
    imp = jnp.where((blk == 0) | (blk == cur) | (blk == cur - 1), jnp.inf,
                    jnp.where(blk > cur, -jnp.inf, imp))
    blk_sel = _topk_mask_by_rank(_order_key(imp), blk, n_sel, n_blk)

    n_win = B_WIN + NSA_QBLOCK
    kstart = pl.multiple_of(jnp.maximum(start - B_WIN, 0), NSA_QBLOCK)
    wpos = kstart + lax.broadcasted_iota(jnp.int32, (1, n_win), 1)
    win_bias = _mask_bias((wpos <= qpos) & (wpos > qpos - B_WIN))
    kw = kw_ref[0, pl.ds(kstart, n_win), :]
    vw = vw_ref[0, pl.ds(kstart, n_win), :]
    o_win = [_attend(q, kw, vw, win_bias, scale) for q in heads]

    gates = jax.nn.sigmoid(gate_ref[0])
    sel_rows = jnp.where(blk_sel, 1.0, 0.0).astype(BF16)

    def selected_and_combine(klen):
        kpos = lax.broadcasted_iota(jnp.int32, (1, klen), 1)
        in_sel = jnp.dot(sel_rows, expand_ref[:, :klen], preferred_element_type=F32)
        sel_bias = _mask_bias((in_sel > 0.5) & (kpos <= qpos))
        ks = ks_ref[0, :klen, :]
        vs = vs_ref[0, :klen, :]
        for r, q in enumerate(heads):
            gr = gates[:, 3 * r:3 * r + 3]
            out = (gr[:, 0:1] * o_cmp[r] + gr[:, 1:2] * _attend(q, ks, vs, sel_bias, scale)
                   + gr[:, 2:3] * o_win[r])
            o_ref[0, :, r * HEAD_DIM:(r + 1) * HEAD_DIM] = out.astype(o_ref.dtype)

    _for_causal_extent(start + NSA_QBLOCK, seq, selected_and_combine)


def _nsa_mixer(ra, rb, pc, pd, lay, phi_pe, phi_w1, phi_w2):
    bsz, seq, _ = ra.shape
    hpg = lay.n_bheads // B_KV_GROUPS
    gq = hpg * HEAD_DIM
    assert seq % KEY_TILE == 0 and seq >= B_WIN + NSA_QBLOCK and KEY_TILE % NSA_QBLOCK == 0
    kcmp, vcmp = _nsa_compress(rb, pd, lay, phi_pe, phi_w1, phi_w2)
    n_cmp = seq // B_CMP_STRIDE
    n_blk = seq // B_SEL_LEN
    n_sel = min(B_SEL_N, n_blk)
    assert n_blk <= LANES
    c_lo = np.arange(n_cmp)[:, None] * B_CMP_STRIDE
    b_lo = np.arange(LANES)[None, :] * B_SEL_LEN
    overlap = ((c_lo < b_lo + B_SEL_LEN) & (c_lo + B_CMP_LEN - 1 >= b_lo) & (np.arange(LANES)[None, :] < n_blk))
    expand = (np.arange(seq)[None, :] // B_SEL_LEN) == np.arange(LANES)[:, None]
    qb = lambda width, off: pl.BlockSpec((1, NSA_QBLOCK, width), lambda b, g, i: (b, i, off // width + g))
    kb = lambda off: pl.BlockSpec((1, seq, HEAD_DIM), lambda b, g, i: (b, 0, off // HEAD_DIM + g))
    cmap = lambda b, g, i: (b, g, 0, 0)
    const = lambda b, g, i: (0, 0)
    return pl.pallas_call(
        functools.partial(_nsa_kernel, n_sel=n_sel, heads_per_group=hpg),
        grid=(bsz, B_KV_GROUPS, seq // NSA_QBLOCK),
        in_specs=[qb(gq, lay.b_q), qb(LANES, lay.b_g),
                  pl.BlockSpec((1, 1, n_cmp, HEAD_DIM), cmap),
                  pl.BlockSpec((1, 1, n_cmp, HEAD_DIM), cmap),
                  kb(lay.b_ks), kb(lay.b_vs), kb(lay.b_kw), kb(lay.b_vw),
                  pl.BlockSpec((n_cmp, LANES), const),
                  pl.BlockSpec((LANES, seq), const)],
        out_specs=pl.BlockSpec((1, NSA_QBLOCK, gq), lambda b, g, i: (b, i, g)),
        out_shape=jax.ShapeDtypeStruct((bsz, seq, lay.qw), BF16),
        compiler_params=_params("parallel", "parallel", "arbitrary"),
        name="nsa_mixer",
    )(ra, pd, kcmp, vcmp, ra, pc, ra, pc, jnp.asarray(overlap, BF16), jnp.asarray(expand, BF16))


S5_TIME_BLOCK = 256
S5_UNROLL = 8


def _s5_scan_kernel(u_ref, bcat_ref, ccat_ref, a_ref, d_ref, z_ref, xs_ref, state_ref):
    tt, bsz, lanes = u_ref.shape
    ns = a_ref.shape[-1] // 2

    @pl.when(pl.program_id(1) == 0)
    def _():
        state_ref[...] = jnp.zeros_like(state_ref)

    u = u_ref[...].reshape(tt * bsz, lanes)
    xs_ref[...] = jnp.dot(u.astype(BF16), bcat_ref[0], preferred_element_type=F32)
    a_re = jnp.broadcast_to(a_ref[0, :, :ns], (bsz, ns))
    a_im = jnp.broadcast_to(a_ref[0, :, ns:], (bsz, ns))

    def step(t, carry):
        s_re, s_im = carry
        rows = pl.ds(pl.multiple_of(t * bsz, bsz), bsz)
        n_re = a_re * s_re - a_im * s_im + xs_ref[rows, :ns]
        n_im = a_re * s_im + a_im * s_re + xs_ref[rows, ns:]
        xs_ref[rows, :ns] = n_re
        xs_ref[rows, ns:] = n_im
        return n_re, n_im

    s_re, s_im = lax.fori_loop(0, tt, step, (state_ref[:, :ns], state_ref[:, ns:]), unroll=S5_UNROLL)
    state_ref[:, :ns] = s_re
    state_ref[:, ns:] = s_im
    y = jnp.dot(xs_ref[...].astype(BF16), ccat_ref[0], preferred_element_type=F32) + d_ref[0] * u
    z_ref[...] = _gelu_tanh(y).reshape(tt, bsz, lanes)


def _s5_glu_kernel(za_ref, zt_ref, w_ref, b_ref, o_ref):
    zz = jnp.dot(za_ref[...].astype(BF16), w_ref[...], preferred_element_type=F32) + b_ref[...]
    o_ref[...] = (zt_ref[...] * jax.nn.sigmoid(zz)).astype(o_ref.dtype)


def _s5_mixer(u_tm, lam_re, lam_im, log_dt, b_re, b_im, c_re, c_im, d, glu_w, glu_b):
    seq, bsz, width = u_tm.shape
    gpc = LANES // C_GROUP
    n_col = width // LANES
    ns = gpc * C_STATE
    tt = min(S5_TIME_BLOCK, seq)
    assert width % LANES == 0 and seq % tt == 0 and bsz % 8 == 0
    dt = jnp.exp(log_dt)[:, None]
    mag = jnp.exp(lam_re * dt)
    a_re, a_im = mag * jnp.cos(lam_im * dt), mag * jnp.sin(lam_im * dt)
    den = lam_re * lam_re + lam_im * lam_im
    co_re = ((a_re - 1.0) * lam_re + a_im * lam_im) / den
    co_im = (a_im * lam_re - (a_re - 1.0) * lam_im) / den
    bb_re = co_re[..., None] * b_re - co_im[..., None] * b_im
    bb_im = co_re[..., None] * b_im + co_im[..., None] * b_re
    eye = jnp.eye(gpc, dtype=F32)

    def block_diag(m):
        r, c = m.shape[-2:]
        return jnp.einsum('ngrc,gk->ngrkc', m, eye).reshape(n_col, gpc * r, gpc * c)

    to_in = lambda m: block_diag(m.reshape(n_col, gpc, C_STATE, C_GROUP).transpose(0, 1, 3, 2))
    to_out = lambda m: block_diag(m.reshape(n_col, gpc, C_GROUP, C_STATE).transpose(0, 1, 3, 2))
    bcat = jnp.concatenate([to_in(bb_re), to_in(bb_im)], axis=-1).astype(BF16)
    ccat = jnp.concatenate([to_out(c_re), to_out(-c_im)], axis=-2).astype(BF16)
    acat = jnp.concatenate([a_re.reshape(n_col, 1, ns), a_im.reshape(n_col, 1, ns)], axis=-1)
    z = pl.pallas_call(
        _s5_scan_kernel,
        grid=(n_col, seq // tt),
        in_specs=[pl.BlockSpec((tt, bsz, LANES), lambda c, t: (t, 0, c)),
                  pl.BlockSpec((1, LANES, 2 * ns), lambda c, t: (c, 0, 0)),
                  pl.BlockSpec((1, 2 * ns, LANES), lambda c, t: (c, 0, 0)),
                  pl.BlockSpec((1, 1, 2 * ns), lambda c, t: (c, 0, 0)),
                  pl.BlockSpec((1, 1, LANES), lambda c, t: (c, 0, 0))],
        out_specs=pl.BlockSpec((tt, bsz, LANES), lambda c, t: (t, 0, c)),
        out_shape=jax.ShapeDtypeStruct((seq, bsz, width), F32),
        scratch_shapes=[pltpu.VMEM((tt * bsz, 2 * ns), F32), pltpu.VMEM((bsz, 2 * ns), F32)],
        compiler_params=_params("parallel", "arbitrary"),
        name="s5_scan",
    )(u_tm, bcat, ccat, acat, d.reshape(n_col, 1, LANES))
    z = z.reshape(seq * bsz, width)
    tm, tn = min(ROW_TILE, seq * bsz), min(512, width)
    return pl.pallas_call(
        _s5_glu_kernel,
        grid=(seq * bsz // tm, width // tn),
        in_specs=[pl.BlockSpec((tm, width), lambda i, j: (i, 0)),
                  pl.BlockSpec((tm, tn), lambda i, j: (i, j)),
                  pl.BlockSpec((width, tn), lambda i, j: (0, j)),
                  pl.BlockSpec((1, tn), lambda i, j: (0, j))],
        out_specs=pl.BlockSpec((tm, tn), lambda i, j: (i, j)),
        out_shape=jax.ShapeDtypeStruct((seq * bsz, width), BF16),
        compiler_params=_params("parallel", "parallel"),
        name="s5_glu",
    )(z, z, glu_w.astype(BF16), glu_b.reshape(1, width))


def _mixing_sublayer(hb, bsz, seq, rope, lay, w_in, phi_pe, phi_w1, phi_w2, lam_re, lam_im, log_dt,
                     b_re, b_im, c_re, c_im, s5_d, glu_w, glu_b, w_out):
    w_ra, w_rb, w_pc, w_pd = lay.weights(w_in)
    view = lambda t: t.reshape(bsz, seq, t.shape[-1])
    ra = view(_matmul(hb, w_ra, lay.ra_tile, BF16, rope))
    rb = view(_matmul(hb, w_rb, w_rb.shape[1], F32, rope))
    pc = view(_matmul(hb, w_pc, w_pc.shape[1], BF16))
    pd = view(_matmul(hb, w_pd, lay.pd_tile, F32))
    o_a = _dsa_mixer(ra, pc, pd, lay)
    o_b = _nsa_mixer(ra, rb, pc, pd, lay, phi_pe, phi_w1, phi_w2)
    u_tm = pd[:, :, lay.c_u:lay.c_u + lay.cw].transpose(1, 0, 2)
    o_c = _s5_mixer(u_tm, lam_re, lam_im, log_dt, b_re, b_im, c_re, c_im, s5_d, glu_w, glu_b)
    o_c = o_c.reshape(seq, bsz, lay.cw).transpose(1, 0, 2).reshape(bsz * seq, lay.cw)
    return _out_proj(o_a.reshape(bsz * seq, -1), o_b.reshape(bsz * seq, -1), o_c, w_out.astype(BF16))


def _moe(xt, xb, router_w, router_bias, layer, w_gate, w_up, w_down):
    n_tok, dm = xt.shape
    aff = jax.nn.sigmoid(xt @ router_w)
    sel = (aff + router_bias).reshape(n_tok, N_EXPERT_GROUPS, EXPERTS_PER_GROUP)
    grp_score = jnp.sum(lax.top_k(sel, TOP_K)[0], axis=-1)
    _, grp = lax.top_k(grp_score, 1)
    grp_idx = jnp.broadcast_to(grp[:, :, None], (n_tok, 1, EXPERTS_PER_GROUP))
    sel_in = jnp.take_along_axis(sel, grp_idx, axis=1)[:, 0]
    _, loc = lax.top_k(sel_in, TOP_K)
    eid = grp * EXPERTS_PER_GROUP + loc
    w = jnp.take_along_axis(aff, eid, axis=1)
    w = w / jnp.sum(w, axis=-1, keepdims=True)

    n_asg = n_tok * TOP_K
    cap = -(-n_asg // MOE_BLOCK) * MOE_BLOCK + N_EXPERTS * MOE_BLOCK
    e_flat = eid.reshape(-1)
    tok_flat = jnp.arange(n_asg, dtype=jnp.int32) // TOP_K
    order = jnp.argsort(e_flat)
    e_sorted = e_flat[order]
    counts = jnp.bincount(e_flat, length=N_EXPERTS)
    padded = (counts + MOE_BLOCK - 1) // MOE_BLOCK * MOE_BLOCK
    start = jnp.cumsum(counts) - counts
    pstart = jnp.cumsum(padded) - padded
    pend = pstart + padded
    dest = (pstart[e_sorted] + (jnp.arange(n_asg) - start[e_sorted])).astype(jnp.int32)
    slot_tok = jnp.full((cap,), n_tok, jnp.int32).at[dest].set(tok_flat[order])
    slot_gate = jnp.zeros((cap,), F32).at[dest].set(w.reshape(-1)[order])
    n_blocks = cap // MOE_BLOCK
    blk_start = jnp.arange(n_blocks) * MOE_BLOCK
    blk_exp = jnp.minimum(jnp.sum(pend[None, :] <= blk_start[:, None], axis=1), N_EXPERTS - 1).astype(jnp.int32)
    n_act = (pend[-1] // MOE_BLOCK).astype(jnp.int32).reshape(1)
    x_pad = jnp.concatenate([xb, jnp.zeros((1, dm), BF16)], axis=0)
    xs = x_pad[slot_tok]
    y = _moe_experts(xs, blk_exp, n_act, slot_gate, layer, w_gate, w_up, w_down)
    pos = jnp.zeros((n_asg,), jnp.int32).at[order].set(dest).reshape(n_tok, TOP_K)
    return y[pos[:, 0]] + y[pos[:, 1]]


def kernel(x, positions, ln_in_g, ln_in_b, w_in, nsa_phi_pe, nsa_phi_w1, nsa_phi_w2, s5_lam_re, s5_lam_im, s5_log_dt, s5_b_re, s5_b_im, s5_c_re, s5_c_im, s5_d, s5_glu_w, s5_glu_b, w_out, ln_mix_g, ln_mix_b, router_w, router_bias, moe_w_gate, moe_w_up, moe_w_down, ln_ffn_g, ln_ffn_b):
    depth = w_in.shape[0]
    bsz, seq, d_model = x.shape
    alpha = (2 * depth) ** 0.25
    lay = _Packed(d_model)
    rope = _rope_tables(positions)
    wg_b, wu_b, wd_b = _to_bf16(moe_w_gate), _to_bf16(moe_w_up), _to_bf16(moe_w_down)
    h, hb = _layer_norm(x.reshape(bsz * seq, d_model), ln_in_g, ln_in_b)
    for l in range(depth):
        mix = _mixing_sublayer(hb, bsz, seq, rope, lay, w_in[l], nsa_phi_pe[l], nsa_phi_w1[l], nsa_phi_w2[l],
                               s5_lam_re[l], s5_lam_im[l], s5_log_dt[l], s5_b_re[l], s5_b_im[l],
                               s5_c_re[l], s5_c_im[l], s5_d[l], s5_glu_w[l], s5_glu_b[l], w_out[l])
        h, hb = _layer_norm(h, ln_mix_g[l], ln_mix_b[l], residual=mix, alpha=alpha)
        ffn = _moe(h, hb, router_w, router_bias, l, wg_b, wu_b, wd_b)
        h, hb = _layer_norm(h, ln_ffn_g[l], ln_ffn_b[l], residual=ffn, alpha=alpha)
    return h.reshape(bsz, seq, d_model)
```

```python
import functools
import math

import numpy as np
import jax
import jax.numpy as jnp
from jax import lax
from jax.experimental import pallas as pl
from jax.experimental.pallas import tpu as pltpu

F32 = jnp.float32
BF16 = jnp.bfloat16

LANES = 128
HEAD_DIM = 128
ROT_DIM = HEAD_DIM // 4
ROPE_THETA = 500000.0
A_IDX_HEADS = 4
A_IDX_DIM = 128
A_TOPK_MAX = 256
A_QBLOCK = 128
B_KV_GROUPS = 2
B_CMP_LEN = 32
B_CMP_STRIDE = 16
B_SEL_LEN = 64
B_SEL_N = 16
B_WIN = 512
C_GROUP = 16
C_STATE = 64
N_EXPERTS = 32
N_EXPERT_GROUPS = 4
EXPERTS_PER_GROUP = N_EXPERTS // N_EXPERT_GROUPS
TOP_K = 2
MOE_BLOCK = 256
LN_EPS = 1e-5

VMEM_LIMIT_BYTES = 56 * 1024 * 1024
ROW_TILE = 1024


def _params(*semantics):
    return pltpu.CompilerParams(dimension_semantics=semantics, vmem_limit_bytes=VMEM_LIMIT_BYTES)


def _mm_kernel(a_ref, b_ref, o_ref):
    o_ref[...] = jnp.dot(a_ref[...], b_ref[...], preferred_element_type=F32).astype(o_ref.dtype)


def _mm_rope_kernel(a_ref, b_ref, c_ref, sa_ref, sb_ref, o_ref):
    y = jnp.dot(a_ref[...], b_ref[...], preferred_element_type=F32)
    half = ROT_DIM // 2
    for hb in range(y.shape[1] // HEAD_DIM):
        cols = slice(hb * HEAD_DIM, (hb + 1) * HEAD_DIM)
        yh = y[:, cols]
        out = (yh * c_ref[...] + pltpu.roll(yh, half, 1) * sa_ref[...]
               + pltpu.roll(yh, HEAD_DIM - half, 1) * sb_ref[...])
        o_ref[:, cols] = out.astype(o_ref.dtype)


def _matmul(a, b, tn, out_dtype, rope=None):
    m, k = a.shape
    n = b.shape[1]
    tm = min(ROW_TILE, m)
    assert m % tm == 0 and n % tn == 0
    in_specs = [pl.BlockSpec((tm, k), lambda i, j: (i, 0)),
                pl.BlockSpec((k, tn), lambda i, j: (0, j))]
    args = (a, b)
    if rope is not None:
        in_specs += [pl.BlockSpec((tm, HEAD_DIM), lambda i, j: (i, 0))] * 3
        args += tuple(rope)
    return pl.pallas_call(
        _mm_kernel if rope is None else _mm_rope_kernel,
        grid=(m // tm, n // tn),
        in_specs=in_specs,
        out_specs=pl.BlockSpec((tm, tn), lambda i, j: (i, j)),
        out_shape=jax.ShapeDtypeStruct((m, n), out_dtype),
        compiler_params=_params("parallel", "parallel"),
        name="in_proj" if rope is None else "in_proj_rope",
    )(*args)


def _out_proj_kernel(a0_ref, a1_ref, a2_ref, w_ref, o_ref):
    k0 = a0_ref.shape[1]
    k1 = k0 + a1_ref.shape[1]
    acc = jnp.dot(a0_ref[...], w_ref[:k0, :], preferred_element_type=F32)
    acc = acc + jnp.dot(a1_ref[...], w_ref[k0:k1, :], preferred_element_type=F32)
    o_ref[...] = acc + jnp.dot(a2_ref[...], w_ref[k1:, :], preferred_element_type=F32)


def _out_proj(a0, a1, a2, w, tn=512):
    m = a0.shape[0]
    k, n = w.shape
    tm = min(ROW_TILE, m)
    assert a0.shape[1] + a1.shape[1] + a2.shape[1] == k and m % tm == 0 and n % tn == 0
    amap = lambda i, j: (i, 0)
    return pl.pallas_call(
        _out_proj_kernel,
        grid=(m // tm, n // tn),
        in_specs=[pl.BlockSpec((tm, a0.shape[1]), amap), pl.BlockSpec((tm, a1.shape[1]), amap),
                  pl.BlockSpec((tm, a2.shape[1]), amap), pl.BlockSpec((k, tn), lambda i, j: (0, j))],
        out_specs=pl.BlockSpec((tm, tn), lambda i, j: (i, j)),
        out_shape=jax.ShapeDtypeStruct((m, n), F32),
        compiler_params=_params("parallel", "parallel"),
        name="out_proj",
    )(a0, a1, a2, w)


LN_ROWS = 256


def _ln_store(v, g_ref, b_ref, o_ref, ob_ref):
    mu = jnp.mean(v, axis=-1, keepdims=True)
    c = v - mu
    var = jnp.mean(c * c, axis=-1, keepdims=True)
    out = c * lax.rsqrt(var + LN_EPS) * g_ref[...] + b_ref[...]
    o_ref[...] = out
    ob_ref[...] = out.astype(BF16)


def _ln_kernel(x_ref, g_ref, b_ref, o_ref, ob_ref):
    _ln_store(x_ref[...], g_ref, b_ref, o_ref, ob_ref)


def _ln_residual_kernel(x_ref, r_ref, g_ref, b_ref, o_ref, ob_ref, *, alpha):
    _ln_store(alpha * x_ref[...] + r_ref[...], g_ref, b_ref, o_ref, ob_ref)


def _ln_gated_pair_kernel(x_ref, y0_ref, y1_ref, w_ref, g_ref, b_ref, o_ref, ob_ref, *, alpha):
    w = w_ref[...]
    ffn = w[:, 0:1] * y0_ref[...] + w[:, 1:2] * y1_ref[...]
    _ln_store(alpha * x_ref[...] + ffn, g_ref, b_ref, o_ref, ob_ref)


def _layer_norm(x, g, b, residual=None, alpha=1.0, gated_pair=None):
    m, d = x.shape
    rows = pl.BlockSpec((LN_ROWS, d), lambda i: (i, 0))
    vec = pl.BlockSpec((1, d), lambda i: (0, 0))
    assert m % LN_ROWS == 0
    if gated_pair is not None:
        y0, y1, w = gated_pair
        body = functools.partial(_ln_gated_pair_kernel, alpha=alpha)
        ins = [rows, rows, rows, pl.BlockSpec((LN_ROWS, w.shape[1]), lambda i: (i, 0)), vec, vec]
        args = (x, y0, y1, w, g.reshape(1, d), b.reshape(1, d))
    elif residual is None:
        body, ins, args = _ln_kernel, [rows, vec, vec], (x, g.reshape(1, d), b.reshape(1, d))
    else:
        body = functools.partial(_ln_residual_kernel, alpha=alpha)
        ins, args = [rows, rows, vec, vec], (x, residual, g.reshape(1, d), b.reshape(1, d))
    return pl.pallas_call(
        body, grid=(m // LN_ROWS,), in_specs=ins, out_specs=[rows, rows],
        out_shape=[jax.ShapeDtypeStruct((m, d), F32), jax.ShapeDtypeStruct((m, d), BF16)],
        compiler_params=_params("parallel"), name="layer_norm",
    )(*args)


def _moe_up_kernel(blk_exp_ref, n_act_ref, x_ref, wg_ref, wu_ref, hid_ref):
    i = pl.program_id(0)

    @pl.when(i < n_act_ref[0])
    def _():
        x = x_ref[...]
        g = jnp.dot(x, wg_ref[0, 0], preferred_element_type=F32)
        u = jnp.dot(x, wu_ref[0, 0], preferred_element_type=F32)
        hid_ref[...] = (g * jax.nn.sigmoid(g) * u).astype(hid_ref.dtype)

    @pl.when(i >= n_act_ref[0])
    def _():
        hid_ref[...] = jnp.zeros_like(hid_ref)


def _moe_down_kernel(blk_exp_ref, n_act_ref, hid_ref, wd_ref, y_ref):
    i = pl.program_id(0)

    @pl.when(i < n_act_ref[0])
    def _():
        y_ref[...] = jnp.dot(hid_ref[...], wd_ref[0, 0], preferred_element_type=F32)

    @pl.when(i >= n_act_ref[0])
    def _():
        y_ref[...] = jnp.zeros_like(y_ref)


CAST_TILE_BYTES = 4 * 1024 * 1024


def _cast_kernel(x_ref, o_ref):
    o_ref[...] = x_ref[...].astype(o_ref.dtype)


def _to_bf16(w):
    k, n = w.shape[-2:]
    w3 = w.reshape(-1, k, n)
    tk = next(t for t in range(k, 0, -16) if k % t == 0 and t * n * 4 <= CAST_TILE_BYTES)
    tile = pl.BlockSpec((1, tk, n), lambda e, j: (e, j, 0))
    out = pl.pallas_call(
        _cast_kernel, grid=(w3.shape[0], k // tk), in_specs=[tile], out_specs=tile,
        out_shape=jax.ShapeDtypeStruct(w3.shape, BF16),
        compiler_params=_params("parallel", "parallel"), name="cast_bf16",
    )(w3)
    return out.reshape(w.shape)


def _moe_experts(xs, blk_exp, n_act, layer, wg, wu, wd):
    cap, dm = xs.shape
    de = wg.shape[-1]
    n_blocks = cap // MOE_BLOCK
    wmap = lambda i, be, na: (layer, be[i], 0, 0)
    rmap = lambda i, be, na: (i, 0)
    hid = pl.pallas_call(
        _moe_up_kernel,
        grid_spec=pltpu.PrefetchScalarGridSpec(
            num_scalar_prefetch=2, grid=(n_blocks,),
            in_specs=[pl.BlockSpec((MOE_BLOCK, dm), rmap),
                      pl.BlockSpec((1, 1, dm, de), wmap),
                      pl.BlockSpec((1, 1, dm, de), wmap)],
            out_specs=pl.BlockSpec((MOE_BLOCK, de), rmap)),
        out_shape=jax.ShapeDtypeStruct((cap, de), BF16),
        compiler_params=_params("arbitrary"),
        name="moe_gate_up",
    )(blk_exp, n_act, xs, wg, wu)
    return pl.pallas_call(
        _moe_down_kernel,
        grid_spec=pltpu.PrefetchScalarGridSpec(
            num_scalar_prefetch=2, grid=(n_blocks,),
            in_specs=[pl.BlockSpec((MOE_BLOCK, de), rmap),
                      pl.BlockSpec((1, 1, de, dm), wmap)],
            out_specs=pl.BlockSpec((MOE_BLOCK, dm), rmap)),
        out_shape=jax.ShapeDtypeStruct((cap, dm), F32),
        compiler_params=_params("arbitrary"),
        name="moe_down",
    )(blk_exp, n_act, hid, wd)


INT32_MIN = -2 ** 31
BISECT_UNROLL = 8


def _order_key(x):
    b = lax.bitcast_convert_type(x + 0.0, jnp.int32)
    return b ^ ((b >> 31) & 0x7FFFFFFF)


def _count(mask):
    return jnp.sum(jnp.where(mask, 1.0, 0.0), axis=-1, keepdims=True)


def _topk_mask(key, pos, k, n_pos_bits):
    kf = float(k)
    thr = jnp.where(_count(key >= 0) >= kf, 0, INT32_MIN).astype(jnp.int32)

    def value_step(i, thr):
        cand = thr + jnp.left_shift(jnp.int32(1), 30 - i)
        return jnp.where(_count(key >= cand) >= kf, cand, thr)

    thr = lax.fori_loop(0, 31, value_step, thr, unroll=BISECT_UNROLL)
    above = key > thr
    n_pos = 1 << n_pos_bits
    tied_pos = jnp.where(key == thr, pos, n_pos)
    need = kf - _count(above)

    def take_lowest(_):
        def pos_step(i, last):
            cand = last + jnp.left_shift(jnp.int32(1), n_pos_bits - 1 - i)
            return jnp.where(_count(tied_pos < cand) < need, cand, last)

        return lax.fori_loop(0, n_pos_bits, pos_step, jnp.zeros_like(thr), unroll=BISECT_UNROLL)

    def take_all(_):
        return jnp.full_like(thr, n_pos - 1)

    surplus = jnp.max(_count(tied_pos < n_pos) - need)
    last = lax.cond(surplus > 0.0, take_lowest, take_all, None)
    return above | (tied_pos <= last)


def _topk_mask_by_rank(key, pos, k, n):
    rank = jnp.zeros(key.shape, F32)
    for m in range(n):
        rank = rank + jnp.where(key[:, m:m + 1] > key - jnp.where(pos > m, 1, 0), 1.0, 0.0)
    return (rank < float(k)) & (pos < n)


def _mask_bias(mask):
    return jnp.where(mask, 0.0, -jnp.inf)


def _softmax_weights(s, bias):
    s = s + bias
    m = jnp.max(s, axis=-1, keepdims=True)
    m = jnp.where(jnp.isfinite(m), m, 0.0)
    e = jnp.exp(s - m)
    return e, jnp.maximum(jnp.sum(e, axis=-1, keepdims=True), 1e-30)


def _dot_nt(a, b):
    return lax.dot_general(a, b, (((1,), (1,)), ((), ())), preferred_element_type=F32)


def _attend(q, k, v, bias, scale):
    e, denom = _softmax_weights(_dot_nt(q, k) * scale, bias)
    return jnp.dot(e.astype(BF16), v, preferred_element_type=F32) / denom


KEY_TILE = 512


def _for_causal_extent(q_end, seq, body):
    for klen in range(KEY_TILE, seq + 1, KEY_TILE):
        pl.when((q_end > klen - KEY_TILE) & (q_end <= klen))(functools.partial(body, klen))


class _Packed:
    def __init__(self, d_model):
        self.qw = d_model // 4
        self.gw = B_KV_GROUPS * HEAD_DIM
        self.iqw = A_IDX_HEADS * A_IDX_DIM
        self.cw = d_model // 2
        self.n_bheads = self.qw // HEAD_DIM
        self.gate_cols = 3 * self.n_bheads // B_KV_GROUPS
        qw, gw = self.qw, self.gw
        self.in_sizes = (qw, HEAD_DIM, HEAD_DIM, self.iqw, A_IDX_DIM, A_IDX_HEADS, qw,
                         gw, gw, gw, gw, gw, gw, 3 * self.n_bheads, self.cw)
        self.ra_tile = 512
        o = 0
        self.a_q = o; o += qw
        self.b_q = o; o += qw
        self.a_iq = o; o += self.iqw
        self.b_ks = o; o += gw
        self.b_kw = o; o += gw
        self.a_k = o; o += HEAD_DIM
        self.a_ik = o; o += A_IDX_DIM
        self.ra_width = -(-o // self.ra_tile) * self.ra_tile
        self.b_vs, self.b_vw, self.a_v = 0, gw, 2 * gw
        self.pc_width = 2 * gw + HEAD_DIM
        self.c_u = 0
        self.b_vc = self.cw
        self.a_iw = self.b_vc + gw
        self.b_g = self.a_iw + LANES
        self.pd_width = self.b_g + B_KV_GROUPS * LANES
        self.pd_tile = next(t for t in (896, 768, 640, 512, 384, 256, 128) if self.pd_width % t == 0)
        assert qw % 512 == 0 and self.iqw % 512 == 0 and self.a_iq % 512 == 0

    def weights(self, w_in):
        cuts = np.cumsum(self.in_sizes)[:-1].tolist()
        (a_q, a_k, a_v, a_iq, a_ik, a_iw, b_q, b_kc, b_vc, b_ks, b_vs, b_kw, b_vw, b_g,
         c_u) = jnp.split(w_in.astype(BF16), cuts, axis=1)
        zeros = lambda n: jnp.zeros((w_in.shape[0], n), BF16)
        ra = [a_q, b_q, a_iq, b_ks, b_kw, a_k, a_ik]
        ra.append(zeros(self.ra_width - sum(t.shape[1] for t in ra)))
        pd = [c_u, b_vc, a_iw, zeros(LANES - a_iw.shape[1])]
        for g in range(B_KV_GROUPS):
            pd += [b_g[:, g * self.gate_cols:(g + 1) * self.gate_cols], zeros(LANES - self.gate_cols)]
        cat = lambda ts: jnp.concatenate(ts, axis=1)
        return cat(ra), b_kc, cat([b_vs, b_vw, a_v]), cat(pd)


def _rope_tables(positions):
    half = ROT_DIM // 2
    inv = ROPE_THETA ** (-jnp.arange(0, ROT_DIM, 2, dtype=F32) / ROT_DIM)
    ang = positions.astype(F32).reshape(-1, 1) * inv
    cos, sin = jnp.cos(ang), jnp.sin(ang)
    n = ang.shape[0]
    ones, zeros, z16 = jnp.ones((n, HEAD_DIM - ROT_DIM), F32), jnp.zeros((n, HEAD_DIM - ROT_DIM), F32), jnp.zeros((n, half), F32)
    return (jnp.concatenate([cos, cos, ones], axis=1),
            jnp.concatenate([z16, sin, zeros], axis=1),
            jnp.concatenate([-sin, z16, zeros], axis=1))


def _dsa_kernel(q_ref, iq_ref, iw_ref, k_ref, v_ref, ik_ref, o_ref, *, n_keep, n_heads):
    seq = k_ref.shape[1]
    start = pl.program_id(1) * A_QBLOCK
    qpos = start + lax.broadcasted_iota(jnp.int32, (A_QBLOCK, 1), 0)

    def block(klen):
        kpos = lax.broadcasted_iota(jnp.int32, (1, klen), 1)
        causal = kpos <= qpos
        ik = ik_ref[0, :klen, :]
        iw = iw_ref[0] * (A_IDX_HEADS * A_IDX_DIM) ** -0.5
        score = jnp.zeros((A_QBLOCK, klen), F32)
        for h in range(A_IDX_HEADS):
            rel = _dot_nt(iq_ref[0, :, h * A_IDX_DIM:(h + 1) * A_IDX_DIM], ik)
            score = score + iw[:, h:h + 1] * jnp.maximum(rel, 0.0)
        score = jnp.where(causal, score, -jnp.inf)
        sel = _topk_mask(_order_key(score), kpos, n_keep, max(1, (klen - 1).bit_length())) & causal
        bias = _mask_bias(sel)
        k = k_ref[0, :klen, :]
        v = v_ref[0, :klen, :]
        for h in range(n_heads):
            hs = slice(h * HEAD_DIM, (h + 1) * HEAD_DIM)
            o_ref[0, :, hs] = _attend(q_ref[0, :, hs], k, v, bias, HEAD_DIM ** -0.5).astype(o_ref.dtype)

    _for_causal_extent(start + A_QBLOCK, seq, block)


def _dsa_mixer(ra, pc, pd, lay):
    bsz, seq, _ = ra.shape
    n_keep = min(A_TOPK_MAX, seq // 4)
    assert seq % KEY_TILE == 0 and KEY_TILE >= n_keep
    qb = lambda width, off: pl.BlockSpec((1, A_QBLOCK, width), lambda b, i: (b, i, off // width))
    kb = lambda width, off: pl.BlockSpec((1, seq, width), lambda b, i: (b, 0, off // width))
    return pl.pallas_call(
        functools.partial(_dsa_kernel, n_keep=n_keep, n_heads=lay.qw // HEAD_DIM),
        grid=(bsz, seq // A_QBLOCK),
        in_specs=[qb(lay.qw, lay.a_q), qb(lay.iqw, lay.a_iq), qb(LANES, lay.a_iw),
                  kb(HEAD_DIM, lay.a_k), kb(HEAD_DIM, lay.a_v), kb(A_IDX_DIM, lay.a_ik)],
        out_specs=pl.BlockSpec((1, A_QBLOCK, lay.qw), lambda b, i: (b, i, 0)),
        out_shape=jax.ShapeDtypeStruct((bsz, seq, lay.qw), BF16),
        compiler_params=_params("parallel", "arbitrary"),
        name="dsa_mixer",
    )(ra, ra, pd, ra, pc, ra)


NSA_QBLOCK = 128
GELU_C = math.sqrt(2.0 / math.pi)


def _gelu_tanh(x):
    return 0.5 * x * (1.0 + jnp.tanh(GELU_C * (x + 0.044715 * (x * x * x))))


def _nsa_compress_kernel(kc_ref, vc_ref, pe_ref, w1_ref, w2_ref, ko_ref, vo_ref):
    n_chunk = ko_ref.shape[2]
    half = B_CMP_LEN // 2
    assert B_CMP_STRIDE == half
    row = lax.broadcasted_iota(jnp.int32, (n_chunk, 1), 0)
    for t, (src, dst) in enumerate(((kc_ref, ko_ref), (vc_ref, vo_ref))):
        lo = jnp.zeros((n_chunk, w1_ref.shape[-1]), F32)
        hi = jnp.zeros((n_chunk, w1_ref.shape[-1]), F32)
        for l in range(half):
            x = src[0, pl.ds(l, n_chunk, stride=B_CMP_STRIDE), :]
            lo = lo + jnp.dot((x + pe_ref[t, l:l + 1, :]).astype(BF16),
                              w1_ref[t, l * HEAD_DIM:(l + 1) * HEAD_DIM, :], preferred_element_type=F32)
            hi = hi + jnp.dot((x + pe_ref[t, half + l:half + l + 1, :]).astype(BF16),
                              w1_ref[t, (half + l) * HEAD_DIM:(half + l + 1) * HEAD_DIM, :],
                              preferred_element_type=F32)
        pre = lo + pltpu.roll(hi, n_chunk - 1, 0)
        out = jnp.dot(_gelu_tanh(pre).astype(BF16), w2_ref[t], preferred_element_type=F32)
        dst[0, 0] = jnp.where(row < n_chunk - 1, out, 0.0).astype(dst.dtype)


def _nsa_compress(rb, pd, lay, phi_pe, phi_w1, phi_w2):
    bsz, seq, _ = rb.shape
    n_chunk = seq // B_CMP_STRIDE
    cmap = lambda b, g: (0, 0, 0)
    omap = lambda b, g: (b, g, 0, 0)
    vc_blk = lay.b_vc // HEAD_DIM
    out_sds = jax.ShapeDtypeStruct((bsz, B_KV_GROUPS, n_chunk, HEAD_DIM), BF16)
    return pl.pallas_call(
        _nsa_compress_kernel,
        grid=(bsz, B_KV_GROUPS),
        in_specs=[pl.BlockSpec((1, seq, HEAD_DIM), lambda b, g: (b, 0, g)),
                  pl.BlockSpec((1, seq, HEAD_DIM), lambda b, g: (b, 0, vc_blk + g)),
                  pl.BlockSpec(phi_pe.shape, cmap),
                  pl.BlockSpec(phi_w1.shape, cmap),
                  pl.BlockSpec(phi_w2.shape, cmap)],
        out_specs=[pl.BlockSpec((1, 1, n_chunk, HEAD_DIM), omap),
                   pl.BlockSpec((1, 1, n_chunk, HEAD_DIM), omap)],
        out_shape=[out_sds, out_sds],
        compiler_params=_params("parallel", "parallel"),
        name="nsa_compress",
    )(rb, pd, phi_pe, phi_w1.astype(BF16), phi_w2.astype(BF16))


def _nsa_kernel(q_ref, gate_ref, kcmp_ref, vcmp_ref, ks_ref, vs_ref, kw_ref, vw_ref, ovl_ref, expand_ref,
                o_ref, *, n_sel, heads_per_group):
    seq = ks_ref.shape[1]
    n_cmp = kcmp_ref.shape[2]
    n_blk = seq // B_SEL_LEN
    scale = HEAD_DIM ** -0.5
    start = pl.program_id(2) * NSA_QBLOCK
    qpos = start + lax.broadcasted_iota(jnp.int32, (NSA_QBLOCK, 1), 0)
    heads = [q_ref[0, :, r * HEAD_DIM:(r + 1) * HEAD_DIM] for r in range(heads_per_group)]

    cend = lax.broadcasted_iota(jnp.int32, (1, n_cmp), 1) * B_CMP_STRIDE + (B_CMP_LEN - 1)
    cmask = _mask_bias(cend <= qpos)
    kcmp = kcmp_ref[0, 0]
    vcmp = vcmp_ref[0, 0]
    o_cmp = []
    psum = jnp.zeros((NSA_QBLOCK, n_cmp), F32)
    for q in heads:
        e, denom = _softmax_weights(_dot_nt(q, kcmp) * scale, cmask)
        p = e / denom
        psum = psum + p
        o_cmp.append(jnp.dot(p.astype(BF16), vcmp, preferred_element_type=F32))
    p_hi = psum.astype(BF16)
    p_lo = (psum - p_hi.astype(F32)).astype(BF16)
    ovl = ovl_ref[...]
    imp = (jnp.dot(p_hi, ovl, preferred_element_type=F32) + jnp.dot(p_lo, ovl, preferred_element_type=F32))
    blk = lax.broadcasted_iota(jnp.int32, (1, imp.shape[1]), 1)
    cur = qpos // B_SEL_LEN
    imp = jnp.where((blk == 0) | (blk == cur) | (blk == cur - 1), jnp.inf,
                    jnp.where(blk > cur, -jnp.inf, imp))
    blk_sel = _topk_mask_by_rank(_order_key(imp), blk, n_sel, n_blk)

    n_win = B_WIN + NSA_QBLOCK
    kstart = pl.multiple_of(jnp.maximum(start - B_WIN, 0), NSA_QBLOCK)
    wpos = kstart + lax.broadcasted_iota(jnp.int32, (1, n_win), 1)
    win_bias = _mask_bias((wpos <= qpos) & (wpos > qpos - B_WIN))
    kw = kw_ref[0, pl.ds(kstart, n_win), :]
    vw = vw_ref[0, pl.ds(kstart, n_win), :]
    o_win = [_attend(q, kw, vw, win_bias, scale) for q in heads]

    gates = jax.nn.sigmoid(gate_ref[0])
    sel_rows = jnp.where(blk_sel, 1.0, 0.0).astype(BF16)

    def selected_and_combine(klen):
        kpos = lax.broadcasted_iota(jnp.int32, (1, klen), 1)
        in_sel = jnp.dot(sel_rows, expand_ref[:, :klen], preferred_element_type=F32)
        sel_bias = _mask_bias((in_sel > 0.5) & (kpos <= qpos))
        ks = ks_ref[0, :klen, :]
        vs = vs_ref[0, :klen, :]
        for r, q in enumerate(heads):
            gr = gates[:, 3 * r:3 * r + 3]
            out = (gr[:, 0:1] * o_cmp[r] + gr[:, 1:2] * _attend(q, ks, vs, sel_bias, scale)
                   + gr[:, 2:3] * o_win[r])
            o_ref[0, :, r * HEAD_DIM:(r + 1) * HEAD_DIM] = out.astype(o_ref.dtype)

    _for_causal_extent(start + NSA_QBLOCK, seq, selected_and_combine)


def _nsa_mixer(ra, rb, pc, pd, lay, phi_pe, phi_w1, phi_w2):
    bsz, seq, _ = ra.shape
    hpg = lay.n_bheads // B_KV_GROUPS
    gq = hpg * HEAD_DIM
    assert seq % KEY_TILE == 0 and seq >= B_WIN + NSA_QBLOCK and KEY_TILE % NSA_QBLOCK == 0
    kcmp, vcmp = _nsa_compress(rb, pd, lay, phi_pe, phi_w1, phi_w2)
    n_cmp = seq // B_CMP_STRIDE
    n_blk = seq // B_SEL_LEN
    n_sel = min(B_SEL_N, n_blk)
    assert n_blk <= LANES
    c_lo = np.arange(n_cmp)[:, None] * B_CMP_STRIDE
    b_lo = np.arange(LANES)[None, :] * B_SEL_LEN
    overlap = ((c_lo < b_lo + B_SEL_LEN) & (c_lo + B_CMP_LEN - 1 >= b_lo) & (np.arange(LANES)[None, :] < n_blk))
    expand = (np.arange(seq)[None, :] // B_SEL_LEN) == np.arange(LANES)[:, None]
    qb = lambda width, off: pl.BlockSpec((1, NSA_QBLOCK, width), lambda b, g, i: (b, i, off // width + g))
    kb = lambda off: pl.BlockSpec((1, seq, HEAD_DIM), lambda b, g, i: (b, 0, off // HEAD_DIM + g))
    cmap = lambda b, g, i: (b, g, 0, 0)
    const = lambda b, g, i: (0, 0)
    return pl.pallas_call(
        functools.partial(_nsa_kernel, n_sel=n_sel, heads_per_group=hpg),
        grid=(bsz, B_KV_GROUPS, seq // NSA_QBLOCK),
        in_specs=[qb(gq, lay.b_q), qb(LANES, lay.b_g),
                  pl.BlockSpec((1, 1, n_cmp, HEAD_DIM), cmap),
                  pl.BlockSpec((1, 1, n_cmp, HEAD_DIM), cmap),
                  kb(lay.b_ks), kb(lay.b_vs), kb(lay.b_kw), kb(lay.b_vw),
                  pl.BlockSpec((n_cmp, LANES), const),
                  pl.BlockSpec((LANES, seq), const)],
        out_specs=pl.BlockSpec((1, NSA_QBLOCK, gq), lambda b, g, i: (b, i, g)),
        out_shape=jax.ShapeDtypeStruct((bsz, seq, lay.qw), BF16),
        compiler_params=_params("parallel", "parallel", "arbitrary"),
        name="nsa_mixer",
    )(ra, pd, kcmp, vcmp, ra, pc, ra, pc, jnp.asarray(overlap, BF16), jnp.asarray(expand, BF16))


S5_TIME_BLOCK = 256
S5_UNROLL = 8


def _s5_scan_kernel(u_ref, bcat_ref, ccat_ref, a_ref, d_ref, z_ref, xs_ref, state_ref):
    tt, bsz, lanes = u_ref.shape
    ns = a_ref.shape[-1] // 2

    @pl.when(pl.program_id(1) == 0)
    def _():
        state_ref[...] = jnp.zeros_like(state_ref)

    u = u_ref[...].reshape(tt * bsz, lanes)
    xs_ref[...] = jnp.dot(u.astype(BF16), bcat_ref[0], preferred_element_type=F32)
    a_re = jnp.broadcast_to(a_ref[0, :, :ns], (bsz, ns))
    a_im = jnp.broadcast_to(a_ref[0, :, ns:], (bsz, ns))

    def step(t, carry):
        s_re, s_im = carry
        rows = pl.ds(pl.multiple_of(t * bsz, bsz), bsz)
        n_re = a_re * s_re - a_im * s_im + xs_ref[rows, :ns]
        n_im = a_re * s_im + a_im * s_re + xs_ref[rows, ns:]
        xs_ref[rows, :ns] = n_re
        xs_ref[rows, ns:] = n_im
        return n_re, n_im

    s_re, s_im = lax.fori_loop(0, tt, step, (state_ref[:, :ns], state_ref[:, ns:]), unroll=S5_UNROLL)
    state_ref[:, :ns] = s_re
    state_ref[:, ns:] = s_im
    y = jnp.dot(xs_ref[...].astype(BF16), ccat_ref[0], preferred_element_type=F32) + d_ref[0] * u
    z_ref[...] = _gelu_tanh(y).reshape(tt, bsz, lanes)


def _s5_glu_kernel(za_ref, zt_ref, w_ref, b_ref, o_ref):
    zz = jnp.dot(za_ref[...].astype(BF16), w_ref[...], preferred_element_type=F32) + b_ref[...]
    o_ref[...] = (zt_ref[...] * jax.nn.sigmoid(zz)).astype(o_ref.dtype)


def _s5_mixer(u_tm, lam_re, lam_im, log_dt, b_re, b_im, c_re, c_im, d, glu_w, glu_b):
    seq, bsz, width = u_tm.shape
    gpc = LANES // C_GROUP
    n_col = width // LANES
    ns = gpc * C_STATE
    tt = min(S5_TIME_BLOCK, seq)
    assert width % LANES == 0 and seq % tt == 0 and bsz % 8 == 0
    dt = jnp.exp(log_dt)[:, None]
    mag = jnp.exp(lam_re * dt)
    a_re, a_im = mag * jnp.cos(lam_im * dt), mag * jnp.sin(lam_im * dt)
    den = lam_re * lam_re + lam_im * lam_im
    co_re = ((a_re - 1.0) * lam_re + a_im * lam_im) / den
    co_im = (a_im * lam_re - (a_re - 1.0) * lam_im) / den
    bb_re = co_re[..., None] * b_re - co_im[..., None] * b_im
    bb_im = co_re[..., None] * b_im + co_im[..., None] * b_re
    eye = jnp.eye(gpc, dtype=F32)

    def block_diag(m):
        r, c = m.shape[-2:]
        return jnp.einsum('ngrc,gk->ngrkc', m, eye).reshape(n_col, gpc * r, gpc * c)

    to_in = lambda m: block_diag(m.reshape(n_col, gpc, C_STATE, C_GROUP).transpose(0, 1, 3, 2))
    to_out = lambda m: block_diag(m.reshape(n_col, gpc, C_GROUP, C_STATE).transpose(0, 1, 3, 2))
    bcat = jnp.concatenate([to_in(bb_re), to_in(bb_im)], axis=-1).astype(BF16)
    ccat = jnp.concatenate([to_out(c_re), to_out(-c_im)], axis=-2).astype(BF16)
    acat = jnp.concatenate([a_re.reshape(n_col, 1, ns), a_im.reshape(n_col, 1, ns)], axis=-1)
    z = pl.pallas_call(
        _s5_scan_kernel,
        grid=(n_col, seq // tt),
        in_specs=[pl.BlockSpec((tt, bsz, LANES), lambda c, t: (t, 0, c)),
                  pl.BlockSpec((1, LANES, 2 * ns), lambda c, t: (c, 0, 0)),
                  pl.BlockSpec((1, 2 * ns, LANES), lambda c, t: (c, 0, 0)),
                  pl.BlockSpec((1, 1, 2 * ns), lambda c, t: (c, 0, 0)),
                  pl.BlockSpec((1, 1, LANES), lambda c, t: (c, 0, 0))],
        out_specs=pl.BlockSpec((tt, bsz, LANES), lambda c, t: (t, 0, c)),
        out_shape=jax.ShapeDtypeStruct((seq, bsz, width), F32),
        scratch_shapes=[pltpu.VMEM((tt * bsz, 2 * ns), F32), pltpu.VMEM((bsz, 2 * ns), F32)],
        compiler_params=_params("parallel", "arbitrary"),
        name="s5_scan",
    )(u_tm, bcat, ccat, acat, d.reshape(n_col, 1, LANES))
    z = z.reshape(seq * bsz, width)
    tm, tn = min(ROW_TILE, seq * bsz), min(512, width)
    return pl.pallas_call(
        _s5_glu_kernel,
        grid=(seq * bsz // tm, width // tn),
        in_specs=[pl.BlockSpec((tm, width), lambda i, j: (i, 0)),
                  pl.BlockSpec((tm, tn), lambda i, j: (i, j)),
                  pl.BlockSpec((width, tn), lambda i, j: (0, j)),
                  pl.BlockSpec((1, tn), lambda i, j: (0, j))],
        out_specs=pl.BlockSpec((tm, tn), lambda i, j: (i, j)),
        out_shape=jax.ShapeDtypeStruct((seq * bsz, width), BF16),
        compiler_params=_params("parallel", "parallel"),
        name="s5_glu",
    )(z, z, glu_w.astype(BF16), glu_b.reshape(1, width))


def _mixing_sublayer(hb, bsz, seq, rope, lay, w_in, phi_pe, phi_w1, phi_w2, lam_re, lam_im, log_dt,
                     b_re, b_im, c_re, c_im, s5_d, glu_w, glu_b, w_out):
    w_ra, w_rb, w_pc, w_pd = lay.weights(w_in)
    view = lambda t: t.reshape(bsz, seq, t.shape[-1])
    ra = view(_matmul(hb, w_ra, lay.ra_tile, BF16, rope))
    rb = view(_matmul(hb, w_rb, w_rb.shape[1], F32, rope))
    pc = view(_matmul(hb, w_pc, w_pc.shape[1], BF16))
    pd = view(_matmul(hb, w_pd, lay.pd_tile, F32))
    o_a = _dsa_mixer(ra, pc, pd, lay)
    o_b = _nsa_mixer(ra, rb, pc, pd, lay, phi_pe, phi_w1, phi_w2)
    u_tm = pd[:, :, lay.c_u:lay.c_u + lay.cw].transpose(1, 0, 2)
    o_c = _s5_mixer(u_tm, lam_re, lam_im, log_dt, b_re, b_im, c_re, c_im, s5_d, glu_w, glu_b)
    o_c = o_c.reshape(seq, bsz, lay.cw).transpose(1, 0, 2).reshape(bsz * seq, lay.cw)
    return _out_proj(o_a.reshape(bsz * seq, -1), o_b.reshape(bsz * seq, -1), o_c, w_out.astype(BF16))


ROUTER_ROWS = 256


def _split_bf16(x):
    hi = x.astype(BF16)
    return hi, (x - hi.astype(F32)).astype(BF16)


def _router_kernel(x_ref, wh_ref, wl_ref, bias_ref, eid_ref, gate_ref):
    x_hi, x_lo = _split_bf16(x_ref[...])
    wh = wh_ref[...]
    logits = (jnp.dot(x_hi, wh, preferred_element_type=F32) + jnp.dot(x_hi, wl_ref[...], preferred_element_type=F32)
              + jnp.dot(x_lo, wh, preferred_element_type=F32))
    aff = jax.nn.sigmoid(logits)
    lane_i = lax.broadcasted_iota(jnp.int32, (1, LANES), 1)
    lane = lane_i.astype(F32)
    grp = (lane_i // EXPERTS_PER_GROUP).astype(F32)
    sel = jnp.where(lane_i < N_EXPERTS, aff + bias_ref[...], -jnp.inf)

    def top2(v):
        m1 = jnp.max(v, axis=-1, keepdims=True)
        i1 = jnp.min(jnp.where(v == m1, lane, float(LANES)), axis=-1, keepdims=True)
        v2 = jnp.where(lane == i1, -jnp.inf, v)
        m2 = jnp.max(v2, axis=-1, keepdims=True)
        i2 = jnp.min(jnp.where(v2 == m2, lane, float(LANES)), axis=-1, keepdims=True)
        return m1, i1, m2, i2

    best_score = best_grp = None
    for g in range(N_EXPERT_GROUPS):
        m1, _, m2, _ = top2(jnp.where(grp == float(g), sel, -jnp.inf))
        score = m1 + m2
        if g == 0:
            best_score, best_grp = score, jnp.zeros_like(score)
        else:
            better = score > best_score
            best_score = jnp.where(better, score, best_score)
            best_grp = jnp.where(better, float(g), best_grp)
    _, i1, _, i2 = top2(jnp.where(grp == best_grp, sel, -jnp.inf))
    w1 = jnp.sum(jnp.where(lane == i1, aff, 0.0), axis=-1, keepdims=True)
    w2 = jnp.sum(jnp.where(lane == i2, aff, 0.0), axis=-1, keepdims=True)
    tot = w1 + w2
    eid_ref[...] = jnp.where(lane_i == 0, i1, jnp.where(lane_i == 1, i2, 0.0)).astype(jnp.int32)
    gate_ref[...] = jnp.where(lane_i == 0, w1 / tot, jnp.where(lane_i == 1, w2 / tot, 0.0))


def _route(xt, router_w, router_bias):
    n_tok, dm = xt.shape
    assert n_tok % ROUTER_ROWS == 0 and N_EXPERTS <= LANES
    pad = LANES - N_EXPERTS
    wh, wl = _split_bf16(jnp.pad(router_w, ((0, 0), (0, pad))))
    bias = jnp.pad(router_bias, (0, pad)).reshape(1, LANES)
    rows = lambda w: pl.BlockSpec((ROUTER_ROWS, w), lambda i: (i, 0))
    const = lambda shape: pl.BlockSpec(shape, lambda i: (0, 0))
    eid, gate = pl.pallas_call(
        _router_kernel, grid=(n_tok // ROUTER_ROWS,),
        in_specs=[rows(dm), const((dm, LANES)), const((dm, LANES)), const((1, LANES))],
        out_specs=[rows(LANES), rows(LANES)],
        out_shape=[jax.ShapeDtypeStruct((n_tok, LANES), jnp.int32), jax.ShapeDtypeStruct((n_tok, LANES), F32)],
        compiler_params=_params("parallel"), name="moe_router",
    )(xt, wh, wl, bias)
    return eid[:, :TOP_K], gate[:, :TOP_K]


def _moe(xt, xb, router_w, router_bias, layer, w_gate, w_up, w_down):
    n_tok, dm = xt.shape
    eid, gate = _route(xt, router_w, router_bias)
    n_asg = n_tok * TOP_K
    cap = -(-n_asg // MOE_BLOCK) * MOE_BLOCK + N_EXPERTS * MOE_BLOCK
    onehot = (eid.reshape(n_asg, 1) == jnp.arange(N_EXPERTS, dtype=jnp.int32)[None, :]).astype(jnp.int32)
    seen = jnp.cumsum(onehot, axis=0)
    counts = seen[-1]
    padded = (counts + MOE_BLOCK - 1) // MOE_BLOCK * MOE_BLOCK
    pend = jnp.cumsum(padded)
    pstart = pend - padded
    dest = jnp.sum(onehot * (seen - 1 + pstart[None, :]), axis=1).astype(jnp.int32)
    tok_flat = jnp.arange(n_asg, dtype=jnp.int32) // TOP_K
    slot_tok = jnp.full((cap,), n_tok, jnp.int32).at[dest].set(tok_flat)
    n_blocks = cap // MOE_BLOCK
    blk_start = jnp.arange(n_blocks) * MOE_BLOCK
    blk_exp = jnp.minimum(jnp.sum(pend[None, :] <= blk_start[:, None], axis=1), N_EXPERTS - 1).astype(jnp.int32)
    n_act = (pend[-1] // MOE_BLOCK).astype(jnp.int32).reshape(1)
    x_pad = jnp.concatenate([xb, jnp.zeros((1, dm), BF16)], axis=0)
    y = _moe_experts(x_pad[slot_tok], blk_exp, n_act, layer, w_gate, w_up, w_down)
    pos = dest.reshape(n_tok, TOP_K)
    return y[pos[:, 0]], y[pos[:, 1]], gate


def kernel(x, positions, ln_in_g, ln_in_b, w_in, nsa_phi_pe, nsa_phi_w1, nsa_phi_w2, s5_lam_re, s5_lam_im, s5_log_dt, s5_b_re, s5_b_im, s5_c_re, s5_c_im, s5_d, s5_glu_w, s5_glu_b, w_out, ln_mix_g, ln_mix_b, router_w, router_bias, moe_w_gate, moe_w_up, moe_w_down, ln_ffn_g, ln_ffn_b):
    depth = w_in.shape[0]
    bsz, seq, d_model = x.shape
    alpha = (2 * depth) ** 0.25
    lay = _Packed(d_model)
    rope = _rope_tables(positions)
    wg_b, wu_b, wd_b = _to_bf16(moe_w_gate), _to_bf16(moe_w_up), _to_bf16(moe_w_down)
    h, hb = _layer_norm(x.reshape(bsz * seq, d_model), ln_in_g, ln_in_b)
    for l in range(depth):
        mix = _mixing_sublayer(hb, bsz, seq, rope, lay, w_in[l], nsa_phi_pe[l], nsa_phi_w1[l], nsa_phi_w2[l],
                               s5_lam_re[l], s5_lam_im[l], s5_log_dt[l], s5_b_re[l], s5_b_im[l],
                               s5_c_re[l], s5_c_im[l], s5_d[l], s5_glu_w[l], s5_glu_b[l], w_out[l])
        h, hb = _layer_norm(h, ln_mix_g[l], ln_mix_b[l], residual=mix, alpha=alpha)
        routed = _moe(h, hb, router_w, router_bias, l, wg_b, wu_b, wd_b)
        h, hb = _layer_norm(h, ln_ffn_g[l], ln_ffn_b[l], gated_pair=routed, alpha=alpha)
    return h.reshape(bsz, seq, d_model)
```

```python
import functools
import math

import numpy as np
import jax
import jax.numpy as jnp
from jax import lax
from jax.experimental import pallas as pl
from jax.experimental.pallas import tpu as pltpu

F32 = jnp.float32
BF16 = jnp.bfloat16

LANES = 128
HEAD_DIM = 128
ROT_DIM = HEAD_DIM // 4
ROPE_THETA = 500000.0
A_IDX_HEADS = 4
A_IDX_DIM = 128
A_TOPK_MAX = 256
A_QBLOCK = 128
B_KV_GROUPS = 2
B_CMP_LEN = 32
B_CMP_STRIDE = 16
B_SEL_LEN = 64
B_SEL_N = 16
B_WIN = 512
C_GROUP = 16
C_STATE = 64
N_EXPERTS = 32
N_EXPERT_GROUPS = 4
EXPERTS_PER_GROUP = N_EXPERTS // N_EXPERT_GROUPS
TOP_K = 2
MOE_BLOCK = 256
LN_EPS = 1e-5

VMEM_LIMIT_BYTES = 56 * 1024 * 1024
ROW_TILE = 1024


def _params(*semantics):
    return pltpu.CompilerParams(dimension_semantics=semantics, vmem_limit_bytes=VMEM_LIMIT_BYTES)


def _mm_kernel(a_ref, b_ref, o_ref):
    o_ref[...] = jnp.dot(a_ref[...], b_ref[...], preferred_element_type=F32).astype(o_ref.dtype)


def _mm_rope_kernel(a_ref, b_ref, c_ref, sa_ref, sb_ref, o_ref):
    y = jnp.dot(a_ref[...], b_ref[...], preferred_element_type=F32)
    half = ROT_DIM // 2
    for hb in range(y.shape[1] // HEAD_DIM):
        cols = slice(hb * HEAD_DIM, (hb + 1) * HEAD_DIM)
        yh = y[:, cols]
        out = (yh * c_ref[...] + pltpu.roll(yh, half, 1) * sa_ref[...]
               + pltpu.roll(yh, HEAD_DIM - half, 1) * sb_ref[...])
        o_ref[:, cols] = out.astype(o_ref.dtype)


def _matmul(a, b, tn, out_dtype, rope=None):
    m, k = a.shape
    n = b.shape[1]
    tm = min(ROW_TILE, m)
    assert m % tm == 0 and n % tn == 0
    in_specs = [pl.BlockSpec((tm, k), lambda i, j: (i, 0)),
                pl.BlockSpec((k, tn), lambda i, j: (0, j))]
    args = (a, b)
    if rope is not None:
        in_specs += [pl.BlockSpec((tm, HEAD_DIM), lambda i, j: (i, 0))] * 3
        args += tuple(rope)
    return pl.pallas_call(
        _mm_kernel if rope is None else _mm_rope_kernel,
        grid=(m // tm, n // tn),
        in_specs=in_specs,
        out_specs=pl.BlockSpec((tm, tn), lambda i, j: (i, j)),
        out_shape=jax.ShapeDtypeStruct((m, n), out_dtype),
        compiler_params=_params("parallel", "parallel"),
        name="in_proj" if rope is None else "in_proj_rope",
    )(*args)


def _out_proj_kernel(a0_ref, a1_ref, a2_ref, w_ref, o_ref):
    k0 = a0_ref.shape[1]
    k1 = k0 + a1_ref.shape[1]
    acc = jnp.dot(a0_ref[...], w_ref[:k0, :], preferred_element_type=F32)
    acc = acc + jnp.dot(a1_ref[...], w_ref[k0:k1, :], preferred_element_type=F32)
    o_ref[...] = acc + jnp.dot(a2_ref[...], w_ref[k1:, :], preferred_element_type=F32)


def _out_proj(a0, a1, a2, w, tn=512):
    m = a0.shape[0]
    k, n = w.shape
    tm = min(ROW_TILE, m)
    assert a0.shape[1] + a1.shape[1] + a2.shape[1] == k and m % tm == 0 and n % tn == 0
    amap = lambda i, j: (i, 0)
    return pl.pallas_call(
        _out_proj_kernel,
        grid=(m // tm, n // tn),
        in_specs=[pl.BlockSpec((tm, a0.shape[1]), amap), pl.BlockSpec((tm, a1.shape[1]), amap),
                  pl.BlockSpec((tm, a2.shape[1]), amap), pl.BlockSpec((k, tn), lambda i, j: (0, j))],
        out_specs=pl.BlockSpec((tm, tn), lambda i, j: (i, j)),
        out_shape=jax.ShapeDtypeStruct((m, n), F32),
        compiler_params=_params("parallel", "parallel"),
        name="out_proj",
    )(a0, a1, a2, w)


LN_ROWS = 256


def _ln_store(v, g_ref, b_ref, o_ref, ob_ref):
    mu = jnp.mean(v, axis=-1, keepdims=True)
    c = v - mu
    var = jnp.mean(c * c, axis=-1, keepdims=True)
    out = c * lax.rsqrt(var + LN_EPS) * g_ref[...] + b_ref[...]
    o_ref[...] = out
    ob_ref[...] = out.astype(BF16)


def _ln_kernel(x_ref, g_ref, b_ref, o_ref, ob_ref):
    _ln_store(x_ref[...], g_ref, b_ref, o_ref, ob_ref)


def _ln_residual_kernel(x_ref, r_ref, g_ref, b_ref, o_ref, ob_ref, *, alpha):
    _ln_store(alpha * x_ref[...] + r_ref[...], g_ref, b_ref, o_ref, ob_ref)


def _ln_gated_pair_kernel(x_ref, y0_ref, y1_ref, w_ref, g_ref, b_ref, o_ref, ob_ref, *, alpha):
    w = w_ref[...]
    ffn = w[:, 0:1] * y0_ref[...] + w[:, 1:2] * y1_ref[...]
    _ln_store(alpha * x_ref[...] + ffn, g_ref, b_ref, o_ref, ob_ref)


def _layer_norm(x, g, b, residual=None, alpha=1.0, gated_pair=None):
    m, d = x.shape
    rows = pl.BlockSpec((LN_ROWS, d), lambda i: (i, 0))
    vec = pl.BlockSpec((1, d), lambda i: (0, 0))
    assert m % LN_ROWS == 0
    if gated_pair is not None:
        y0, y1, w = gated_pair
        body = functools.partial(_ln_gated_pair_kernel, alpha=alpha)
        ins = [rows, rows, rows, pl.BlockSpec((LN_ROWS, w.shape[1]), lambda i: (i, 0)), vec, vec]
        args = (x, y0, y1, w, g.reshape(1, d), b.reshape(1, d))
    elif residual is None:
        body, ins, args = _ln_kernel, [rows, vec, vec], (x, g.reshape(1, d), b.reshape(1, d))
    else:
        body = functools.partial(_ln_residual_kernel, alpha=alpha)
        ins, args = [rows, rows, vec, vec], (x, residual, g.reshape(1, d), b.reshape(1, d))
    return pl.pallas_call(
        body, grid=(m // LN_ROWS,), in_specs=ins, out_specs=[rows, rows],
        out_shape=[jax.ShapeDtypeStruct((m, d), F32), jax.ShapeDtypeStruct((m, d), BF16)],
        compiler_params=_params("parallel"), name="layer_norm",
    )(*args)


MOE_SPLIT = 2


def _moe_up_kernel(blk_exp_ref, first_ref, n_act_ref, x_ref, wg_ref, wu_ref, hid_ref, wres_ref, acc_ref):
    i, k = pl.program_id(0), pl.program_id(1)
    active = i < n_act_ref[0]

    @pl.when(active & (first_ref[i] == 1))
    def _():
        wres_ref[0, k] = wg_ref[0, 0].astype(BF16)
        wres_ref[1, k] = wu_ref[0, 0].astype(BF16)

    @pl.when(active)
    def _():
        x = x_ref[...].astype(BF16)
        g = jnp.dot(x, wres_ref[0, k], preferred_element_type=F32)
        u = jnp.dot(x, wres_ref[1, k], preferred_element_type=F32)

        @pl.when(k == 0)
        def _():
            acc_ref[0] = g
            acc_ref[1] = u

        @pl.when(k > 0)
        def _():
            acc_ref[0] += g
            acc_ref[1] += u

        @pl.when(k == MOE_SPLIT - 1)
        def _():
            g_all = acc_ref[0]
            hid_ref[...] = (g_all * jax.nn.sigmoid(g_all) * acc_ref[1]).astype(hid_ref.dtype)

    @pl.when(jnp.logical_not(active) & (k == MOE_SPLIT - 1))
    def _():
        hid_ref[...] = jnp.zeros_like(hid_ref)


def _moe_down_kernel(blk_exp_ref, first_ref, n_act_ref, hid_ref, wd_ref, y_ref, wres_ref):
    i, n = pl.program_id(0), pl.program_id(1)
    active = i < n_act_ref[0]

    @pl.when(active & (first_ref[i] == 1))
    def _():
        wres_ref[n] = wd_ref[0, 0].astype(BF16)

    @pl.when(active)
    def _():
        y_ref[...] = jnp.dot(hid_ref[...], wres_ref[n], preferred_element_type=F32)

    @pl.when(jnp.logical_not(active))
    def _():
        y_ref[...] = jnp.zeros_like(y_ref)


def _moe_experts(xs, blk_exp, blk_first, n_act, layer, wg, wu, wd):
    cap, dm = xs.shape
    de = wg.shape[-1]
    n_blocks = cap // MOE_BLOCK
    tk, tn = dm // MOE_SPLIT, dm // MOE_SPLIT
    tile = lambda j, first: jnp.where(first == 1, j, MOE_SPLIT - 1)
    hid = pl.pallas_call(
        _moe_up_kernel,
        grid_spec=pltpu.PrefetchScalarGridSpec(
            num_scalar_prefetch=3, grid=(n_blocks, MOE_SPLIT),
            in_specs=[pl.BlockSpec((MOE_BLOCK, tk), lambda i, k, be, bf, na: (i, k)),
                      pl.BlockSpec((1, 1, tk, de), lambda i, k, be, bf, na: (layer, be[i], tile(k, bf[i]), 0)),
                      pl.BlockSpec((1, 1, tk, de), lambda i, k, be, bf, na: (layer, be[i], tile(k, bf[i]), 0))],
            out_specs=pl.BlockSpec((MOE_BLOCK, de), lambda i, k, be, bf, na: (i, 0)),
            scratch_shapes=[pltpu.VMEM((2, MOE_SPLIT, tk, de), BF16), pltpu.VMEM((2, MOE_BLOCK, de), F32)]),
        out_shape=jax.ShapeDtypeStruct((cap, de), BF16),
        compiler_params=_params("arbitrary", "arbitrary"),
        name="moe_gate_up",
    )(blk_exp, blk_first, n_act, xs, wg, wu)
    return pl.pallas_call(
        _moe_down_kernel,
        grid_spec=pltpu.PrefetchScalarGridSpec(
            num_scalar_prefetch=3, grid=(n_blocks, MOE_SPLIT),
            in_specs=[pl.BlockSpec((MOE_BLOCK, de), lambda i, n, be, bf, na: (i, 0)),
                      pl.BlockSpec((1, 1, de, tn), lambda i, n, be, bf, na: (layer, be[i], 0, tile(n, bf[i])))],
            out_specs=pl.BlockSpec((MOE_BLOCK, tn), lambda i, n, be, bf, na: (i, n)),
            scratch_shapes=[pltpu.VMEM((MOE_SPLIT, de, tn), BF16)]),
        out_shape=jax.ShapeDtypeStruct((cap, dm), F32),
        compiler_params=_params("arbitrary", "arbitrary"),
        name="moe_down",
    )(blk_exp, blk_first, n_act, hid, wd)


INT32_MIN = -2 ** 31
BISECT_UNROLL = 8


def _order_key(x):
    b = lax.bitcast_convert_type(x + 0.0, jnp.int32)
    return b ^ ((b >> 31) & 0x7FFFFFFF)


def _count(mask):
    return jnp.sum(jnp.where(mask, 1.0, 0.0), axis=-1, keepdims=True)


def _topk_mask(key, pos, k, n_pos_bits):
    kf = float(k)
    thr = jnp.where(_count(key >= 0) >= kf, 0, INT32_MIN).astype(jnp.int32)

    def value_step(i, thr):
        cand = thr + jnp.left_shift(jnp.int32(1), 30 - i)
        return jnp.where(_count(key >= cand) >= kf, cand, thr)

    thr = lax.fori_loop(0, 31, value_step, thr, unroll=BISECT_UNROLL)
    above = key > thr
    n_pos = 1 << n_pos_bits
    tied_pos = jnp.where(key == thr, pos, n_pos)
    need = kf - _count(above)

    def take_lowest(_):
        def pos_step(i, last):
            cand = last + jnp.left_shift(jnp.int32(1), n_pos_bits - 1 - i)
            return jnp.where(_count(tied_pos < cand) < need, cand, last)

        return lax.fori_loop(0, n_pos_bits, pos_step, jnp.zeros_like(thr), unroll=BISECT_UNROLL)

    def take_all(_):
        return jnp.full_like(thr, n_pos - 1)

    surplus = jnp.max(_count(tied_pos < n_pos) - need)
    last = lax.cond(surplus > 0.0, take_lowest, take_all, None)
    return above | (tied_pos <= last)


def _topk_mask_by_rank(key, pos, k, n):
    rank = jnp.zeros(key.shape, F32)
    for m in range(n):
        rank = rank + jnp.where(key[:, m:m + 1] > key - jnp.where(pos > m, 1, 0), 1.0, 0.0)
    return (rank < float(k)) & (pos < n)


def _mask_bias(mask):
    return jnp.where(mask, 0.0, -jnp.inf)


def _softmax_weights(s, bias):
    s = s + bias
    m = jnp.max(s, axis=-1, keepdims=True)
    m = jnp.where(jnp.isfinite(m), m, 0.0)
    e = jnp.exp(s - m)
    return e, jnp.maximum(jnp.sum(e, axis=-1, keepdims=True), 1e-30)


def _dot_nt(a, b):
    return lax.dot_general(a, b, (((1,), (1,)), ((), ())), preferred_element_type=F32)


def _attend(q, k, v, bias, scale):
    e, denom = _softmax_weights(_dot_nt(q, k) * scale, bias)
    return jnp.dot(e.astype(BF16), v, preferred_element_type=F32) / denom


KEY_TILE = 512


def _for_causal_extent(q_end, seq, body):
    for klen in range(KEY_TILE, seq + 1, KEY_TILE):
        pl.when((q_end > klen - KEY_TILE) & (q_end <= klen))(functools.partial(body, klen))


class _Packed:
    def __init__(self, d_model):
        self.qw = d_model // 4
        self.gw = B_KV_GROUPS * HEAD_DIM
        self.iqw = A_IDX_HEADS * A_IDX_DIM
        self.cw = d_model // 2
        self.n_bheads = self.qw // HEAD_DIM
        self.gate_cols = 3 * self.n_bheads // B_KV_GROUPS
        qw, gw = self.qw, self.gw
        self.in_sizes = (qw, HEAD_DIM, HEAD_DIM, self.iqw, A_IDX_DIM, A_IDX_HEADS, qw,
                         gw, gw, gw, gw, gw, gw, 3 * self.n_bheads, self.cw)
        self.ra_tile = 512
        o = 0
        self.a_q = o; o += qw
        self.b_q = o; o += qw
        self.a_iq = o; o += self.iqw
        self.b_ks = o; o += gw
        self.b_kw = o; o += gw
        self.a_k = o; o += HEAD_DIM
        self.a_ik = o; o += A_IDX_DIM
        self.ra_width = -(-o // self.ra_tile) * self.ra_tile
        self.b_vs, self.b_vw, self.a_v = 0, gw, 2 * gw
        self.pc_width = 2 * gw + HEAD_DIM
        self.c_u = 0
        self.b_vc = self.cw
        self.a_iw = self.b_vc + gw
        self.b_g = self.a_iw + LANES
        self.pd_width = self.b_g + B_KV_GROUPS * LANES
        self.pd_tile = next(t for t in (896, 768, 640, 512, 384, 256, 128) if self.pd_width % t == 0)
        assert qw % 512 == 0 and self.iqw % 512 == 0 and self.a_iq % 512 == 0

    def weights(self, w_in):
        cuts = np.cumsum(self.in_sizes)[:-1].tolist()
        (a_q, a_k, a_v, a_iq, a_ik, a_iw, b_q, b_kc, b_vc, b_ks, b_vs, b_kw, b_vw, b_g,
         c_u) = jnp.split(w_in.astype(BF16), cuts, axis=1)
        zeros = lambda n: jnp.zeros((w_in.shape[0], n), BF16)
        ra = [a_q, b_q, a_iq, b_ks, b_kw, a_k, a_ik]
        ra.append(zeros(self.ra_width - sum(t.shape[1] for t in ra)))
        pd = [c_u, b_vc, a_iw, zeros(LANES - a_iw.shape[1])]
        for g in range(B_KV_GROUPS):
            pd += [b_g[:, g * self.gate_cols:(g + 1) * self.gate_cols], zeros(LANES - self.gate_cols)]
        cat = lambda ts: jnp.concatenate(ts, axis=1)
        return cat(ra), b_kc, cat([b_vs, b_vw, a_v]), cat(pd)


def _rope_tables(positions):
    half = ROT_DIM // 2
    inv = ROPE_THETA ** (-jnp.arange(0, ROT_DIM, 2, dtype=F32) / ROT_DIM)
    ang = positions.astype(F32).reshape(-1, 1) * inv
    cos, sin = jnp.cos(ang), jnp.sin(ang)
    n = ang.shape[0]
    ones, zeros, z16 = jnp.ones((n, HEAD_DIM - ROT_DIM), F32), jnp.zeros((n, HEAD_DIM - ROT_DIM), F32), jnp.zeros((n, half), F32)
    return (jnp.concatenate([cos, cos, ones], axis=1),
            jnp.concatenate([z16, sin, zeros], axis=1),
            jnp.concatenate([-sin, z16, zeros], axis=1))


def _dsa_kernel(q_ref, iq_ref, iw_ref, k_ref, v_ref, ik_ref, o_ref, *, n_keep, n_heads):
    seq = k_ref.shape[1]
    start = pl.program_id(1) * A_QBLOCK
    qpos = start + lax.broadcasted_iota(jnp.int32, (A_QBLOCK, 1), 0)

    def block(klen):
        kpos = lax.broadcasted_iota(jnp.int32, (1, klen), 1)
        causal = kpos <= qpos
        ik = ik_ref[0, :klen, :]
        iw = iw_ref[0] * (A_IDX_HEADS * A_IDX_DIM) ** -0.5
        score = jnp.zeros((A_QBLOCK, klen), F32)
        for h in range(A_IDX_HEADS):
            rel = _dot_nt(iq_ref[0, :, h * A_IDX_DIM:(h + 1) * A_IDX_DIM], ik)
            score = score + iw[:, h:h + 1] * jnp.maximum(rel, 0.0)
        score = jnp.where(causal, score, -jnp.inf)
        sel = _topk_mask(_order_key(score), kpos, n_keep, max(1, (klen - 1).bit_length())) & causal
        bias = _mask_bias(sel)
        k = k_ref[0, :klen, :]
        v = v_ref[0, :klen, :]
        for h in range(n_heads):
            hs = slice(h * HEAD_DIM, (h + 1) * HEAD_DIM)
            o_ref[0, :, hs] = _attend(q_ref[0, :, hs], k, v, bias, HEAD_DIM ** -0.5).astype(o_ref.dtype)

    _for_causal_extent(start + A_QBLOCK, seq, block)


def _dsa_mixer(ra, pc, pd, lay):
    bsz, seq, _ = ra.shape
    n_keep = min(A_TOPK_MAX, seq // 4)
    assert seq % KEY_TILE == 0 and KEY_TILE >= n_keep
    qb = lambda width, off: pl.BlockSpec((1, A_QBLOCK, width), lambda b, i: (b, i, off // width))
    kb = lambda width, off: pl.BlockSpec((1, seq, width), lambda b, i: (b, 0, off // width))
    return pl.pallas_call(
        functools.partial(_dsa_kernel, n_keep=n_keep, n_heads=lay.qw // HEAD_DIM),
        grid=(bsz, seq // A_QBLOCK),
        in_specs=[qb(lay.qw, lay.a_q), qb(lay.iqw, lay.a_iq), qb(LANES, lay.a_iw),
                  kb(HEAD_DIM, lay.a_k), kb(HEAD_DIM, lay.a_v), kb(A_IDX_DIM, lay.a_ik)],
        out_specs=pl.BlockSpec((1, A_QBLOCK, lay.qw), lambda b, i: (b, i, 0)),
        out_shape=jax.ShapeDtypeStruct((bsz, seq, lay.qw), BF16),
        compiler_params=_params("parallel", "arbitrary"),
        name="dsa_mixer",
    )(ra, ra, pd, ra, pc, ra)


NSA_QBLOCK = 128
GELU_C = math.sqrt(2.0 / math.pi)


def _gelu_tanh(x):
    return 0.5 * x * (1.0 + jnp.tanh(GELU_C * (x + 0.044715 * (x * x * x))))


def _nsa_compress_kernel(kc_ref, vc_ref, pe_ref, w1_ref, w2_ref, ko_ref, vo_ref):
    n_chunk = ko_ref.shape[2]
    half = B_CMP_LEN // 2
    assert B_CMP_STRIDE == half
    row = lax.broadcasted_iota(jnp.int32, (n_chunk, 1), 0)
    for t, (src, dst) in enumerate(((kc_ref, ko_ref), (vc_ref, vo_ref))):
        lo = jnp.zeros((n_chunk, w1_ref.shape[-1]), F32)
        hi = jnp.zeros((n_chunk, w1_ref.shape[-1]), F32)
        for l in range(half):
            x = src[0, pl.ds(l, n_chunk, stride=B_CMP_STRIDE), :]
            lo = lo + jnp.dot((x + pe_ref[t, l:l + 1, :]).astype(BF16),
                              w1_ref[t, l * HEAD_DIM:(l + 1) * HEAD_DIM, :], preferred_element_type=F32)
            hi = hi + jnp.dot((x + pe_ref[t, half + l:half + l + 1, :]).astype(BF16),
                              w1_ref[t, (half + l) * HEAD_DIM:(half + l + 1) * HEAD_DIM, :],
                              preferred_element_type=F32)
        pre = lo + pltpu.roll(hi, n_chunk - 1, 0)
        out = jnp.dot(_gelu_tanh(pre).astype(BF16), w2_ref[t], preferred_element_type=F32)
        dst[0, 0] = jnp.where(row < n_chunk - 1, out, 0.0).astype(dst.dtype)


def _nsa_compress(rb, pd, lay, phi_pe, phi_w1, phi_w2):
    bsz, seq, _ = rb.shape
    n_chunk = seq // B_CMP_STRIDE
    cmap = lambda b, g: (0, 0, 0)
    omap = lambda b, g: (b, g, 0, 0)
    vc_blk = lay.b_vc // HEAD_DIM
    out_sds = jax.ShapeDtypeStruct((bsz, B_KV_GROUPS, n_chunk, HEAD_DIM), BF16)
    return pl.pallas_call(
        _nsa_compress_kernel,
        grid=(bsz, B_KV_GROUPS),
        in_specs=[pl.BlockSpec((1, seq, HEAD_DIM), lambda b, g: (b, 0, g)),
                  pl.BlockSpec((1, seq, HEAD_DIM), lambda b, g: (b, 0, vc_blk + g)),
                  pl.BlockSpec(phi_pe.shape, cmap),
                  pl.BlockSpec(phi_w1.shape, cmap),
                  pl.BlockSpec(phi_w2.shape, cmap)],
        out_specs=[pl.BlockSpec((1, 1, n_chunk, HEAD_DIM), omap),
                   pl.BlockSpec((1, 1, n_chunk, HEAD_DIM), omap)],
        out_shape=[out_sds, out_sds],
        compiler_params=_params("parallel", "parallel"),
        name="nsa_compress",
    )(rb, pd, phi_pe, phi_w1.astype(BF16), phi_w2.astype(BF16))


def _nsa_kernel(q_ref, gate_ref, kcmp_ref, vcmp_ref, ks_ref, vs_ref, kw_ref, vw_ref, ovl_ref, expand_ref,
                o_ref, *, n_sel, heads_per_group):
    seq = ks_ref.shape[1]
    n_cmp = kcmp_ref.shape[2]
    n_blk = seq // B_SEL_LEN
    scale = HEAD_DIM ** -0.5
    start = pl.program_id(2) * NSA_QBLOCK
    qpos = start + lax.broadcasted_iota(jnp.int32, (NSA_QBLOCK, 1), 0)
    heads = [q_ref[0, :, r * HEAD_DIM:(r + 1) * HEAD_DIM] for r in range(heads_per_group)]

    cend = lax.broadcasted_iota(jnp.int32, (1, n_cmp), 1) * B_CMP_STRIDE + (B_CMP_LEN - 1)
    cmask = _mask_bias(cend <= qpos)
    kcmp = kcmp_ref[0, 0]
    vcmp = vcmp_ref[0, 0]
    o_cmp = []
    psum = jnp.zeros((NSA_QBLOCK, n_cmp), F32)
    for q in heads:
        e, denom = _softmax_weights(_dot_nt(q, kcmp) * scale, cmask)
        p = e / denom
        psum = psum + p
        o_cmp.append(jnp.dot(p.astype(BF16), vcmp, preferred_element_type=F32))
    p_hi = psum.astype(BF16)
    p_lo = (psum - p_hi.astype(F32)).astype(BF16)
    ovl = ovl_ref[...]
    imp = (jnp.dot(p_hi, ovl, preferred_element_type=F32) + jnp.dot(p_lo, ovl, preferred_element_type=F32))
    blk = lax.broadcasted_iota(jnp.int32, (1, imp.shape[1]), 1)
    cur = qpos // B_SEL_LEN
    imp = jnp.where((blk == 0) | (blk == cur) | (blk == cur - 1), jnp.inf,
                    jnp.where(blk > cur, -jnp.inf, imp))
    blk_sel = _topk_mask_by_rank(_order_key(imp), blk, n_sel, n_blk)

    n_win = B_WIN + NSA_QBLOCK
    kstart = pl.multiple_of(jnp.maximum(start - B_WIN, 0), NSA_QBLOCK)
    wpos = kstart + lax.broadcasted_iota(jnp.int32, (1, n_win), 1)
    win_bias = _mask_bias((wpos <= qpos) & (wpos > qpos - B_WIN))
    kw = kw_ref[0, pl.ds(kstart, n_win), :]
    vw = vw_ref[0, pl.ds(kstart, n_win), :]
    o_win = [_attend(q, kw, vw, win_bias, scale) for q in heads]

    gates = jax.nn.sigmoid(gate_ref[0])
    sel_rows = jnp.where(blk_sel, 1.0, 0.0).astype(BF16)

    def selected_and_combine(klen):
        kpos = lax.broadcasted_iota(jnp.int32, (1, klen), 1)
        in_sel = jnp.dot(sel_rows, expand_ref[:, :klen], preferred_element_type=F32)
        sel_bias = _mask_bias((in_sel > 0.5) & (kpos <= qpos))
        ks = ks_ref[0, :klen, :]
        vs = vs_ref[0, :klen, :]
        for r, q in enumerate(heads):
            gr = gates[:, 3 * r:3 * r + 3]
            out = (gr[:, 0:1] * o_cmp[r] + gr[:, 1:2] * _attend(q, ks, vs, sel_bias, scale)
                   + gr[:, 2:3] * o_win[r])
            o_ref[0, :, r * HEAD_DIM:(r + 1) * HEAD_DIM] = out.astype(o_ref.dtype)

    _for_causal_extent(start + NSA_QBLOCK, seq, selected_and_combine)


def _nsa_mixer(ra, rb, pc, pd, lay, phi_pe, phi_w1, phi_w2):
    bsz, seq, _ = ra.shape
    hpg = lay.n_bheads // B_KV_GROUPS
    gq = hpg * HEAD_DIM
    assert seq % KEY_TILE == 0 and seq >= B_WIN + NSA_QBLOCK and KEY_TILE % NSA_QBLOCK == 0
    kcmp, vcmp = _nsa_compress(rb, pd, lay, phi_pe, phi_w1, phi_w2)
    n_cmp = seq // B_CMP_STRIDE
    n_blk = seq // B_SEL_LEN
    n_sel = min(B_SEL_N, n_blk)
    assert n_blk <= LANES
    c_lo = np.arange(n_cmp)[:, None] * B_CMP_STRIDE
    b_lo = np.arange(LANES)[None, :] * B_SEL_LEN
    overlap = ((c_lo < b_lo + B_SEL_LEN) & (c_lo + B_CMP_LEN - 1 >= b_lo) & (np.arange(LANES)[None, :] < n_blk))
    expand = (np.arange(seq)[None, :] // B_SEL_LEN) == np.arange(LANES)[:, None]
    qb = lambda width, off: pl.BlockSpec((1, NSA_QBLOCK, width), lambda b, g, i: (b, i, off // width + g))
    kb = lambda off: pl.BlockSpec((1, seq, HEAD_DIM), lambda b, g, i: (b, 0, off // HEAD_DIM + g))
    cmap = lambda b, g, i: (b, g, 0, 0)
    const = lambda b, g, i: (0, 0)
    return pl.pallas_call(
        functools.partial(_nsa_kernel, n_sel=n_sel, heads_per_group=hpg),
        grid=(bsz, B_KV_GROUPS, seq // NSA_QBLOCK),
        in_specs=[qb(gq, lay.b_q), qb(LANES, lay.b_g),
                  pl.BlockSpec((1, 1, n_cmp, HEAD_DIM), cmap),
                  pl.BlockSpec((1, 1, n_cmp, HEAD_DIM), cmap),
                  kb(lay.b_ks), kb(lay.b_vs), kb(lay.b_kw), kb(lay.b_vw),
                  pl.BlockSpec((n_cmp, LANES), const),
                  pl.BlockSpec((LANES, seq), const)],
        out_specs=pl.BlockSpec((1, NSA_QBLOCK, gq), lambda b, g, i: (b, i, g)),
        out_shape=jax.ShapeDtypeStruct((bsz, seq, lay.qw), BF16),
        compiler_params=_params("parallel", "parallel", "arbitrary"),
        name="nsa_mixer",
    )(ra, pd, kcmp, vcmp, ra, pc, ra, pc, jnp.asarray(overlap, BF16), jnp.asarray(expand, BF16))


S5_TIME_BLOCK = 256
S5_UNROLL = 8


def _s5_scan_kernel(u_ref, bcat_ref, ccat_ref, a_ref, d_ref, z_ref, xs_ref, state_ref, bt_ref, tb_ref):
    bsz, tt, lanes = u_ref.shape
    ns = a_ref.shape[-1] // 2

    @pl.when(pl.program_id(1) == 0)
    def _():
        state_ref[...] = jnp.zeros_like(state_ref)

    bt_ref[...] = u_ref[...].reshape(bsz * tt, lanes)

    def to_time_major(t, carry):
        tb_ref[pl.ds(pl.multiple_of(t * bsz, bsz), bsz), :] = bt_ref[pl.ds(t, bsz, stride=tt), :]
        return carry

    lax.fori_loop(0, tt, to_time_major, 0, unroll=S5_UNROLL)
    u = tb_ref[...]
    xs_ref[...] = jnp.dot(u.astype(BF16), bcat_ref[0], preferred_element_type=F32)
    a_re = jnp.broadcast_to(a_ref[0, :, :ns], (bsz, ns))
    a_im = jnp.broadcast_to(a_ref[0, :, ns:], (bsz, ns))

    def step(t, carry):
        s_re, s_im = carry
        rows = pl.ds(pl.multiple_of(t * bsz, bsz), bsz)
        n_re = a_re * s_re - a_im * s_im + xs_ref[rows, :ns]
        n_im = a_re * s_im + a_im * s_re + xs_ref[rows, ns:]
        xs_ref[rows, :ns] = n_re
        xs_ref[rows, ns:] = n_im
        return n_re, n_im

    s_re, s_im = lax.fori_loop(0, tt, step, (state_ref[:, :ns], state_ref[:, ns:]), unroll=S5_UNROLL)
    state_ref[:, :ns] = s_re
    state_ref[:, ns:] = s_im
    y = jnp.dot(xs_ref[...].astype(BF16), ccat_ref[0], preferred_element_type=F32) + d_ref[0] * u
    tb_ref[...] = _gelu_tanh(y)
    for b in range(bsz):
        z_ref[b] = tb_ref[pl.ds(b, tt, stride=bsz), :]


def _s5_glu_kernel(za_ref, zt_ref, w_ref, b_ref, o_ref):
    zz = jnp.dot(za_ref[...].astype(BF16), w_ref[...], preferred_element_type=F32) + b_ref[...]
    o_ref[...] = (zt_ref[...] * jax.nn.sigmoid(zz)).astype(o_ref.dtype)


def _s5_mixer(pd, u_off, width, lam_re, lam_im, log_dt, b_re, b_im, c_re, c_im, d, glu_w, glu_b):
    bsz, seq, _ = pd.shape
    assert u_off % LANES == 0
    gpc = LANES // C_GROUP
    n_col = width // LANES
    ns = gpc * C_STATE
    tt = min(S5_TIME_BLOCK, seq)
    assert width % LANES == 0 and seq % tt == 0 and bsz % 8 == 0
    dt = jnp.exp(log_dt)[:, None]
    mag = jnp.exp(lam_re * dt)
    a_re, a_im = mag * jnp.cos(lam_im * dt), mag * jnp.sin(lam_im * dt)
    den = lam_re * lam_re + lam_im * lam_im
    co_re = ((a_re - 1.0) * lam_re + a_im * lam_im) / den
    co_im = (a_im * lam_re - (a_re - 1.0) * lam_im) / den
    bb_re = co_re[..., None] * b_re - co_im[..., None] * b_im
    bb_im = co_re[..., None] * b_im + co_im[..., None] * b_re
    eye = jnp.eye(gpc, dtype=F32)

    def block_diag(m):
        r, c = m.shape[-2:]
        return jnp.einsum('ngrc,gk->ngrkc', m, eye).reshape(n_col, gpc * r, gpc * c)

    to_in = lambda m: block_diag(m.reshape(n_col, gpc, C_STATE, C_GROUP).transpose(0, 1, 3, 2))
    to_out = lambda m: block_diag(m.reshape(n_col, gpc, C_GROUP, C_STATE).transpose(0, 1, 3, 2))
    bcat = jnp.concatenate([to_in(bb_re), to_in(bb_im)], axis=-1).astype(BF16)
    ccat = jnp.concatenate([to_out(c_re), to_out(-c_im)], axis=-2).astype(BF16)
    acat = jnp.concatenate([a_re.reshape(n_col, 1, ns), a_im.reshape(n_col, 1, ns)], axis=-1)
    z = pl.pallas_call(
        _s5_scan_kernel,
        grid=(n_col, seq // tt),
        in_specs=[pl.BlockSpec((bsz, tt, LANES), lambda c, t: (0, t, u_off // LANES + c)),
                  pl.BlockSpec((1, LANES, 2 * ns), lambda c, t: (c, 0, 0)),
                  pl.BlockSpec((1, 2 * ns, LANES), lambda c, t: (c, 0, 0)),
                  pl.BlockSpec((1, 1, 2 * ns), lambda c, t: (c, 0, 0)),
                  pl.BlockSpec((1, 1, LANES), lambda c, t: (c, 0, 0))],
        out_specs=pl.BlockSpec((bsz, tt, LANES), lambda c, t: (0, t, c)),
        out_shape=jax.ShapeDtypeStruct((bsz, seq, width), F32),
        scratch_shapes=[pltpu.VMEM((tt * bsz, 2 * ns), F32), pltpu.VMEM((bsz, 2 * ns), F32),
                        pltpu.VMEM((bsz * tt, LANES), F32), pltpu.VMEM((tt * bsz, LANES), F32)],
        compiler_params=_params("parallel", "arbitrary"),
        name="s5_scan",
    )(pd, bcat, ccat, acat, d.reshape(n_col, 1, LANES))
    z = z.reshape(bsz * seq, width)
    tm, tn = min(ROW_TILE, seq * bsz), min(512, width)
    return pl.pallas_call(
        _s5_glu_kernel,
        grid=(seq * bsz // tm, width // tn),
        in_specs=[pl.BlockSpec((tm, width), lambda i, j: (i, 0)),
                  pl.BlockSpec((tm, tn), lambda i, j: (i, j)),
                  pl.BlockSpec((width, tn), lambda i, j: (0, j)),
                  pl.BlockSpec((1, tn), lambda i, j: (0, j))],
        out_specs=pl.BlockSpec((tm, tn), lambda i, j: (i, j)),
        out_shape=jax.ShapeDtypeStruct((seq * bsz, width), BF16),
        compiler_params=_params("parallel", "parallel"),
        name="s5_glu",
    )(z, z, glu_w.astype(BF16), glu_b.reshape(1, width))


def _mixing_sublayer(hb, bsz, seq, rope, lay, w_in, phi_pe, phi_w1, phi_w2, lam_re, lam_im, log_dt,
                     b_re, b_im, c_re, c_im, s5_d, glu_w, glu_b, w_out):
    w_ra, w_rb, w_pc, w_pd = lay.weights(w_in)
    view = lambda t: t.reshape(bsz, seq, t.shape[-1])
    ra = view(_matmul(hb, w_ra, lay.ra_tile, BF16, rope))
    rb = view(_matmul(hb, w_rb, w_rb.shape[1], F32, rope))
    pc = view(_matmul(hb, w_pc, w_pc.shape[1], BF16))
    pd = view(_matmul(hb, w_pd, lay.pd_tile, F32))
    o_a = _dsa_mixer(ra, pc, pd, lay)
    o_b = _nsa_mixer(ra, rb, pc, pd, lay, phi_pe, phi_w1, phi_w2)
    o_c = _s5_mixer(pd, lay.c_u, lay.cw, lam_re, lam_im, log_dt, b_re, b_im, c_re, c_im, s5_d, glu_w, glu_b)
    return _out_proj(o_a.reshape(bsz * seq, -1), o_b.reshape(bsz * seq, -1), o_c, w_out.astype(BF16))


ROUTER_ROWS = 256


def _split_bf16(x):
    hi = x.astype(BF16)
    return hi, (x - hi.astype(F32)).astype(BF16)


def _router_kernel(x_ref, wh_ref, wl_ref, bias_ref, eid_ref, gate_ref):
    x_hi, x_lo = _split_bf16(x_ref[...])
    wh = wh_ref[...]
    logits = (jnp.dot(x_hi, wh, preferred_element_type=F32) + jnp.dot(x_hi, wl_ref[...], preferred_element_type=F32)
              + jnp.dot(x_lo, wh, preferred_element_type=F32))
    aff = jax.nn.sigmoid(logits)
    lane_i = lax.broadcasted_iota(jnp.int32, (1, LANES), 1)
    lane = lane_i.astype(F32)
    grp = (lane_i // EXPERTS_PER_GROUP).astype(F32)
    sel = jnp.where(lane_i < N_EXPERTS, aff + bias_ref[...], -jnp.inf)

    def top2(v):
        m1 = jnp.max(v, axis=-1, keepdims=True)
        i1 = jnp.min(jnp.where(v == m1, lane, float(LANES)), axis=-1, keepdims=True)
        v2 = jnp.where(lane == i1, -jnp.inf, v)
        m2 = jnp.max(v2, axis=-1, keepdims=True)
        i2 = jnp.min(jnp.where(v2 == m2, lane, float(LANES)), axis=-1, keepdims=True)
        return m1, i1, m2, i2

    best_score = best_grp = None
    for g in range(N_EXPERT_GROUPS):
        m1, _, m2, _ = top2(jnp.where(grp == float(g), sel, -jnp.inf))
        score = m1 + m2
        if g == 0:
            best_score, best_grp = score, jnp.zeros_like(score)
        else:
            better = score > best_score
            best_score = jnp.where(better, score, best_score)
            best_grp = jnp.where(better, float(g), best_grp)
    _, i1, _, i2 = top2(jnp.where(grp == best_grp, sel, -jnp.inf))
    w1 = jnp.sum(jnp.where(lane == i1, aff, 0.0), axis=-1, keepdims=True)
    w2 = jnp.sum(jnp.where(lane == i2, aff, 0.0), axis=-1, keepdims=True)
    tot = w1 + w2
    eid_ref[...] = jnp.where(lane_i == 0, i1, jnp.where(lane_i == 1, i2, 0.0)).astype(jnp.int32)
    gate_ref[...] = jnp.where(lane_i == 0, w1 / tot, jnp.where(lane_i == 1, w2 / tot, 0.0))


def _route(xt, router_w, router_bias):
    n_tok, dm = xt.shape
    assert n_tok % ROUTER_ROWS == 0 and N_EXPERTS <= LANES
    pad = LANES - N_EXPERTS
    wh, wl = _split_bf16(jnp.pad(router_w, ((0, 0), (0, pad))))
    bias = jnp.pad(router_bias, (0, pad)).reshape(1, LANES)
    rows = lambda w: pl.BlockSpec((ROUTER_ROWS, w), lambda i: (i, 0))
    const = lambda shape: pl.BlockSpec(shape, lambda i: (0, 0))
    eid, gate = pl.pallas_call(
        _router_kernel, grid=(n_tok // ROUTER_ROWS,),
        in_specs=[rows(dm), const((dm, LANES)), const((dm, LANES)), const((1, LANES))],
        out_specs=[rows(LANES), rows(LANES)],
        out_shape=[jax.ShapeDtypeStruct((n_tok, LANES), jnp.int32), jax.ShapeDtypeStruct((n_tok, LANES), F32)],
        compiler_params=_params("parallel"), name="moe_router",
    )(xt, wh, wl, bias)
    return eid[:, :TOP_K], gate[:, :TOP_K]


def _moe(xt, router_w, router_bias, layer, w_gate, w_up, w_down):
    n_tok, dm = xt.shape
    eid, gate = _route(xt, router_w, router_bias)
    n_asg = n_tok * TOP_K
    cap = -(-n_asg // MOE_BLOCK) * MOE_BLOCK + N_EXPERTS * MOE_BLOCK
    onehot = (eid.reshape(n_asg, 1) == jnp.arange(N_EXPERTS, dtype=jnp.int32)[None, :]).astype(jnp.int32)
    seen = jnp.cumsum(onehot, axis=0)
    counts = seen[-1]
    padded = (counts + MOE_BLOCK - 1) // MOE_BLOCK * MOE_BLOCK
    pend = jnp.cumsum(padded)
    pstart = pend - padded
    dest = jnp.sum(onehot * (seen - 1 + pstart[None, :]), axis=1).astype(jnp.int32)
    tok_flat = jnp.arange(n_asg, dtype=jnp.int32) // TOP_K
    slot_tok = (jnp.arange(cap, dtype=jnp.int32) % n_tok).at[dest].set(tok_flat)
    n_blocks = cap // MOE_BLOCK
    blk_start = jnp.arange(n_blocks) * MOE_BLOCK
    blk_exp = jnp.minimum(jnp.sum(pend[None, :] <= blk_start[:, None], axis=1), N_EXPERTS - 1).astype(jnp.int32)
    blk_first = jnp.concatenate([jnp.ones((1,), jnp.int32), (blk_exp[1:] != blk_exp[:-1]).astype(jnp.int32)])
    n_act = (pend[-1] // MOE_BLOCK).astype(jnp.int32).reshape(1)
    y = _moe_experts(xt[slot_tok], blk_exp, blk_first, n_act, layer, w_gate, w_up, w_down)
    pos = dest.reshape(n_tok, TOP_K)
    return y[pos[:, 0]], y[pos[:, 1]], gate


def kernel(x, positions, ln_in_g, ln_in_b, w_in, nsa_phi_pe, nsa_phi_w1, nsa_phi_w2, s5_lam_re, s5_lam_im, s5_log_dt, s5_b_re, s5_b_im, s5_c_re, s5_c_im, s5_d, s5_glu_w, s5_glu_b, w_out, ln_mix_g, ln_mix_b, router_w, router_bias, moe_w_gate, moe_w_up, moe_w_down, ln_ffn_g, ln_ffn_b):
    depth = w_in.shape[0]
    bsz, seq, d_model = x.shape
    alpha = (2 * depth) ** 0.25
    lay = _Packed(d_model)
    rope = _rope_tables(positions)
    h, hb = _layer_norm(x.reshape(bsz * seq, d_model), ln_in_g, ln_in_b)
    for l in range(depth):
        mix = _mixing_sublayer(hb, bsz, seq, rope, lay, w_in[l], nsa_phi_pe[l], nsa_phi_w1[l], nsa_phi_w2[l],
                               s5_lam_re[l], s5_lam_im[l], s5_log_dt[l], s5_b_re[l], s5_b_im[l],
                               s5_c_re[l], s5_c_im[l], s5_d[l], s5_glu_w[l], s5_glu_b[l], w_out[l])
        h, hb = _layer_norm(h, ln_mix_g[l], ln_mix_b[l], residual=mix, alpha=alpha)
        routed = _moe(h, router_w, router_bias, l, moe_w_gate, moe_w_up, moe_w_down)
        h, hb = _layer_norm(h, ln_ffn_g[l], ln_ffn_b[l], gated_pair=routed, alpha=alpha)
    return h.reshape(bsz, seq, d_model)
```

```python
import functools
import math

import numpy as np
import jax
import jax.numpy as jnp
from jax import lax
from jax.experimental import pallas as pl
from jax.experimental.pallas import tpu as pltpu

F32 = jnp.float32
BF16 = jnp.bfloat16

LANES = 128
HEAD_DIM = 128
ROT_DIM = HEAD_DIM // 4
ROPE_THETA = 500000.0
A_IDX_HEADS = 4
A_IDX_DIM = 128
A_TOPK_MAX = 256
A_QBLOCK = 128
B_KV_GROUPS = 2
B_CMP_LEN = 32
B_CMP_STRIDE = 16
B_SEL_LEN = 64
B_SEL_N = 16
B_WIN = 512
C_GROUP = 16
C_STATE = 64
N_EXPERTS = 32
N_EXPERT_GROUPS = 4
EXPERTS_PER_GROUP = N_EXPERTS // N_EXPERT_GROUPS
TOP_K = 2
MOE_BLOCK = 256
LN_EPS = 1e-5

VMEM_LIMIT_BYTES = 56 * 1024 * 1024
ROW_TILE = 1024


def _params(*semantics):
    return pltpu.CompilerParams(dimension_semantics=semantics, vmem_limit_bytes=VMEM_LIMIT_BYTES)


def _mm_kernel(a_ref, b_ref, o_ref):
    o_ref[...] = jnp.dot(a_ref[...], b_ref[...], preferred_element_type=F32).astype(o_ref.dtype)


def _mm_rope_kernel(a_ref, b_ref, c_ref, sa_ref, sb_ref, o_ref):
    y = jnp.dot(a_ref[...], b_ref[...], preferred_element_type=F32)
    half = ROT_DIM // 2
    for hb in range(y.shape[1] // HEAD_DIM):
        cols = slice(hb * HEAD_DIM, (hb + 1) * HEAD_DIM)
        yh = y[:, cols]
        out = (yh * c_ref[...] + pltpu.roll(yh, half, 1) * sa_ref[...]
               + pltpu.roll(yh, HEAD_DIM - half, 1) * sb_ref[...])
        o_ref[:, cols] = out.astype(o_ref.dtype)


def _matmul(a, b, tn, out_dtype, rope=None):
    m, k = a.shape
    n = b.shape[1]
    tm = min(ROW_TILE, m)
    assert m % tm == 0 and n % tn == 0
    in_specs = [pl.BlockSpec((tm, k), lambda i, j: (i, 0)),
                pl.BlockSpec((k, tn), lambda i, j: (0, j))]
    args = (a, b)
    if rope is not None:
        in_specs += [pl.BlockSpec((tm, HEAD_DIM), lambda i, j: (i, 0))] * 3
        args += tuple(rope)
    return pl.pallas_call(
        _mm_kernel if rope is None else _mm_rope_kernel,
        grid=(m // tm, n // tn),
        in_specs=in_specs,
        out_specs=pl.BlockSpec((tm, tn), lambda i, j: (i, j)),
        out_shape=jax.ShapeDtypeStruct((m, n), out_dtype),
        compiler_params=_params("parallel", "parallel"),
        name="in_proj" if rope is None else "in_proj_rope",
    )(*args)


def _out_proj_kernel(a0_ref, a1_ref, a2_ref, w_ref, o_ref):
    k0 = a0_ref.shape[1]
    k1 = k0 + a1_ref.shape[1]
    acc = jnp.dot(a0_ref[...], w_ref[:k0, :], preferred_element_type=F32)
    acc = acc + jnp.dot(a1_ref[...], w_ref[k0:k1, :], preferred_element_type=F32)
    o_ref[...] = acc + jnp.dot(a2_ref[...], w_ref[k1:, :], preferred_element_type=F32)


def _out_proj(a0, a1, a2, w, tn=512):
    m = a0.shape[0]
    k, n = w.shape
    tm = min(ROW_TILE, m)
    assert a0.shape[1] + a1.shape[1] + a2.shape[1] == k and m % tm == 0 and n % tn == 0
    amap = lambda i, j: (i, 0)
    return pl.pallas_call(
        _out_proj_kernel,
        grid=(m // tm, n // tn),
        in_specs=[pl.BlockSpec((tm, a0.shape[1]), amap), pl.BlockSpec((tm, a1.shape[1]), amap),
                  pl.BlockSpec((tm, a2.shape[1]), amap), pl.BlockSpec((k, tn), lambda i, j: (0, j))],
        out_specs=pl.BlockSpec((tm, tn), lambda i, j: (i, j)),
        out_shape=jax.ShapeDtypeStruct((m, n), F32),
        compiler_params=_params("parallel", "parallel"),
        name="out_proj",
    )(a0, a1, a2, w)


LN_ROWS = 256


def _ln_store(v, g_ref, b_ref, o_ref, ob_ref):
    mu = jnp.mean(v, axis=-1, keepdims=True)
    c = v - mu
    var = jnp.mean(c * c, axis=-1, keepdims=True)
    out = c * lax.rsqrt(var + LN_EPS) * g_ref[...] + b_ref[...]
    o_ref[...] = out
    ob_ref[...] = out.astype(BF16)


def _ln_kernel(x_ref, g_ref, b_ref, o_ref, ob_ref):
    _ln_store(x_ref[...], g_ref, b_ref, o_ref, ob_ref)


def _ln_residual_kernel(x_ref, r_ref, g_ref, b_ref, o_ref, ob_ref, *, alpha):
    _ln_store(alpha * x_ref[...] + r_ref[...], g_ref, b_ref, o_ref, ob_ref)


def _ln_gated_pair_kernel(x_ref, y0_ref, y1_ref, w_ref, g_ref, b_ref, o_ref, ob_ref, *, alpha):
    w = w_ref[...]
    ffn = w[:, 0:1] * y0_ref[...] + w[:, 1:2] * y1_ref[...]
    _ln_store(alpha * x_ref[...] + ffn, g_ref, b_ref, o_ref, ob_ref)


def _layer_norm(x, g, b, residual=None, alpha=1.0, gated_pair=None):
    m, d = x.shape
    rows = pl.BlockSpec((LN_ROWS, d), lambda i: (i, 0))
    vec = pl.BlockSpec((1, d), lambda i: (0, 0))
    assert m % LN_ROWS == 0
    if gated_pair is not None:
        y0, y1, w = gated_pair
        body = functools.partial(_ln_gated_pair_kernel, alpha=alpha)
        ins = [rows, rows, rows, pl.BlockSpec((LN_ROWS, w.shape[1]), lambda i: (i, 0)), vec, vec]
        args = (x, y0, y1, w, g.reshape(1, d), b.reshape(1, d))
    elif residual is None:
        body, ins, args = _ln_kernel, [rows, vec, vec], (x, g.reshape(1, d), b.reshape(1, d))
    else:
        body = functools.partial(_ln_residual_kernel, alpha=alpha)
        ins, args = [rows, rows, vec, vec], (x, residual, g.reshape(1, d), b.reshape(1, d))
    return pl.pallas_call(
        body, grid=(m // LN_ROWS,), in_specs=ins, out_specs=[rows, rows],
        out_shape=[jax.ShapeDtypeStruct((m, d), F32), jax.ShapeDtypeStruct((m, d), BF16)],
        compiler_params=_params("parallel"), name="layer_norm",
    )(*args)


MOE_SPLIT = 2


def _moe_up_kernel(blk_exp_ref, first_ref, n_act_ref, x_ref, wg_ref, wu_ref, hid_ref, wres_ref, acc_ref):
    i, k = pl.program_id(0), pl.program_id(1)
    active = i < n_act_ref[0]

    de = hid_ref.shape[1]

    @pl.when(active & (first_ref[i] == 1))
    def _():
        wres_ref[k, :, :de] = wg_ref[0, 0].astype(BF16)
        wres_ref[k, :, de:] = wu_ref[0, 0].astype(BF16)

    @pl.when(active)
    def _():
        gu = jnp.dot(x_ref[...].astype(BF16), wres_ref[k], preferred_element_type=F32)

        @pl.when(k == 0)
        def _():
            acc_ref[...] = gu

        @pl.when(k > 0)
        def _():
            acc_ref[...] += gu

        @pl.when(k == MOE_SPLIT - 1)
        def _():
            g_all = acc_ref[:, :de]
            hid_ref[...] = (g_all * jax.nn.sigmoid(g_all) * acc_ref[:, de:]).astype(hid_ref.dtype)

    @pl.when(jnp.logical_not(active) & (k == MOE_SPLIT - 1))
    def _():
        hid_ref[...] = jnp.zeros_like(hid_ref)


def _moe_down_kernel(blk_exp_ref, first_ref, n_act_ref, hid_ref, wd_ref, y_ref, wres_ref):
    i, n = pl.program_id(0), pl.program_id(1)
    active = i < n_act_ref[0]

    @pl.when(active & (first_ref[i] == 1))
    def _():
        wres_ref[n] = wd_ref[0, 0].astype(BF16)

    @pl.when(active)
    def _():
        y_ref[...] = jnp.dot(hid_ref[...], wres_ref[n], preferred_element_type=F32)

    @pl.when(jnp.logical_not(active))
    def _():
        y_ref[...] = jnp.zeros_like(y_ref)


def _moe_experts(xs, blk_exp, blk_first, n_act, layer, wg, wu, wd):
    cap, dm = xs.shape
    de = wg.shape[-1]
    n_blocks = cap // MOE_BLOCK
    tk, tn = dm // MOE_SPLIT, dm // MOE_SPLIT
    tile = lambda j, first: jnp.where(first == 1, j, MOE_SPLIT - 1)
    hid = pl.pallas_call(
        _moe_up_kernel,
        grid_spec=pltpu.PrefetchScalarGridSpec(
            num_scalar_prefetch=3, grid=(n_blocks, MOE_SPLIT),
            in_specs=[pl.BlockSpec((MOE_BLOCK, tk), lambda i, k, be, bf, na: (i, k)),
                      pl.BlockSpec((1, 1, tk, de), lambda i, k, be, bf, na: (layer, be[i], tile(k, bf[i]), 0)),
                      pl.BlockSpec((1, 1, tk, de), lambda i, k, be, bf, na: (layer, be[i], tile(k, bf[i]), 0))],
            out_specs=pl.BlockSpec((MOE_BLOCK, de), lambda i, k, be, bf, na: (i, 0)),
            scratch_shapes=[pltpu.VMEM((MOE_SPLIT, tk, 2 * de), BF16), pltpu.VMEM((MOE_BLOCK, 2 * de), F32)]),
        out_shape=jax.ShapeDtypeStruct((cap, de), BF16),
        compiler_params=_params("arbitrary", "arbitrary"),
        name="moe_gate_up",
    )(blk_exp, blk_first, n_act, xs, wg, wu)
    return pl.pallas_call(
        _moe_down_kernel,
        grid_spec=pltpu.PrefetchScalarGridSpec(
            num_scalar_prefetch=3, grid=(n_blocks, MOE_SPLIT),
            in_specs=[pl.BlockSpec((MOE_BLOCK, de), lambda i, n, be, bf, na: (i, 0)),
                      pl.BlockSpec((1, 1, de, tn), lambda i, n, be, bf, na: (layer, be[i], 0, tile(n, bf[i])))],
            out_specs=pl.BlockSpec((MOE_BLOCK, tn), lambda i, n, be, bf, na: (i, n)),
            scratch_shapes=[pltpu.VMEM((MOE_SPLIT, de, tn), BF16)]),
        out_shape=jax.ShapeDtypeStruct((cap, dm), F32),
        compiler_params=_params("arbitrary", "arbitrary"),
        name="moe_down",
    )(blk_exp, blk_first, n_act, hid, wd)


INT32_MIN = -2 ** 31
BISECT_UNROLL = 8


def _order_key(x):
    b = lax.bitcast_convert_type(x + 0.0, jnp.int32)
    return b ^ ((b >> 31) & 0x7FFFFFFF)


def _count(mask):
    return jnp.sum(jnp.where(mask, 1.0, 0.0), axis=-1, keepdims=True)


def _topk_mask(key, pos, k, n_pos_bits):
    kf = float(k)
    thr = jnp.where(_count(key >= 0) >= kf, 0, INT32_MIN).astype(jnp.int32)

    def value_step(i, thr):
        cand = thr + jnp.left_shift(jnp.int32(1), 30 - i)
        return jnp.where(_count(key >= cand) >= kf, cand, thr)

    thr = lax.fori_loop(0, 31, value_step, thr, unroll=BISECT_UNROLL)
    above = key > thr
    n_pos = 1 << n_pos_bits
    tied_pos = jnp.where(key == thr, pos, n_pos)
    need = kf - _count(above)

    def take_lowest(_):
        def pos_step(i, last):
            cand = last + jnp.left_shift(jnp.int32(1), n_pos_bits - 1 - i)
            return jnp.where(_count(tied_pos < cand) < need, cand, last)

        return lax.fori_loop(0, n_pos_bits, pos_step, jnp.zeros_like(thr), unroll=BISECT_UNROLL)

    def take_all(_):
        return jnp.full_like(thr, n_pos - 1)

    surplus = jnp.max(_count(tied_pos < n_pos) - need)
    last = lax.cond(surplus > 0.0, take_lowest, take_all, None)
    return above | (tied_pos <= last)


def _topk_mask_by_rank(key, pos, k, n):
    rank = jnp.zeros(key.shape, F32)
    for m in range(n):
        rank = rank + jnp.where(key[:, m:m + 1] > key - jnp.where(pos > m, 1, 0), 1.0, 0.0)
    return (rank < float(k)) & (pos < n)


def _mask_bias(mask):
    return jnp.where(mask, 0.0, -jnp.inf)


def _softmax_weights(s, bias):
    s = s + bias
    m = jnp.max(s, axis=-1, keepdims=True)
    m = jnp.where(jnp.isfinite(m), m, 0.0)
    e = jnp.exp(s - m)
    return e, jnp.maximum(jnp.sum(e, axis=-1, keepdims=True), 1e-30)


def _dot_nt(a, b):
    return lax.dot_general(a, b, (((1,), (1,)), ((), ())), preferred_element_type=F32)


def _attend(q, k, v, bias, scale):
    e, denom = _softmax_weights(_dot_nt(q, k) * scale, bias)
    return jnp.dot(e.astype(BF16), v, preferred_element_type=F32) / denom


KEY_TILE = 512


def _for_causal_extent(q_end, seq, body):
    for klen in range(KEY_TILE, seq + 1, KEY_TILE):
        pl.when((q_end > klen - KEY_TILE) & (q_end <= klen))(functools.partial(body, klen))


class _Packed:
    def __init__(self, d_model):
        self.qw = d_model // 4
        self.gw = B_KV_GROUPS * HEAD_DIM
        self.iqw = A_IDX_HEADS * A_IDX_DIM
        self.cw = d_model // 2
        self.n_bheads = self.qw // HEAD_DIM
        self.gate_cols = 3 * self.n_bheads // B_KV_GROUPS
        qw, gw = self.qw, self.gw
        self.in_sizes = (qw, HEAD_DIM, HEAD_DIM, self.iqw, A_IDX_DIM, A_IDX_HEADS, qw,
                         gw, gw, gw, gw, gw, gw, 3 * self.n_bheads, self.cw)
        self.ra_tile = 512
        o = 0
        self.a_q = o; o += qw
        self.b_q = o; o += qw
        self.a_iq = o; o += self.iqw
        self.b_ks = o; o += gw
        self.b_kw = o; o += gw
        self.a_k = o; o += HEAD_DIM
        self.a_ik = o; o += A_IDX_DIM
        self.ra_width = -(-o // self.ra_tile) * self.ra_tile
        self.b_vs, self.b_vw, self.a_v = 0, gw, 2 * gw
        self.pc_width = 2 * gw + HEAD_DIM
        self.c_u = 0
        self.b_vc = self.cw
        self.a_iw = self.b_vc + gw
        self.b_g = self.a_iw + LANES
        self.pd_width = self.b_g + B_KV_GROUPS * LANES
        self.pd_tile = next(t for t in (896, 768, 640, 512, 384, 256, 128) if self.pd_width % t == 0)
        assert qw % 512 == 0 and self.iqw % 512 == 0 and self.a_iq % 512 == 0

    def weights(self, w_in):
        cuts = np.cumsum(self.in_sizes)[:-1].tolist()
        (a_q, a_k, a_v, a_iq, a_ik, a_iw, b_q, b_kc, b_vc, b_ks, b_vs, b_kw, b_vw, b_g,
         c_u) = jnp.split(w_in.astype(BF16), cuts, axis=1)
        zeros = lambda n: jnp.zeros((w_in.shape[0], n), BF16)
        ra = [a_q, b_q, a_iq, b_ks, b_kw, a_k, a_ik]
        ra.append(zeros(self.ra_width - sum(t.shape[1] for t in ra)))
        pd = [c_u, b_vc, a_iw, zeros(LANES - a_iw.shape[1])]
        for g in range(B_KV_GROUPS):
            pd += [b_g[:, g * self.gate_cols:(g + 1) * self.gate_cols], zeros(LANES - self.gate_cols)]
        cat = lambda ts: jnp.concatenate(ts, axis=1)
        return cat(ra), b_kc, cat([b_vs, b_vw, a_v]), cat(pd)


def _rope_tables(positions):
    half = ROT_DIM // 2
    inv = ROPE_THETA ** (-jnp.arange(0, ROT_DIM, 2, dtype=F32) / ROT_DIM)
    ang = positions.astype(F32).reshape(-1, 1) * inv
    cos, sin = jnp.cos(ang), jnp.sin(ang)
    n = ang.shape[0]
    ones, zeros, z16 = jnp.ones((n, HEAD_DIM - ROT_DIM), F32), jnp.zeros((n, HEAD_DIM - ROT_DIM), F32), jnp.zeros((n, half), F32)
    return (jnp.concatenate([cos, cos, ones], axis=1),
            jnp.concatenate([z16, sin, zeros], axis=1),
            jnp.concatenate([-sin, z16, zeros], axis=1))


def _dsa_kernel(q_ref, iq_ref, iw_ref, k_ref, v_ref, ik_ref, o_ref, *, n_keep, n_heads):
    seq = k_ref.shape[1]
    start = pl.program_id(1) * A_QBLOCK
    qpos = start + lax.broadcasted_iota(jnp.int32, (A_QBLOCK, 1), 0)

    def block(klen):
        kpos = lax.broadcasted_iota(jnp.int32, (1, klen), 1)
        causal = kpos <= qpos
        ik = ik_ref[0, :klen, :]
        iw = iw_ref[0] * (A_IDX_HEADS * A_IDX_DIM) ** -0.5
        score = jnp.zeros((A_QBLOCK, klen), F32)
        for h in range(A_IDX_HEADS):
            rel = _dot_nt(iq_ref[0, :, h * A_IDX_DIM:(h + 1) * A_IDX_DIM], ik)
            score = score + iw[:, h:h + 1] * jnp.maximum(rel, 0.0)
        score = jnp.where(causal, score, -jnp.inf)
        sel = _topk_mask(_order_key(score), kpos, n_keep, max(1, (klen - 1).bit_length())) & causal
        bias = _mask_bias(sel)
        k = k_ref[0, :klen, :]
        v = v_ref[0, :klen, :]
        for h in range(n_heads):
            hs = slice(h * HEAD_DIM, (h + 1) * HEAD_DIM)
            o_ref[0, :, hs] = _attend(q_ref[0, :, hs], k, v, bias, HEAD_DIM ** -0.5).astype(o_ref.dtype)

    _for_causal_extent(start + A_QBLOCK, seq, block)


def _dsa_mixer(ra, pc, pd, lay):
    bsz, seq, _ = ra.shape
    n_keep = min(A_TOPK_MAX, seq // 4)
    assert seq % KEY_TILE == 0 and KEY_TILE >= n_keep
    qb = lambda width, off: pl.BlockSpec((1, A_QBLOCK, width), lambda b, i: (b, i, off // width))
    kb = lambda width, off: pl.BlockSpec((1, seq, width), lambda b, i: (b, 0, off // width))
    return pl.pallas_call(
        functools.partial(_dsa_kernel, n_keep=n_keep, n_heads=lay.qw // HEAD_DIM),
        grid=(bsz, seq // A_QBLOCK),
        in_specs=[qb(lay.qw, lay.a_q), qb(lay.iqw, lay.a_iq), qb(LANES, lay.a_iw),
                  kb(HEAD_DIM, lay.a_k), kb(HEAD_DIM, lay.a_v), kb(A_IDX_DIM, lay.a_ik)],
        out_specs=pl.BlockSpec((1, A_QBLOCK, lay.qw), lambda b, i: (b, i, 0)),
        out_shape=jax.ShapeDtypeStruct((bsz, seq, lay.qw), BF16),
        compiler_params=_params("parallel", "arbitrary"),
        name="dsa_mixer",
    )(ra, ra, pd, ra, pc, ra)


NSA_QBLOCK = 256
GELU_C = math.sqrt(2.0 / math.pi)


def _gelu_tanh(x):
    return 0.5 * x * (1.0 + jnp.tanh(GELU_C * (x + 0.044715 * (x * x * x))))


def _nsa_compress_kernel(kc_ref, vc_ref, pe_ref, w1_ref, w2_ref, ko_ref, vo_ref):
    n_chunk = ko_ref.shape[2]
    half = B_CMP_LEN // 2
    assert B_CMP_STRIDE == half
    row = lax.broadcasted_iota(jnp.int32, (n_chunk, 1), 0)
    for t, (src, dst) in enumerate(((kc_ref, ko_ref), (vc_ref, vo_ref))):
        lo = jnp.zeros((n_chunk, w1_ref.shape[-1]), F32)
        hi = jnp.zeros((n_chunk, w1_ref.shape[-1]), F32)
        for l in range(half):
            x = src[0, pl.ds(l, n_chunk, stride=B_CMP_STRIDE), :]
            lo = lo + jnp.dot((x + pe_ref[t, l:l + 1, :]).astype(BF16),
                              w1_ref[t, l * HEAD_DIM:(l + 1) * HEAD_DIM, :], preferred_element_type=F32)
            hi = hi + jnp.dot((x + pe_ref[t, half + l:half + l + 1, :]).astype(BF16),
                              w1_ref[t, (half + l) * HEAD_DIM:(half + l + 1) * HEAD_DIM, :],
                              preferred_element_type=F32)
        pre = lo + pltpu.roll(hi, n_chunk - 1, 0)
        out = jnp.dot(_gelu_tanh(pre).astype(BF16), w2_ref[t], preferred_element_type=F32)
        dst[0, 0] = jnp.where(row < n_chunk - 1, out, 0.0).astype(dst.dtype)


def _nsa_compress(rb, pd, lay, phi_pe, phi_w1, phi_w2):
    bsz, seq, _ = rb.shape
    n_chunk = seq // B_CMP_STRIDE
    cmap = lambda b, g: (0, 0, 0)
    omap = lambda b, g: (b, g, 0, 0)
    vc_blk = lay.b_vc // HEAD_DIM
    out_sds = jax.ShapeDtypeStruct((bsz, B_KV_GROUPS, n_chunk, HEAD_DIM), BF16)
    return pl.pallas_call(
        _nsa_compress_kernel,
        grid=(bsz, B_KV_GROUPS),
        in_specs=[pl.BlockSpec((1, seq, HEAD_DIM), lambda b, g: (b, 0, g)),
                  pl.BlockSpec((1, seq, HEAD_DIM), lambda b, g: (b, 0, vc_blk + g)),
                  pl.BlockSpec(phi_pe.shape, cmap),
                  pl.BlockSpec(phi_w1.shape, cmap),
                  pl.BlockSpec(phi_w2.shape, cmap)],
        out_specs=[pl.BlockSpec((1, 1, n_chunk, HEAD_DIM), omap),
                   pl.BlockSpec((1, 1, n_chunk, HEAD_DIM), omap)],
        out_shape=[out_sds, out_sds],
        compiler_params=_params("parallel", "parallel"),
        name="nsa_compress",
    )(rb, pd, phi_pe, phi_w1.astype(BF16), phi_w2.astype(BF16))


def _nsa_kernel(q_ref, gate_ref, kcmp_ref, vcmp_ref, ks_ref, vs_ref, kw_ref, vw_ref, ovl_ref, expand_ref,
                o_ref, *, n_sel, heads_per_group):
    seq = ks_ref.shape[1]
    n_cmp = kcmp_ref.shape[2]
    n_blk = seq // B_SEL_LEN
    scale = HEAD_DIM ** -0.5
    start = pl.program_id(2) * NSA_QBLOCK
    qpos = start + lax.broadcasted_iota(jnp.int32, (NSA_QBLOCK, 1), 0)
    heads = [q_ref[0, :, r * HEAD_DIM:(r + 1) * HEAD_DIM] for r in range(heads_per_group)]

    cend = lax.broadcasted_iota(jnp.int32, (1, n_cmp), 1) * B_CMP_STRIDE + (B_CMP_LEN - 1)
    cmask = _mask_bias(cend <= qpos)
    kcmp = kcmp_ref[0, 0]
    vcmp = vcmp_ref[0, 0]
    o_cmp = []
    psum = jnp.zeros((NSA_QBLOCK, n_cmp), F32)
    for q in heads:
        e, denom = _softmax_weights(_dot_nt(q, kcmp) * scale, cmask)
        p = e / denom
        psum = psum + p
        o_cmp.append(jnp.dot(p.astype(BF16), vcmp, preferred_element_type=F32))
    p_hi = psum.astype(BF16)
    p_lo = (psum - p_hi.astype(F32)).astype(BF16)
    ovl = ovl_ref[...]
    imp = (jnp.dot(p_hi, ovl, preferred_element_type=F32) + jnp.dot(p_lo, ovl, preferred_element_type=F32))
    blk = lax.broadcasted_iota(jnp.int32, (1, imp.shape[1]), 1)
    cur = qpos // B_SEL_LEN
    imp = jnp.where((blk == 0) | (blk == cur) | (blk == cur - 1), jnp.inf,
                    jnp.where(blk > cur, -jnp.inf, imp))
    blk_sel = _topk_mask_by_rank(_order_key(imp), blk, n_sel, n_blk)

    n_win = B_WIN + NSA_QBLOCK
    kstart = pl.multiple_of(jnp.maximum(start - B_WIN, 0), NSA_QBLOCK)
    wpos = kstart + lax.broadcasted_iota(jnp.int32, (1, n_win), 1)
    win_bias = _mask_bias((wpos <= qpos) & (wpos > qpos - B_WIN))
    kw = kw_ref[0, pl.ds(kstart, n_win), :]
    vw = vw_ref[0, pl.ds(kstart, n_win), :]
    o_win = [_attend(q, kw, vw, win_bias, scale) for q in heads]

    gates = jax.nn.sigmoid(gate_ref[0])
    sel_rows = jnp.where(blk_sel, 1.0, 0.0).astype(BF16)

    def selected_and_combine(klen):
        kpos = lax.broadcasted_iota(jnp.int32, (1, klen), 1)
        in_sel = jnp.dot(sel_rows, expand_ref[:, :klen], preferred_element_type=F32)
        sel_bias = _mask_bias((in_sel > 0.5) & (kpos <= qpos))
        ks = ks_ref[0, :klen, :]
        vs = vs_ref[0, :klen, :]
        for r, q in enumerate(heads):
            gr = gates[:, 3 * r:3 * r + 3]
            out = (gr[:, 0:1] * o_cmp[r] + gr[:, 1:2] * _attend(q, ks, vs, sel_bias, scale)
                   + gr[:, 2:3] * o_win[r])
            o_ref[0, :, r * HEAD_DIM:(r + 1) * HEAD_DIM] = out.astype(o_ref.dtype)

    _for_causal_extent(start + NSA_QBLOCK, seq, selected_and_combine)


def _nsa_mixer(ra, rb, pc, pd, lay, phi_pe, phi_w1, phi_w2):
    bsz, seq, _ = ra.shape
    hpg = lay.n_bheads // B_KV_GROUPS
    gq = hpg * HEAD_DIM
    assert seq % KEY_TILE == 0 and seq >= B_WIN + NSA_QBLOCK and KEY_TILE % NSA_QBLOCK == 0
    kcmp, vcmp = _nsa_compress(rb, pd, lay, phi_pe, phi_w1, phi_w2)
    n_cmp = seq // B_CMP_STRIDE
    n_blk = seq // B_SEL_LEN
    n_sel = min(B_SEL_N, n_blk)
    assert n_blk <= LANES
    c_lo = np.arange(n_cmp)[:, None] * B_CMP_STRIDE
    b_lo = np.arange(LANES)[None, :] * B_SEL_LEN
    overlap = ((c_lo < b_lo + B_SEL_LEN) & (c_lo + B_CMP_LEN - 1 >= b_lo) & (np.arange(LANES)[None, :] < n_blk))
    expand = (np.arange(seq)[None, :] // B_SEL_LEN) == np.arange(LANES)[:, None]
    qb = lambda width, off: pl.BlockSpec((1, NSA_QBLOCK, width), lambda b, g, i: (b, i, off // width + g))
    kb = lambda off: pl.BlockSpec((1, seq, HEAD_DIM), lambda b, g, i: (b, 0, off // HEAD_DIM + g))
    cmap = lambda b, g, i: (b, g, 0, 0)
    const = lambda b, g, i: (0, 0)
    return pl.pallas_call(
        functools.partial(_nsa_kernel, n_sel=n_sel, heads_per_group=hpg),
        grid=(bsz, B_KV_GROUPS, seq // NSA_QBLOCK),
        in_specs=[qb(gq, lay.b_q), qb(LANES, lay.b_g),
                  pl.BlockSpec((1, 1, n_cmp, HEAD_DIM), cmap),
                  pl.BlockSpec((1, 1, n_cmp, HEAD_DIM), cmap),
                  kb(lay.b_ks), kb(lay.b_vs), kb(lay.b_kw), kb(lay.b_vw),
                  pl.BlockSpec((n_cmp, LANES), const),
                  pl.BlockSpec((LANES, seq), const)],
        out_specs=pl.BlockSpec((1, NSA_QBLOCK, gq), lambda b, g, i: (b, i, g)),
        out_shape=jax.ShapeDtypeStruct((bsz, seq, lay.qw), BF16),
        compiler_params=_params("parallel", "parallel", "arbitrary"),
        name="nsa_mixer",
    )(ra, pd, kcmp, vcmp, ra, pc, ra, pc, jnp.asarray(overlap, BF16), jnp.asarray(expand, BF16))


S5_TIME_BLOCK = 256
S5_UNROLL = 8


def _s5_scan_kernel(u_ref, bcat_ref, ccat_ref, a_ref, d_ref, z_ref, xs_ref, state_ref, bt_ref, tb_ref):
    bsz, tt, lanes = u_ref.shape
    ns = a_ref.shape[-1] // 2

    @pl.when(pl.program_id(1) == 0)
    def _():
        state_ref[...] = jnp.zeros_like(state_ref)

    bt_ref[...] = u_ref[...].reshape(bsz * tt, lanes)

    def to_time_major(t, carry):
        tb_ref[pl.ds(pl.multiple_of(t * bsz, bsz), bsz), :] = bt_ref[pl.ds(t, bsz, stride=tt), :]
        return carry

    lax.fori_loop(0, tt, to_time_major, 0, unroll=S5_UNROLL)
    u = tb_ref[...]
    xs_ref[...] = jnp.dot(u.astype(BF16), bcat_ref[0], preferred_element_type=F32)
    a_re = jnp.broadcast_to(a_ref[0, :, :ns], (bsz, ns))
    a_im = jnp.broadcast_to(a_ref[0, :, ns:], (bsz, ns))

    def step(t, carry):
        s_re, s_im = carry
        rows = pl.ds(pl.multiple_of(t * bsz, bsz), bsz)
        n_re = a_re * s_re - a_im * s_im + xs_ref[rows, :ns]
        n_im = a_re * s_im + a_im * s_re + xs_ref[rows, ns:]
        xs_ref[rows, :ns] = n_re
        xs_ref[rows, ns:] = n_im
        return n_re, n_im

    s_re, s_im = lax.fori_loop(0, tt, step, (state_ref[:, :ns], state_ref[:, ns:]), unroll=S5_UNROLL)
    state_ref[:, :ns] = s_re
    state_ref[:, ns:] = s_im
    y = jnp.dot(xs_ref[...].astype(BF16), ccat_ref[0], preferred_element_type=F32) + d_ref[0] * u
    tb_ref[...] = _gelu_tanh(y)
    for b in range(bsz):
        z_ref[b] = tb_ref[pl.ds(b, tt, stride=bsz), :]


def _s5_glu_kernel(za_ref, zt_ref, w_ref, b_ref, o_ref):
    zz = jnp.dot(za_ref[...].astype(BF16), w_ref[...], preferred_element_type=F32) + b_ref[...]
    o_ref[...] = (zt_ref[...] * jax.nn.sigmoid(zz)).astype(o_ref.dtype)


def _s5_mixer(pd, u_off, width, lam_re, lam_im, log_dt, b_re, b_im, c_re, c_im, d, glu_w, glu_b):
    bsz, seq, _ = pd.shape
    assert u_off % LANES == 0
    gpc = LANES // C_GROUP
    n_col = width // LANES
    ns = gpc * C_STATE
    tt = min(S5_TIME_BLOCK, seq)
    assert width % LANES == 0 and seq % tt == 0 and bsz % 8 == 0
    dt = jnp.exp(log_dt)[:, None]
    mag = jnp.exp(lam_re * dt)
    a_re, a_im = mag * jnp.cos(lam_im * dt), mag * jnp.sin(lam_im * dt)
    den = lam_re * lam_re + lam_im * lam_im
    co_re = ((a_re - 1.0) * lam_re + a_im * lam_im) / den
    co_im = (a_im * lam_re - (a_re - 1.0) * lam_im) / den
    bb_re = co_re[..., None] * b_re - co_im[..., None] * b_im
    bb_im = co_re[..., None] * b_im + co_im[..., None] * b_re
    eye = jnp.eye(gpc, dtype=F32)

    def block_diag(m):
        r, c = m.shape[-2:]
        return jnp.einsum('ngrc,gk->ngrkc', m, eye).reshape(n_col, gpc * r, gpc * c)

    to_in = lambda m: block_diag(m.reshape(n_col, gpc, C_STATE, C_GROUP).transpose(0, 1, 3, 2))
    to_out = lambda m: block_diag(m.reshape(n_col, gpc, C_GROUP, C_STATE).transpose(0, 1, 3, 2))
    bcat = jnp.concatenate([to_in(bb_re), to_in(bb_im)], axis=-1).astype(BF16)
    ccat = jnp.concatenate([to_out(c_re), to_out(-c_im)], axis=-2).astype(BF16)
    acat = jnp.concatenate([a_re.reshape(n_col, 1, ns), a_im.reshape(n_col, 1, ns)], axis=-1)
    z = pl.pallas_call(
        _s5_scan_kernel,
        grid=(n_col, seq // tt),
        in_specs=[pl.BlockSpec((bsz, tt, LANES), lambda c, t: (0, t, u_off // LANES + c)),
                  pl.BlockSpec((1, LANES, 2 * ns), lambda c, t: (c, 0, 0)),
                  pl.BlockSpec((1, 2 * ns, LANES), lambda c, t: (c, 0, 0)),
                  pl.BlockSpec((1, 1, 2 * ns), lambda c, t: (c, 0, 0)),
                  pl.BlockSpec((1, 1, LANES), lambda c, t: (c, 0, 0))],
        out_specs=pl.BlockSpec((bsz, tt, LANES), lambda c, t: (0, t, c)),
        out_shape=jax.ShapeDtypeStruct((bsz, seq, width), F32),
        scratch_shapes=[pltpu.VMEM((tt * bsz, 2 * ns), F32), pltpu.VMEM((bsz, 2 * ns), F32),
                        pltpu.VMEM((bsz * tt, LANES), F32), pltpu.VMEM((tt * bsz, LANES), F32)],
        compiler_params=_params("parallel", "arbitrary"),
        name="s5_scan",
    )(pd, bcat, ccat, acat, d.reshape(n_col, 1, LANES))
    z = z.reshape(bsz * seq, width)
    tm, tn = min(ROW_TILE, seq * bsz), min(512, width)
    return pl.pallas_call(
        _s5_glu_kernel,
        grid=(seq * bsz // tm, width // tn),
        in_specs=[pl.BlockSpec((tm, width), lambda i, j: (i, 0)),
                  pl.BlockSpec((tm, tn), lambda i, j: (i, j)),
                  pl.BlockSpec((width, tn), lambda i, j: (0, j)),
                  pl.BlockSpec((1, tn), lambda i, j: (0, j))],
        out_specs=pl.BlockSpec((tm, tn), lambda i, j: (i, j)),
        out_shape=jax.ShapeDtypeStruct((seq * bsz, width), BF16),
        compiler_params=_params("parallel", "parallel"),
        name="s5_glu",
    )(z, z, glu_w.astype(BF16), glu_b.reshape(1, width))


def _mixing_sublayer(hb, bsz, seq, rope, lay, w_in, phi_pe, phi_w1, phi_w2, lam_re, lam_im, log_dt,
                     b_re, b_im, c_re, c_im, s5_d, glu_w, glu_b, w_out):
    w_ra, w_rb, w_pc, w_pd = lay.weights(w_in)
    view = lambda t: t.reshape(bsz, seq, t.shape[-1])
    ra = view(_matmul(hb, w_ra, lay.ra_tile, BF16, rope))
    rb = view(_matmul(hb, w_rb, w_rb.shape[1], F32, rope))
    pc = view(_matmul(hb, w_pc, w_pc.shape[1], BF16))
    pd = view(_matmul(hb, w_pd, lay.pd_tile, F32))
    o_a = _dsa_mixer(ra, pc, pd, lay)
    o_b = _nsa_mixer(ra, rb, pc, pd, lay, phi_pe, phi_w1, phi_w2)
    o_c = _s5_mixer(pd, lay.c_u, lay.cw, lam_re, lam_im, log_dt, b_re, b_im, c_re, c_im, s5_d, glu_w, glu_b)
    return _out_proj(o_a.reshape(bsz * seq, -1), o_b.reshape(bsz * seq, -1), o_c, w_out.astype(BF16))


ROUTER_ROWS = 256


def _split_bf16(x):
    hi = x.astype(BF16)
    return hi, (x - hi.astype(F32)).astype(BF16)


def _router_kernel(x_ref, wh_ref, wl_ref, bias_ref, eid_ref, gate_ref):
    x_hi, x_lo = _split_bf16(x_ref[...])
    wh = wh_ref[...]
    logits = (jnp.dot(x_hi, wh, preferred_element_type=F32) + jnp.dot(x_hi, wl_ref[...], preferred_element_type=F32)
              + jnp.dot(x_lo, wh, preferred_element_type=F32))
    aff = jax.nn.sigmoid(logits)
    lane_i = lax.broadcasted_iota(jnp.int32, (1, LANES), 1)
    lane = lane_i.astype(F32)
    grp = (lane_i // EXPERTS_PER_GROUP).astype(F32)
    sel = jnp.where(lane_i < N_EXPERTS, aff + bias_ref[...], -jnp.inf)

    def top2(v):
        m1 = jnp.max(v, axis=-1, keepdims=True)
        i1 = jnp.min(jnp.where(v == m1, lane, float(LANES)), axis=-1, keepdims=True)
        v2 = jnp.where(lane == i1, -jnp.inf, v)
        m2 = jnp.max(v2, axis=-1, keepdims=True)
        i2 = jnp.min(jnp.where(v2 == m2, lane, float(LANES)), axis=-1, keepdims=True)
        return m1, i1, m2, i2

    best_score = best_grp = None
    for g in range(N_EXPERT_GROUPS):
        m1, _, m2, _ = top2(jnp.where(grp == float(g), sel, -jnp.inf))
        score = m1 + m2
        if g == 0:
            best_score, best_grp = score, jnp.zeros_like(score)
        else:
            better = score > best_score
            best_score = jnp.where(better, score, best_score)
            best_grp = jnp.where(better, float(g), best_grp)
    _, i1, _, i2 = top2(jnp.where(grp == best_grp, sel, -jnp.inf))
    w1 = jnp.sum(jnp.where(lane == i1, aff, 0.0), axis=-1, keepdims=True)
    w2 = jnp.sum(jnp.where(lane == i2, aff, 0.0), axis=-1, keepdims=True)
    tot = w1 + w2
    eid_ref[...] = jnp.where(lane_i == 0, i1, jnp.where(lane_i == 1, i2, 0.0)).astype(jnp.int32)
    gate_ref[...] = jnp.where(lane_i == 0, w1 / tot, jnp.where(lane_i == 1, w2 / tot, 0.0))


def _route(xt, router_w, router_bias):
    n_tok, dm = xt.shape
    assert n_tok % ROUTER_ROWS == 0 and N_EXPERTS <= LANES
    pad = LANES - N_EXPERTS
    wh, wl = _split_bf16(jnp.pad(router_w, ((0, 0), (0, pad))))
    bias = jnp.pad(router_bias, (0, pad)).reshape(1, LANES)
    rows = lambda w: pl.BlockSpec((ROUTER_ROWS, w), lambda i: (i, 0))
    const = lambda shape: pl.BlockSpec(shape, lambda i: (0, 0))
    eid, gate = pl.pallas_call(
        _router_kernel, grid=(n_tok // ROUTER_ROWS,),
        in_specs=[rows(dm), const((dm, LANES)), const((dm, LANES)), const((1, LANES))],
        out_specs=[rows(LANES), rows(LANES)],
        out_shape=[jax.ShapeDtypeStruct((n_tok, LANES), jnp.int32), jax.ShapeDtypeStruct((n_tok, LANES), F32)],
        compiler_params=_params("parallel"), name="moe_router",
    )(xt, wh, wl, bias)
    return eid[:, :TOP_K], gate[:, :TOP_K]


def _moe(xt, router_w, router_bias, layer, w_gate, w_up, w_down):
    n_tok, dm = xt.shape
    eid, gate = _route(xt, router_w, router_bias)
    n_asg = n_tok * TOP_K
    cap = -(-n_asg // MOE_BLOCK) * MOE_BLOCK + N_EXPERTS * MOE_BLOCK
    onehot = (eid.reshape(n_asg, 1) == jnp.arange(N_EXPERTS, dtype=jnp.int32)[None, :]).astype(jnp.int32)
    seen = jnp.cumsum(onehot, axis=0)
    counts = seen[-1]
    padded = (counts + MOE_BLOCK - 1) // MOE_BLOCK * MOE_BLOCK
    pend = jnp.cumsum(padded)
    pstart = pend - padded
    dest = jnp.sum(onehot * (seen - 1 + pstart[None, :]), axis=1).astype(jnp.int32)
    tok_flat = jnp.arange(n_asg, dtype=jnp.int32) // TOP_K
    slot_tok = (jnp.arange(cap, dtype=jnp.int32) % n_tok).at[dest].set(tok_flat)
    n_blocks = cap // MOE_BLOCK
    blk_start = jnp.arange(n_blocks) * MOE_BLOCK
    blk_exp = jnp.minimum(jnp.sum(pend[None, :] <= blk_start[:, None], axis=1), N_EXPERTS - 1).astype(jnp.int32)
    blk_first = jnp.concatenate([jnp.ones((1,), jnp.int32), (blk_exp[1:] != blk_exp[:-1]).astype(jnp.int32)])
    n_act = (pend[-1] // MOE_BLOCK).astype(jnp.int32).reshape(1)
    y = _moe_experts(xt[slot_tok], blk_exp, blk_first, n_act, layer, w_gate, w_up, w_down)
    pos = dest.reshape(n_tok, TOP_K)
    return y[pos[:, 0]], y[pos[:, 1]], gate


def kernel(x, positions, ln_in_g, ln_in_b, w_in, nsa_phi_pe, nsa_phi_w1, nsa_phi_w2, s5_lam_re, s5_lam_im, s5_log_dt, s5_b_re, s5_b_im, s5_c_re, s5_c_im, s5_d, s5_glu_w, s5_glu_b, w_out, ln_mix_g, ln_mix_b, router_w, router_bias, moe_w_gate, moe_w_up, moe_w_down, ln_ffn_g, ln_ffn_b):
    depth = w_in.shape[0]
    bsz, seq, d_model = x.shape
    alpha = (2 * depth) ** 0.25
    lay = _Packed(d_model)
    rope = _rope_tables(positions)
    h, hb = _layer_norm(x.reshape(bsz * seq, d_model), ln_in_g, ln_in_b)
    for l in range(depth):
        mix = _mixing_sublayer(hb, bsz, seq, rope, lay, w_in[l], nsa_phi_pe[l], nsa_phi_w1[l], nsa_phi_w2[l],
                               s5_lam_re[l], s5_lam_im[l], s5_log_dt[l], s5_b_re[l], s5_b_im[l],
                               s5_c_re[l], s5_c_im[l], s5_d[l], s5_glu_w[l], s5_glu_b[l], w_out[l])
        h, hb = _layer_norm(h, ln_mix_g[l], ln_mix_b[l], residual=mix, alpha=alpha)
        routed = _moe(h, router_w, router_bias, l, moe_w_gate, moe_w_up, moe_w_down)
        h, hb = _layer_norm(h, ln_ffn_g[l], ln_ffn_b[l], gated_pair=routed, alpha=alpha)
    return h.reshape(bsz, seq, d_model)
```

```python
import functools
import math

import numpy as np
import jax
import jax.numpy as jnp
from jax import lax
from jax.experimental import pallas as pl
from jax.experimental.pallas import tpu as pltpu

F32 = jnp.float32
BF16 = jnp.bfloat16

LANES = 128
HEAD_DIM = 128
ROT_DIM = HEAD_DIM // 4
ROPE_THETA = 500000.0
A_IDX_HEADS = 4
A_IDX_DIM = 128
A_TOPK_MAX = 256
A_QBLOCK = 128
B_KV_GROUPS = 2
B_CMP_LEN = 32
B_CMP_STRIDE = 16
B_SEL_LEN = 64
B_SEL_N = 16
B_WIN = 512
C_GROUP = 16
C_STATE = 64
N_EXPERTS = 32
N_EXPERT_GROUPS = 4
EXPERTS_PER_GROUP = N_EXPERTS // N_EXPERT_GROUPS
TOP_K = 2
MOE_BLOCK = 256
LN_EPS = 1e-5

VMEM_LIMIT_BYTES = 56 * 1024 * 1024
ROW_TILE = 1024


def _params(*semantics):
    return pltpu.CompilerParams(dimension_semantics=semantics, vmem_limit_bytes=VMEM_LIMIT_BYTES)


def _mm_kernel(a_ref, b_ref, o_ref):
    o_ref[...] = jnp.dot(a_ref[...], b_ref[...], preferred_element_type=F32).astype(o_ref.dtype)


def _mm_rope_kernel(a_ref, b_ref, c_ref, sa_ref, sb_ref, o_ref):
    y = jnp.dot(a_ref[...], b_ref[...], preferred_element_type=F32)
    half = ROT_DIM // 2
    for hb in range(y.shape[1] // HEAD_DIM):
        cols = slice(hb * HEAD_DIM, (hb + 1) * HEAD_DIM)
        yh = y[:, cols]
        out = (yh * c_ref[...] + pltpu.roll(yh, half, 1) * sa_ref[...]
               + pltpu.roll(yh, HEAD_DIM - half, 1) * sb_ref[...])
        o_ref[:, cols] = out.astype(o_ref.dtype)


def _matmul(a, b, tn, out_dtype, rope=None):
    m, k = a.shape
    n = b.shape[1]
    tm = min(ROW_TILE, m)
    assert m % tm == 0 and n % tn == 0
    in_specs = [pl.BlockSpec((tm, k), lambda i, j: (i, 0)),
                pl.BlockSpec((k, tn), lambda i, j: (0, j))]
    args = (a, b)
    if rope is not None:
        in_specs += [pl.BlockSpec((tm, HEAD_DIM), lambda i, j: (i, 0))] * 3
        args += tuple(rope)
    return pl.pallas_call(
        _mm_kernel if rope is None else _mm_rope_kernel,
        grid=(m // tm, n // tn),
        in_specs=in_specs,
        out_specs=pl.BlockSpec((tm, tn), lambda i, j: (i, j)),
        out_shape=jax.ShapeDtypeStruct((m, n), out_dtype),
        compiler_params=_params("parallel", "parallel"),
        name="in_proj" if rope is None else "in_proj_rope",
    )(*args)


def _out_proj_kernel(a0_ref, a1_ref, a2_ref, w_ref, o_ref):
    k0 = a0_ref.shape[1]
    k1 = k0 + a1_ref.shape[1]
    acc = jnp.dot(a0_ref[...], w_ref[:k0, :], preferred_element_type=F32)
    acc = acc + jnp.dot(a1_ref[...], w_ref[k0:k1, :], preferred_element_type=F32)
    o_ref[...] = acc + jnp.dot(a2_ref[...], w_ref[k1:, :], preferred_element_type=F32)


def _out_proj(a0, a1, a2, w, tn=512):
    m = a0.shape[0]
    k, n = w.shape
    tm = min(ROW_TILE, m)
    assert a0.shape[1] + a1.shape[1] + a2.shape[1] == k and m % tm == 0 and n % tn == 0
    amap = lambda i, j: (i, 0)
    return pl.pallas_call(
        _out_proj_kernel,
        grid=(m // tm, n // tn),
        in_specs=[pl.BlockSpec((tm, a0.shape[1]), amap), pl.BlockSpec((tm, a1.shape[1]), amap),
                  pl.BlockSpec((tm, a2.shape[1]), amap), pl.BlockSpec((k, tn), lambda i, j: (0, j))],
        out_specs=pl.BlockSpec((tm, tn), lambda i, j: (i, j)),
        out_shape=jax.ShapeDtypeStruct((m, n), F32),
        compiler_params=_params("parallel", "parallel"),
        name="out_proj",
    )(a0, a1, a2, w)


LN_ROWS = 256


def _ln_store(v, g_ref, b_ref, o_ref, ob_ref):
    mu = jnp.mean(v, axis=-1, keepdims=True)
    c = v - mu
    var = jnp.mean(c * c, axis=-1, keepdims=True)
    out = c * lax.rsqrt(var + LN_EPS) * g_ref[...] + b_ref[...]
    o_ref[...] = out
    ob_ref[...] = out.astype(BF16)


def _ln_kernel(x_ref, g_ref, b_ref, o_ref, ob_ref):
    _ln_store(x_ref[...], g_ref, b_ref, o_ref, ob_ref)


def _ln_residual_kernel(x_ref, r_ref, g_ref, b_ref, o_ref, ob_ref, *, alpha):
    _ln_store(alpha * x_ref[...] + r_ref[...], g_ref, b_ref, o_ref, ob_ref)


def _ln_gated_pair_kernel(x_ref, y0_ref, y1_ref, w_ref, g_ref, b_ref, o_ref, ob_ref, *, alpha):
    w = w_ref[...]
    ffn = w[:, 0:1] * y0_ref[...] + w[:, 1:2] * y1_ref[...]
    _ln_store(alpha * x_ref[...] + ffn, g_ref, b_ref, o_ref, ob_ref)


def _layer_norm(x, g, b, residual=None, alpha=1.0, gated_pair=None):
    m, d = x.shape
    rows = pl.BlockSpec((LN_ROWS, d), lambda i: (i, 0))
    vec = pl.BlockSpec((1, d), lambda i: (0, 0))
    assert m % LN_ROWS == 0
    if gated_pair is not None:
        y0, y1, w = gated_pair
        body = functools.partial(_ln_gated_pair_kernel, alpha=alpha)
        ins = [rows, rows, rows, pl.BlockSpec((LN_ROWS, w.shape[1]), lambda i: (i, 0)), vec, vec]
        args = (x, y0, y1, w, g.reshape(1, d), b.reshape(1, d))
    elif residual is None:
        body, ins, args = _ln_kernel, [rows, vec, vec], (x, g.reshape(1, d), b.reshape(1, d))
    else:
        body = functools.partial(_ln_residual_kernel, alpha=alpha)
        ins, args = [rows, rows, vec, vec], (x, residual, g.reshape(1, d), b.reshape(1, d))
    return pl.pallas_call(
        body, grid=(m // LN_ROWS,), in_specs=ins, out_specs=[rows, rows],
        out_shape=[jax.ShapeDtypeStruct((m, d), F32), jax.ShapeDtypeStruct((m, d), BF16)],
        compiler_params=_params("parallel"), name="layer_norm",
    )(*args)


MOE_SPLIT = 2


def _moe_up_kernel(blk_exp_ref, first_ref, n_act_ref, x_ref, wg_ref, wu_ref, hid_ref, wres_ref, acc_ref):
    i, k = pl.program_id(0), pl.program_id(1)
    active = i < n_act_ref[0]

    de = hid_ref.shape[1]

    @pl.when(active & (first_ref[i] == 1))
    def _():
        wres_ref[k, :, :de] = wg_ref[0, 0].astype(BF16)
        wres_ref[k, :, de:] = wu_ref[0, 0].astype(BF16)

    @pl.when(active)
    def _():
        gu = jnp.dot(x_ref[...].astype(BF16), wres_ref[k], preferred_element_type=F32)

        @pl.when(k == 0)
        def _():
            acc_ref[...] = gu

        @pl.when(k > 0)
        def _():
            acc_ref[...] += gu

        @pl.when(k == MOE_SPLIT - 1)
        def _():
            g_all = acc_ref[:, :de]
            hid_ref[...] = (g_all * jax.nn.sigmoid(g_all) * acc_ref[:, de:]).astype(hid_ref.dtype)

    @pl.when(jnp.logical_not(active) & (k == MOE_SPLIT - 1))
    def _():
        hid_ref[...] = jnp.zeros_like(hid_ref)


def _moe_down_kernel(blk_exp_ref, first_ref, n_act_ref, hid_ref, wd_ref, y_ref, wres_ref):
    i, n = pl.program_id(0), pl.program_id(1)
    active = i < n_act_ref[0]

    @pl.when(active & (first_ref[i] == 1))
    def _():
        wres_ref[n] = wd_ref[0, 0].astype(BF16)

    @pl.when(active)
    def _():
        y_ref[...] = jnp.dot(hid_ref[...], wres_ref[n], preferred_element_type=F32)

    @pl.when(jnp.logical_not(active))
    def _():
        y_ref[...] = jnp.zeros_like(y_ref)


def _moe_experts(xs, blk_exp, blk_first, n_act, layer, wg, wu, wd):
    cap, dm = xs.shape
    de = wg.shape[-1]
    n_blocks = cap // MOE_BLOCK
    tk, tn = dm // MOE_SPLIT, dm // MOE_SPLIT
    tile = lambda j, first: jnp.where(first == 1, j, MOE_SPLIT - 1)
    hid = pl.pallas_call(
        _moe_up_kernel,
        grid_spec=pltpu.PrefetchScalarGridSpec(
            num_scalar_prefetch=3, grid=(n_blocks, MOE_SPLIT),
            in_specs=[pl.BlockSpec((MOE_BLOCK, tk), lambda i, k, be, bf, na: (i, k)),
                      pl.BlockSpec((1, 1, tk, de), lambda i, k, be, bf, na: (layer, be[i], tile(k, bf[i]), 0)),
                      pl.BlockSpec((1, 1, tk, de), lambda i, k, be, bf, na: (layer, be[i], tile(k, bf[i]), 0))],
            out_specs=pl.BlockSpec((MOE_BLOCK, de), lambda i, k, be, bf, na: (i, 0)),
            scratch_shapes=[pltpu.VMEM((MOE_SPLIT, tk, 2 * de), BF16), pltpu.VMEM((MOE_BLOCK, 2 * de), F32)]),
        out_shape=jax.ShapeDtypeStruct((cap, de), BF16),
        compiler_params=_params("arbitrary", "arbitrary"),
        name="moe_gate_up",
    )(blk_exp, blk_first, n_act, xs, wg, wu)
    return pl.pallas_call(
        _moe_down_kernel,
        grid_spec=pltpu.PrefetchScalarGridSpec(
            num_scalar_prefetch=3, grid=(n_blocks, MOE_SPLIT),
            in_specs=[pl.BlockSpec((MOE_BLOCK, de), lambda i, n, be, bf, na: (i, 0)),
                      pl.BlockSpec((1, 1, de, tn), lambda i, n, be, bf, na: (layer, be[i], 0, tile(n, bf[i])))],
            out_specs=pl.BlockSpec((MOE_BLOCK, tn), lambda i, n, be, bf, na: (i, n)),
            scratch_shapes=[pltpu.VMEM((MOE_SPLIT, de, tn), BF16)]),
        out_shape=jax.ShapeDtypeStruct((cap, dm), F32),
        compiler_params=_params("arbitrary", "arbitrary"),
        name="moe_down",
    )(blk_exp, blk_first, n_act, hid, wd)


INT32_MIN = -2 ** 31
BISECT_UNROLL = 8


def _order_key(x):
    b = lax.bitcast_convert_type(x + 0.0, jnp.int32)
    return b ^ ((b >> 31) & 0x7FFFFFFF)


def _count(mask):
    return jnp.sum(jnp.where(mask, 1.0, 0.0), axis=-1, keepdims=True)


def _topk_mask(key, pos, k, n_pos_bits):
    kf = float(k)
    thr = jnp.where(_count(key >= 0) >= kf, 0, INT32_MIN).astype(jnp.int32)

    def value_step(i, thr):
        cand = thr + jnp.left_shift(jnp.int32(1), 30 - i)
        return jnp.where(_count(key >= cand) >= kf, cand, thr)

    thr = lax.fori_loop(0, 31, value_step, thr, unroll=BISECT_UNROLL)
    above = key > thr
    n_pos = 1 << n_pos_bits
    tied_pos = jnp.where(key == thr, pos, n_pos)
    need = kf - _count(above)

    def take_lowest(_):
        def pos_step(i, last):
            cand = last + jnp.left_shift(jnp.int32(1), n_pos_bits - 1 - i)
            return jnp.where(_count(tied_pos < cand) < need, cand, last)

        return lax.fori_loop(0, n_pos_bits, pos_step, jnp.zeros_like(thr), unroll=BISECT_UNROLL)

    def take_all(_):
        return jnp.full_like(thr, n_pos - 1)

    surplus = jnp.max(_count(tied_pos < n_pos) - need)
    last = lax.cond(surplus > 0.0, take_lowest, take_all, None)
    return above | (tied_pos <= last)


def _topk_mask_by_rank(key, pos, k, n):
    rank = jnp.zeros(key.shape, F32)
    for m in range(n):
        rank = rank + jnp.where(key[:, m:m + 1] > key - jnp.where(pos > m, 1, 0), 1.0, 0.0)
    return (rank < float(k)) & (pos < n)


def _mask_bias(mask):
    return jnp.where(mask, 0.0, -jnp.inf)


def _softmax_weights(s, bias):
    s = s + bias
    m = jnp.max(s, axis=-1, keepdims=True)
    m = jnp.where(jnp.isfinite(m), m, 0.0)
    e = jnp.exp(s - m)
    return e, jnp.maximum(jnp.sum(e, axis=-1, keepdims=True), 1e-30)


def _dot_nt(a, b):
    return lax.dot_general(a, b, (((1,), (1,)), ((), ())), preferred_element_type=F32)


def _attend(q, k, v, bias, scale):
    s = (_dot_nt(q, k) * scale).astype(BF16) + bias
    m = jnp.max(s, axis=-1, keepdims=True).astype(F32)
    m = jnp.where(jnp.isfinite(m), m, 0.0).astype(BF16)
    e = jnp.exp(s - m)
    both = jnp.dot(e, v, preferred_element_type=F32)
    dv = v.shape[1] // 2
    return both[:, :dv] / jnp.maximum(both[:, dv:], 1e-30)


def _with_ones(v):
    return jnp.concatenate([v, jnp.ones(v.shape, v.dtype)], axis=1)


KEY_TILE = 512


def _for_causal_extent(q_end, seq, body):
    for klen in range(KEY_TILE, seq + 1, KEY_TILE):
        pl.when((q_end > klen - KEY_TILE) & (q_end <= klen))(functools.partial(body, klen))


class _Packed:
    def __init__(self, d_model):
        self.qw = d_model // 4
        self.gw = B_KV_GROUPS * HEAD_DIM
        self.iqw = A_IDX_HEADS * A_IDX_DIM
        self.cw = d_model // 2
        self.n_bheads = self.qw // HEAD_DIM
        self.gate_cols = 3 * self.n_bheads // B_KV_GROUPS
        qw, gw = self.qw, self.gw
        self.in_sizes = (qw, HEAD_DIM, HEAD_DIM, self.iqw, A_IDX_DIM, A_IDX_HEADS, qw,
                         gw, gw, gw, gw, gw, gw, 3 * self.n_bheads, self.cw)
        self.ra_tile = 512
        o = 0
        self.a_q = o; o += qw
        self.b_q = o; o += qw
        self.a_iq = o; o += self.iqw
        self.b_ks = o; o += gw
        self.b_kw = o; o += gw
        self.a_k = o; o += HEAD_DIM
        self.a_ik = o; o += A_IDX_DIM
        self.ra_width = -(-o // self.ra_tile) * self.ra_tile
        self.b_vs, self.b_vw, self.a_v = 0, gw, 2 * gw
        self.pc_width = 2 * gw + HEAD_DIM
        self.c_u = 0
        self.b_vc = self.cw
        self.a_iw = self.b_vc + gw
        self.b_g = self.a_iw + LANES
        self.pd_width = self.b_g + B_KV_GROUPS * LANES
        self.pd_tile = next(t for t in (896, 768, 640, 512, 384, 256, 128) if self.pd_width % t == 0)
        assert qw % 512 == 0 and self.iqw % 512 == 0 and self.a_iq % 512 == 0

    def weights(self, w_in):
        cuts = np.cumsum(self.in_sizes)[:-1].tolist()
        (a_q, a_k, a_v, a_iq, a_ik, a_iw, b_q, b_kc, b_vc, b_ks, b_vs, b_kw, b_vw, b_g,
         c_u) = jnp.split(w_in.astype(BF16), cuts, axis=1)
        zeros = lambda n: jnp.zeros((w_in.shape[0], n), BF16)
        ra = [a_q, b_q, a_iq, b_ks, b_kw, a_k, a_ik]
        ra.append(zeros(self.ra_width - sum(t.shape[1] for t in ra)))
        pd = [c_u, b_vc, a_iw, zeros(LANES - a_iw.shape[1])]
        for g in range(B_KV_GROUPS):
            pd += [b_g[:, g * self.gate_cols:(g + 1) * self.gate_cols], zeros(LANES - self.gate_cols)]
        cat = lambda ts: jnp.concatenate(ts, axis=1)
        return cat(ra), b_kc, cat([b_vs, b_vw, a_v]), cat(pd)


def _rope_tables(positions):
    half = ROT_DIM // 2
    inv = ROPE_THETA ** (-jnp.arange(0, ROT_DIM, 2, dtype=F32) / ROT_DIM)
    ang = positions.astype(F32).reshape(-1, 1) * inv
    cos, sin = jnp.cos(ang), jnp.sin(ang)
    n = ang.shape[0]
    ones, zeros, z16 = jnp.ones((n, HEAD_DIM - ROT_DIM), F32), jnp.zeros((n, HEAD_DIM - ROT_DIM), F32), jnp.zeros((n, half), F32)
    return (jnp.concatenate([cos, cos, ones], axis=1),
            jnp.concatenate([z16, sin, zeros], axis=1),
            jnp.concatenate([-sin, z16, zeros], axis=1))


def _dsa_kernel(q_ref, iq_ref, iw_ref, k_ref, v_ref, ik_ref, o_ref, *, n_keep, n_heads):
    seq = k_ref.shape[1]
    start = pl.program_id(1) * A_QBLOCK
    qpos = start + lax.broadcasted_iota(jnp.int32, (A_QBLOCK, 1), 0)

    def block(klen):
        kpos = lax.broadcasted_iota(jnp.int32, (1, klen), 1)
        causal = kpos <= qpos
        ik = ik_ref[0, :klen, :]
        iw = iw_ref[0] * (A_IDX_HEADS * A_IDX_DIM) ** -0.5
        score = jnp.zeros((A_QBLOCK, klen), F32)
        for h in range(A_IDX_HEADS):
            rel = _dot_nt(iq_ref[0, :, h * A_IDX_DIM:(h + 1) * A_IDX_DIM], ik)
            score = score + iw[:, h:h + 1] * jnp.maximum(rel, 0.0)
        score = jnp.where(causal, score, -jnp.inf)
        sel = _topk_mask(_order_key(score), kpos, n_keep, max(1, (klen - 1).bit_length())) & causal
        bias = _mask_bias(sel).astype(BF16)
        k = k_ref[0, :klen, :]
        v = _with_ones(v_ref[0, :klen, :])
        for h in range(n_heads):
            hs = slice(h * HEAD_DIM, (h + 1) * HEAD_DIM)
            o_ref[0, :, hs] = _attend(q_ref[0, :, hs], k, v, bias, HEAD_DIM ** -0.5).astype(o_ref.dtype)

    _for_causal_extent(start + A_QBLOCK, seq, block)


def _dsa_mixer(ra, pc, pd, lay):
    bsz, seq, _ = ra.shape
    n_keep = min(A_TOPK_MAX, seq // 4)
    assert seq % KEY_TILE == 0 and KEY_TILE >= n_keep
    qb = lambda width, off: pl.BlockSpec((1, A_QBLOCK, width), lambda b, i: (b, i, off // width))
    kb = lambda width, off: pl.BlockSpec((1, seq, width), lambda b, i: (b, 0, off // width))
    return pl.pallas_call(
        functools.partial(_dsa_kernel, n_keep=n_keep, n_heads=lay.qw // HEAD_DIM),
        grid=(bsz, seq // A_QBLOCK),
        in_specs=[qb(lay.qw, lay.a_q), qb(lay.iqw, lay.a_iq), qb(LANES, lay.a_iw),
                  kb(HEAD_DIM, lay.a_k), kb(HEAD_DIM, lay.a_v), kb(A_IDX_DIM, lay.a_ik)],
        out_specs=pl.BlockSpec((1, A_QBLOCK, lay.qw), lambda b, i: (b, i, 0)),
        out_shape=jax.ShapeDtypeStruct((bsz, seq, lay.qw), BF16),
        compiler_params=_params("parallel", "arbitrary"),
        name="dsa_mixer",
    )(ra, ra, pd, ra, pc, ra)


NSA_QBLOCK = 256
GELU_C = math.sqrt(2.0 / math.pi)


def _gelu_tanh(x):
    return 0.5 * x * (1.0 + jnp.tanh(GELU_C * (x + 0.044715 * (x * x * x))))


def _nsa_compress_kernel(kc_ref, vc_ref, pe_ref, w1_ref, w2_ref, ko_ref, vo_ref):
    n_chunk = ko_ref.shape[2]
    half = B_CMP_LEN // 2
    assert B_CMP_STRIDE == half
    row = lax.broadcasted_iota(jnp.int32, (n_chunk, 1), 0)
    for t, (src, dst) in enumerate(((kc_ref, ko_ref), (vc_ref, vo_ref))):
        lo = jnp.zeros((n_chunk, w1_ref.shape[-1]), F32)
        hi = jnp.zeros((n_chunk, w1_ref.shape[-1]), F32)
        for l in range(half):
            x = src[0, pl.ds(l, n_chunk, stride=B_CMP_STRIDE), :]
            lo = lo + jnp.dot((x + pe_ref[t, l:l + 1, :]).astype(BF16),
                              w1_ref[t, l * HEAD_DIM:(l + 1) * HEAD_DIM, :], preferred_element_type=F32)
            hi = hi + jnp.dot((x + pe_ref[t, half + l:half + l + 1, :]).astype(BF16),
                              w1_ref[t, (half + l) * HEAD_DIM:(half + l + 1) * HEAD_DIM, :],
                              preferred_element_type=F32)
        pre = lo + pltpu.roll(hi, n_chunk - 1, 0)
        out = jnp.dot(_gelu_tanh(pre).astype(BF16), w2_ref[t], preferred_element_type=F32)
        dst[0, 0] = jnp.where(row < n_chunk - 1, out, 0.0).astype(dst.dtype)


def _nsa_compress(rb, pd, lay, phi_pe, phi_w1, phi_w2):
    bsz, seq, _ = rb.shape
    n_chunk = seq // B_CMP_STRIDE
    cmap = lambda b, g: (0, 0, 0)
    omap = lambda b, g: (b, g, 0, 0)
    vc_blk = lay.b_vc // HEAD_DIM
    out_sds = jax.ShapeDtypeStruct((bsz, B_KV_GROUPS, n_chunk, HEAD_DIM), BF16)
    return pl.pallas_call(
        _nsa_compress_kernel,
        grid=(bsz, B_KV_GROUPS),
        in_specs=[pl.BlockSpec((1, seq, HEAD_DIM), lambda b, g: (b, 0, g)),
                  pl.BlockSpec((1, seq, HEAD_DIM), lambda b, g: (b, 0, vc_blk + g)),
                  pl.BlockSpec(phi_pe.shape, cmap),
                  pl.BlockSpec(phi_w1.shape, cmap),
                  pl.BlockSpec(phi_w2.shape, cmap)],
        out_specs=[pl.BlockSpec((1, 1, n_chunk, HEAD_DIM), omap),
                   pl.BlockSpec((1, 1, n_chunk, HEAD_DIM), omap)],
        out_shape=[out_sds, out_sds],
        compiler_params=_params("parallel", "parallel"),
        name="nsa_compress",
    )(rb, pd, phi_pe, phi_w1.astype(BF16), phi_w2.astype(BF16))


def _nsa_kernel(q_ref, gate_ref, kcmp_ref, vcmp_ref, ks_ref, vs_ref, kw_ref, vw_ref, ovl_ref, expand_ref,
                o_ref, *, n_sel, heads_per_group):
    seq = ks_ref.shape[1]
    n_cmp = kcmp_ref.shape[2]
    n_blk = seq // B_SEL_LEN
    scale = HEAD_DIM ** -0.5
    start = pl.program_id(2) * NSA_QBLOCK
    qpos = start + lax.broadcasted_iota(jnp.int32, (NSA_QBLOCK, 1), 0)
    heads = [q_ref[0, :, r * HEAD_DIM:(r + 1) * HEAD_DIM] for r in range(heads_per_group)]

    cend = lax.broadcasted_iota(jnp.int32, (1, n_cmp), 1) * B_CMP_STRIDE + (B_CMP_LEN - 1)
    cmask = _mask_bias(cend <= qpos)
    kcmp = kcmp_ref[0, 0]
    vcmp = vcmp_ref[0, 0]
    o_cmp = []
    psum = jnp.zeros((NSA_QBLOCK, n_cmp), F32)
    for q in heads:
        e, denom = _softmax_weights(_dot_nt(q, kcmp) * scale, cmask)
        p = e / denom
        psum = psum + p
        o_cmp.append(jnp.dot(p.astype(BF16), vcmp, preferred_element_type=F32))
    p_hi = psum.astype(BF16)
    p_lo = (psum - p_hi.astype(F32)).astype(BF16)
    ovl = ovl_ref[...]
    imp = (jnp.dot(p_hi, ovl, preferred_element_type=F32) + jnp.dot(p_lo, ovl, preferred_element_type=F32))
    blk = lax.broadcasted_iota(jnp.int32, (1, imp.shape[1]), 1)
    cur = qpos // B_SEL_LEN
    imp = jnp.where((blk == 0) | (blk == cur) | (blk == cur - 1), jnp.inf,
                    jnp.where(blk > cur, -jnp.inf, imp))
    blk_sel = _topk_mask_by_rank(_order_key(imp), blk, n_sel, n_blk)

    n_win = B_WIN + NSA_QBLOCK
    kstart = pl.multiple_of(jnp.maximum(start - B_WIN, 0), NSA_QBLOCK)
    wpos = kstart + lax.broadcasted_iota(jnp.int32, (1, n_win), 1)
    win_bias = _mask_bias((wpos <= qpos) & (wpos > qpos - B_WIN)).astype(BF16)
    kw = kw_ref[0, pl.ds(kstart, n_win), :]
    vw = _with_ones(vw_ref[0, pl.ds(kstart, n_win), :])
    o_win = [_attend(q, kw, vw, win_bias, scale) for q in heads]

    gates = jax.nn.sigmoid(gate_ref[0])
    sel_rows = jnp.where(blk_sel, 1.0, 0.0).astype(BF16)

    def selected_and_combine(klen):
        kpos = lax.broadcasted_iota(jnp.int32, (1, klen), 1)
        in_sel = jnp.dot(sel_rows, expand_ref[:, :klen], preferred_element_type=F32)
        sel_bias = _mask_bias((in_sel > 0.5) & (kpos <= qpos)).astype(BF16)
        ks = ks_ref[0, :klen, :]
        vs = _with_ones(vs_ref[0, :klen, :])
        for r, q in enumerate(heads):
            gr = gates[:, 3 * r:3 * r + 3]
            out = (gr[:, 0:1] * o_cmp[r] + gr[:, 1:2] * _attend(q, ks, vs, sel_bias, scale)
                   + gr[:, 2:3] * o_win[r])
            o_ref[0, :, r * HEAD_DIM:(r + 1) * HEAD_DIM] = out.astype(o_ref.dtype)

    _for_causal_extent(start + NSA_QBLOCK, seq, selected_and_combine)


def _nsa_mixer(ra, rb, pc, pd, lay, phi_pe, phi_w1, phi_w2):
    bsz, seq, _ = ra.shape
    hpg = lay.n_bheads // B_KV_GROUPS
    gq = hpg * HEAD_DIM
    assert seq % KEY_TILE == 0 and seq >= B_WIN + NSA_QBLOCK and KEY_TILE % NSA_QBLOCK == 0
    kcmp, vcmp = _nsa_compress(rb, pd, lay, phi_pe, phi_w1, phi_w2)
    n_cmp = seq // B_CMP_STRIDE
    n_blk = seq // B_SEL_LEN
    n_sel = min(B_SEL_N, n_blk)
    assert n_blk <= LANES
    c_lo = np.arange(n_cmp)[:, None] * B_CMP_STRIDE
    b_lo = np.arange(LANES)[None, :] * B_SEL_LEN
    overlap = ((c_lo < b_lo + B_SEL_LEN) & (c_lo + B_CMP_LEN - 1 >= b_lo) & (np.arange(LANES)[None, :] < n_blk))
    expand = (np.arange(seq)[None, :] // B_SEL_LEN) == np.arange(LANES)[:, None]
    qb = lambda width, off: pl.BlockSpec((1, NSA_QBLOCK, width), lambda b, g, i: (b, i, off // width + g))
    kb = lambda off: pl.BlockSpec((1, seq, HEAD_DIM), lambda b, g, i: (b, 0, off // HEAD_DIM + g))
    cmap = lambda b, g, i: (b, g, 0, 0)
    const = lambda b, g, i: (0, 0)
    return pl.pallas_call(
        functools.partial(_nsa_kernel, n_sel=n_sel, heads_per_group=hpg),
        grid=(bsz, B_KV_GROUPS, seq // NSA_QBLOCK),
        in_specs=[qb(gq, lay.b_q), qb(LANES, lay.b_g),
                  pl.BlockSpec((1, 1, n_cmp, HEAD_DIM), cmap),
                  pl.BlockSpec((1, 1, n_cmp, HEAD_DIM), cmap),
                  kb(lay.b_ks), kb(lay.b_vs), kb(lay.b_kw), kb(lay.b_vw),
                  pl.BlockSpec((n_cmp, LANES), const),
                  pl.BlockSpec((LANES, seq), const)],
        out_specs=pl.BlockSpec((1, NSA_QBLOCK, gq), lambda b, g, i: (b, i, g)),
        out_shape=jax.ShapeDtypeStruct((bsz, seq, lay.qw), BF16),
        compiler_params=_params("parallel", "parallel", "arbitrary"),
        name="nsa_mixer",
    )(ra, pd, kcmp, vcmp, ra, pc, ra, pc, jnp.asarray(overlap, BF16), jnp.asarray(expand, BF16))


S5_TIME_BLOCK = 256
S5_UNROLL = 8


def _s5_scan_kernel(u_ref, bcat_ref, ccat_ref, a_ref, d_ref, z_ref, xs_ref, state_ref, bt_ref, tb_ref):
    bsz, tt, lanes = u_ref.shape
    ns = a_ref.shape[-1] // 2

    @pl.when(pl.program_id(1) == 0)
    def _():
        state_ref[...] = jnp.zeros_like(state_ref)

    bt_ref[...] = u_ref[...].reshape(bsz * tt, lanes)

    def to_time_major(t, carry):
        tb_ref[pl.ds(pl.multiple_of(t * bsz, bsz), bsz), :] = bt_ref[pl.ds(t, bsz, stride=tt), :]
        return carry

    lax.fori_loop(0, tt, to_time_major, 0, unroll=S5_UNROLL)
    u = tb_ref[...]
    xs_ref[...] = jnp.dot(u.astype(BF16), bcat_ref[0], preferred_element_type=F32)
    a_re = jnp.broadcast_to(a_ref[0, :, :ns], (bsz, ns))
    a_im = jnp.broadcast_to(a_ref[0, :, ns:], (bsz, ns))

    def step(t, carry):
        s_re, s_im = carry
        rows = pl.ds(pl.multiple_of(t * bsz, bsz), bsz)
        n_re = a_re * s_re - a_im * s_im + xs_ref[rows, :ns]
        n_im = a_re * s_im + a_im * s_re + xs_ref[rows, ns:]
        xs_ref[rows, :ns] = n_re
        xs_ref[rows, ns:] = n_im
        return n_re, n_im

    s_re, s_im = lax.fori_loop(0, tt, step, (state_ref[:, :ns], state_ref[:, ns:]), unroll=S5_UNROLL)
    state_ref[:, :ns] = s_re
    state_ref[:, ns:] = s_im
    y = jnp.dot(xs_ref[...].astype(BF16), ccat_ref[0], preferred_element_type=F32) + d_ref[0] * u
    tb_ref[...] = _gelu_tanh(y)
    for b in range(bsz):
        z_ref[b] = tb_ref[pl.ds(b, tt, stride=bsz), :]


def _s5_glu_kernel(za_ref, zt_ref, w_ref, b_ref, o_ref):
    zz = jnp.dot(za_ref[...].astype(BF16), w_ref[...], preferred_element_type=F32) + b_ref[...]
    o_ref[...] = (zt_ref[...] * jax.nn.sigmoid(zz)).astype(o_ref.dtype)


def _s5_mixer(pd, u_off, width, lam_re, lam_im, log_dt, b_re, b_im, c_re, c_im, d, glu_w, glu_b):
    bsz, seq, _ = pd.shape
    assert u_off % LANES == 0
    gpc = LANES // C_GROUP
    n_col = width // LANES
    ns = gpc * C_STATE
    tt = min(S5_TIME_BLOCK, seq)
    assert width % LANES == 0 and seq % tt == 0 and bsz % 8 == 0
    dt = jnp.exp(log_dt)[:, None]
    mag = jnp.exp(lam_re * dt)
    a_re, a_im = mag * jnp.cos(lam_im * dt), mag * jnp.sin(lam_im * dt)
    den = lam_re * lam_re + lam_im * lam_im
    co_re = ((a_re - 1.0) * lam_re + a_im * lam_im) / den
    co_im = (a_im * lam_re - (a_re - 1.0) * lam_im) / den
    bb_re = co_re[..., None] * b_re - co_im[..., None] * b_im
    bb_im = co_re[..., None] * b_im + co_im[..., None] * b_re
    eye = jnp.eye(gpc, dtype=F32)

    def block_diag(m):
        r, c = m.shape[-2:]
        return jnp.einsum('ngrc,gk->ngrkc', m, eye).reshape(n_col, gpc * r, gpc * c)

    to_in = lambda m: block_diag(m.reshape(n_col, gpc, C_STATE, C_GROUP).transpose(0, 1, 3, 2))
    to_out = lambda m: block_diag(m.reshape(n_col, gpc, C_GROUP, C_STATE).transpose(0, 1, 3, 2))
    bcat = jnp.concatenate([to_in(bb_re), to_in(bb_im)], axis=-1).astype(BF16)
    ccat = jnp.concatenate([to_out(c_re), to_out(-c_im)], axis=-2).astype(BF16)
    acat = jnp.concatenate([a_re.reshape(n_col, 1, ns), a_im.reshape(n_col, 1, ns)], axis=-1)
    z = pl.pallas_call(
        _s5_scan_kernel,
        grid=(n_col, seq // tt),
        in_specs=[pl.BlockSpec((bsz, tt, LANES), lambda c, t: (0, t, u_off // LANES + c)),
                  pl.BlockSpec((1, LANES, 2 * ns), lambda c, t: (c, 0, 0)),
                  pl.BlockSpec((1, 2 * ns, LANES), lambda c, t: (c, 0, 0)),
                  pl.BlockSpec((1, 1, 2 * ns), lambda c, t: (c, 0, 0)),
                  pl.BlockSpec((1, 1, LANES), lambda c, t: (c, 0, 0))],
        out_specs=pl.BlockSpec((bsz, tt, LANES), lambda c, t: (0, t, c)),
        out_shape=jax.ShapeDtypeStruct((bsz, seq, width), F32),
        scratch_shapes=[pltpu.VMEM((tt * bsz, 2 * ns), F32), pltpu.VMEM((bsz, 2 * ns), F32),
                        pltpu.VMEM((bsz * tt, LANES), F32), pltpu.VMEM((tt * bsz, LANES), F32)],
        compiler_params=_params("parallel", "arbitrary"),
        name="s5_scan",
    )(pd, bcat, ccat, acat, d.reshape(n_col, 1, LANES))
    z = z.reshape(bsz * seq, width)
    tm, tn = min(ROW_TILE, seq * bsz), min(512, width)
    return pl.pallas_call(
        _s5_glu_kernel,
        grid=(seq * bsz // tm, width // tn),
        in_specs=[pl.BlockSpec((tm, width), lambda i, j: (i, 0)),
                  pl.BlockSpec((tm, tn), lambda i, j: (i, j)),
                  pl.BlockSpec((width, tn), lambda i, j: (0, j)),
                  pl.BlockSpec((1, tn), lambda i, j: (0, j))],
        out_specs=pl.BlockSpec((tm, tn), lambda i, j: (i, j)),
        out_shape=jax.ShapeDtypeStruct((seq * bsz, width), BF16),
        compiler_params=_params("parallel", "parallel"),
        name="s5_glu",
    )(z, z, glu_w.astype(BF16), glu_b.reshape(1, width))


def _mixing_sublayer(hb, bsz, seq, rope, lay, w_in, phi_pe, phi_w1, phi_w2, lam_re, lam_im, log_dt,
                     b_re, b_im, c_re, c_im, s5_d, glu_w, glu_b, w_out):
    w_ra, w_rb, w_pc, w_pd = lay.weights(w_in)
    view = lambda t: t.reshape(bsz, seq, t.shape[-1])
    ra = view(_matmul(hb, w_ra, lay.ra_tile, BF16, rope))
    rb = view(_matmul(hb, w_rb, w_rb.shape[1], F32, rope))
    pc = view(_matmul(hb, w_pc, w_pc.shape[1], BF16))
    pd = view(_matmul(hb, w_pd, lay.pd_tile, F32))
    o_a = _dsa_mixer(ra, pc, pd, lay)
    o_b = _nsa_mixer(ra, rb, pc, pd, lay, phi_pe, phi_w1, phi_w2)
    o_c = _s5_mixer(pd, lay.c_u, lay.cw, lam_re, lam_im, log_dt, b_re, b_im, c_re, c_im, s5_d, glu_w, glu_b)
    return _out_proj(o_a.reshape(bsz * seq, -1), o_b.reshape(bsz * seq, -1), o_c, w_out.astype(BF16))


ROUTER_ROWS = 256


def _split_bf16(x):
    hi = x.astype(BF16)
    return hi, (x - hi.astype(F32)).astype(BF16)


def _router_kernel(x_ref, wh_ref, wl_ref, bias_ref, eid_ref, gate_ref):
    x_hi, x_lo = _split_bf16(x_ref[...])
    wh = wh_ref[...]
    logits = (jnp.dot(x_hi, wh, preferred_element_type=F32) + jnp.dot(x_hi, wl_ref[...], preferred_element_type=F32)
              + jnp.dot(x_lo, wh, preferred_element_type=F32))
    aff = jax.nn.sigmoid(logits)
    lane_i = lax.broadcasted_iota(jnp.int32, (1, LANES), 1)
    lane = lane_i.astype(F32)
    grp = (lane_i // EXPERTS_PER_GROUP).astype(F32)
    sel = jnp.where(lane_i < N_EXPERTS, aff + bias_ref[...], -jnp.inf)

    def top2(v):
        m1 = jnp.max(v, axis=-1, keepdims=True)
        i1 = jnp.min(jnp.where(v == m1, lane, float(LANES)), axis=-1, keepdims=True)
        v2 = jnp.where(lane == i1, -jnp.inf, v)
        m2 = jnp.max(v2, axis=-1, keepdims=True)
        i2 = jnp.min(jnp.where(v2 == m2, lane, float(LANES)), axis=-1, keepdims=True)
        return m1, i1, m2, i2

    best_score = best_grp = None
    for g in range(N_EXPERT_GROUPS):
        m1, _, m2, _ = top2(jnp.where(grp == float(g), sel, -jnp.inf))
        score = m1 + m2
        if g == 0:
            best_score, best_grp = score, jnp.zeros_like(score)
        else:
            better = score > best_score
            best_score = jnp.where(better, score, best_score)
            best_grp = jnp.where(better, float(g), best_grp)
    _, i1, _, i2 = top2(jnp.where(grp == best_grp, sel, -jnp.inf))
    w1 = jnp.sum(jnp.where(lane == i1, aff, 0.0), axis=-1, keepdims=True)
    w2 = jnp.sum(jnp.where(lane == i2, aff, 0.0), axis=-1, keepdims=True)
    tot = w1 + w2
    eid_ref[...] = jnp.where(lane_i == 0, i1, jnp.where(lane_i == 1, i2, 0.0)).astype(jnp.int32)
    gate_ref[...] = jnp.where(lane_i == 0, w1 / tot, jnp.where(lane_i == 1, w2 / tot, 0.0))


def _route(xt, router_w, router_bias):
    n_tok, dm = xt.shape
    assert n_tok % ROUTER_ROWS == 0 and N_EXPERTS <= LANES
    pad = LANES - N_EXPERTS
    wh, wl = _split_bf16(jnp.pad(router_w, ((0, 0), (0, pad))))
    bias = jnp.pad(router_bias, (0, pad)).reshape(1, LANES)
    rows = lambda w: pl.BlockSpec((ROUTER_ROWS, w), lambda i: (i, 0))
    const = lambda shape: pl.BlockSpec(shape, lambda i: (0, 0))
    eid, gate = pl.pallas_call(
        _router_kernel, grid=(n_tok // ROUTER_ROWS,),
        in_specs=[rows(dm), const((dm, LANES)), const((dm, LANES)), const((1, LANES))],
        out_specs=[rows(LANES), rows(LANES)],
        out_shape=[jax.ShapeDtypeStruct((n_tok, LANES), jnp.int32), jax.ShapeDtypeStruct((n_tok, LANES), F32)],
        compiler_params=_params("parallel"), name="moe_router",
    )(xt, wh, wl, bias)
    return eid[:, :TOP_K], gate[:, :TOP_K]


def _moe(xt, router_w, router_bias, layer, w_gate, w_up, w_down):
    n_tok, dm = xt.shape
    eid, gate = _route(xt, router_w, router_bias)
    n_asg = n_tok * TOP_K
    cap = -(-n_asg // MOE_BLOCK) * MOE_BLOCK + N_EXPERTS * MOE_BLOCK
    onehot = (eid.reshape(n_asg, 1) == jnp.arange(N_EXPERTS, dtype=jnp.int32)[None, :]).astype(jnp.int32)
    seen = jnp.cumsum(onehot, axis=0)
    counts = seen[-1]
    padded = (counts + MOE_BLOCK - 1) // MOE_BLOCK * MOE_BLOCK
    pend = jnp.cumsum(padded)
    pstart = pend - padded
    dest = jnp.sum(onehot * (seen - 1 + pstart[None, :]), axis=1).astype(jnp.int32)
    tok_flat = jnp.arange(n_asg, dtype=jnp.int32) // TOP_K
    slot_tok = (jnp.arange(cap, dtype=jnp.int32) % n_tok).at[dest].set(tok_flat)
    n_blocks = cap // MOE_BLOCK
    blk_start = jnp.arange(n_blocks) * MOE_BLOCK
    blk_exp = jnp.minimum(jnp.sum(pend[None, :] <= blk_start[:, None], axis=1), N_EXPERTS - 1).astype(jnp.int32)
    blk_first = jnp.concatenate([jnp.ones((1,), jnp.int32), (blk_exp[1:] != blk_exp[:-1]).astype(jnp.int32)])
    n_act = (pend[-1] // MOE_BLOCK).astype(jnp.int32).reshape(1)
    y = _moe_experts(xt[slot_tok], blk_exp, blk_first, n_act, layer, w_gate, w_up, w_down)
    pos = dest.reshape(n_tok, TOP_K)
    return y[pos[:, 0]], y[pos[:, 1]], gate


def kernel(x, positions, ln_in_g, ln_in_b, w_in, nsa_phi_pe, nsa_phi_w1, nsa_phi_w2, s5_lam_re, s5_lam_im, s5_log_dt, s5_b_re, s5_b_im, s5_c_re, s5_c_im, s5_d, s5_glu_w, s5_glu_b, w_out, ln_mix_g, ln_mix_b, router_w, router_bias, moe_w_gate, moe_w_up, moe_w_down, ln_ffn_g, ln_ffn_b):
    depth = w_in.shape[0]
    bsz, seq, d_model = x.shape
    alpha = (2 * depth) ** 0.25
    lay = _Packed(d_model)
    rope = _rope_tables(positions)
    h, hb = _layer_norm(x.reshape(bsz * seq, d_model), ln_in_g, ln_in_b)
    for l in range(depth):
        mix = _mixing_sublayer(hb, bsz, seq, rope, lay, w_in[l], nsa_phi_pe[l], nsa_phi_w1[l], nsa_phi_w2[l],
                               s5_lam_re[l], s5_lam_im[l], s5_log_dt[l], s5_b_re[l], s5_b_im[l],
                               s5_c_re[l], s5_c_im[l], s5_d[l], s5_glu_w[l], s5_glu_b[l], w_out[l])
        h, hb = _layer_norm(h, ln_mix_g[l], ln_mix_b[l], residual=mix, alpha=alpha)
        routed = _moe(h, router_w, router_bias, l, moe_w_gate, moe_w_up, moe_w_down)
        h, hb = _layer_norm(h, ln_ffn_g[l], ln_ffn_b[l], gated_pair=routed, alpha=alpha)
    return h.reshape(bsz, seq, d_model)
```

```python
import functools
import math

import numpy as np
import jax
import jax.numpy as jnp
from jax import lax
from jax.experimental import pallas as pl
from jax.experimental.pallas import tpu as pltpu

F32 = jnp.float32
BF16 = jnp.bfloat16

LANES = 128
HEAD_DIM = 128
ROT_DIM = HEAD_DIM // 4
ROPE_THETA = 500000.0
A_IDX_HEADS = 4
A_IDX_DIM = 128
A_TOPK_MAX = 256
A_QBLOCK = 128
B_KV_GROUPS = 2
B_CMP_LEN = 32
B_CMP_STRIDE = 16
B_SEL_LEN = 64
B_SEL_N = 16
B_WIN = 512
C_GROUP = 16
C_STATE = 64
N_EXPERTS = 32
N_EXPERT_GROUPS = 4
EXPERTS_PER_GROUP = N_EXPERTS // N_EXPERT_GROUPS
TOP_K = 2
MOE_BLOCK = 256
LN_EPS = 1e-5

VMEM_LIMIT_BYTES = 56 * 1024 * 1024
ROW_TILE = 1024


def _params(*semantics):
    return pltpu.CompilerParams(dimension_semantics=semantics, vmem_limit_bytes=VMEM_LIMIT_BYTES)


def _mm_kernel(a_ref, b_ref, o_ref):
    o_ref[...] = jnp.dot(a_ref[...], b_ref[...], preferred_element_type=F32).astype(o_ref.dtype)


def _mm_rope_kernel(a_ref, b_ref, c_ref, sa_ref, sb_ref, o_ref):
    y = jnp.dot(a_ref[...], b_ref[...], preferred_element_type=F32)
    half = ROT_DIM // 2
    for hb in range(y.shape[1] // HEAD_DIM):
        cols = slice(hb * HEAD_DIM, (hb + 1) * HEAD_DIM)
        yh = y[:, cols]
        out = (yh * c_ref[...] + pltpu.roll(yh, half, 1) * sa_ref[...]
               + pltpu.roll(yh, HEAD_DIM - half, 1) * sb_ref[...])
        o_ref[:, cols] = out.astype(o_ref.dtype)


def _matmul(a, b, tn, out_dtype, rope=None):
    m, k = a.shape
    n = b.shape[1]
    tm = min(ROW_TILE, m)
    assert m % tm == 0 and n % tn == 0
    in_specs = [pl.BlockSpec((tm, k), lambda i, j: (i, 0)),
                pl.BlockSpec((k, tn), lambda i, j: (0, j))]
    args = (a, b)
    if rope is not None:
        in_specs += [pl.BlockSpec((tm, HEAD_DIM), lambda i, j: (i, 0))] * 3
        args += tuple(rope)
    return pl.pallas_call(
        _mm_kernel if rope is None else _mm_rope_kernel,
        grid=(m // tm, n // tn),
        in_specs=in_specs,
        out_specs=pl.BlockSpec((tm, tn), lambda i, j: (i, j)),
        out_shape=jax.ShapeDtypeStruct((m, n), out_dtype),
        compiler_params=_params("parallel", "parallel"),
        name="in_proj" if rope is None else "in_proj_rope",
    )(*args)


def _out_proj_kernel(a0_ref, a1_ref, a2_ref, w_ref, o_ref):
    k0 = a0_ref.shape[1]
    k1 = k0 + a1_ref.shape[1]
    acc = jnp.dot(a0_ref[...], w_ref[:k0, :], preferred_element_type=F32)
    acc = acc + jnp.dot(a1_ref[...], w_ref[k0:k1, :], preferred_element_type=F32)
    o_ref[...] = acc + jnp.dot(a2_ref[...], w_ref[k1:, :], preferred_element_type=F32)


def _out_proj(a0, a1, a2, w, tn=512):
    m = a0.shape[0]
    k, n = w.shape
    tm = min(ROW_TILE, m)
    assert a0.shape[1] + a1.shape[1] + a2.shape[1] == k and m % tm == 0 and n % tn == 0
    amap = lambda i, j: (i, 0)
    return pl.pallas_call(
        _out_proj_kernel,
        grid=(m // tm, n // tn),
        in_specs=[pl.BlockSpec((tm, a0.shape[1]), amap), pl.BlockSpec((tm, a1.shape[1]), amap),
                  pl.BlockSpec((tm, a2.shape[1]), amap), pl.BlockSpec((k, tn), lambda i, j: (0, j))],
        out_specs=pl.BlockSpec((tm, tn), lambda i, j: (i, j)),
        out_shape=jax.ShapeDtypeStruct((m, n), F32),
        compiler_params=_params("parallel", "parallel"),
        name="out_proj",
    )(a0, a1, a2, w)


LN_ROWS = 256


def _ln_store(v, g_ref, b_ref, o_ref, ob_ref):
    mu = jnp.mean(v, axis=-1, keepdims=True)
    c = v - mu
    var = jnp.mean(c * c, axis=-1, keepdims=True)
    out = c * lax.rsqrt(var + LN_EPS) * g_ref[...] + b_ref[...]
    o_ref[...] = out
    ob_ref[...] = out.astype(BF16)
    return out


def _ln_kernel(x_ref, g_ref, b_ref, o_ref, ob_ref):
    _ln_store(x_ref[...], g_ref, b_ref, o_ref, ob_ref)


def _ln_residual_route_kernel(x_ref, r_ref, g_ref, b_ref, wh_ref, wl_ref, rbias_ref, o_ref, ob_ref, eid_ref, gate_ref,
                              *, alpha):
    out = _ln_store(alpha * x_ref[...] + r_ref[...], g_ref, b_ref, o_ref, ob_ref)
    _route_rows(out, wh_ref, wl_ref, rbias_ref, eid_ref, gate_ref)


def _ln_gated_pair_kernel(x_ref, y0_ref, y1_ref, w_ref, g_ref, b_ref, o_ref, ob_ref, *, alpha):
    w = w_ref[...]
    ffn = w[:, 0:1] * y0_ref[...] + w[:, 1:2] * y1_ref[...]
    _ln_store(alpha * x_ref[...] + ffn, g_ref, b_ref, o_ref, ob_ref)


def _layer_norm_and_route(x, g, b, residual, alpha, router_w, router_bias):
    m, d = x.shape
    assert m % LN_ROWS == 0 and N_EXPERTS <= LANES
    pad = LANES - N_EXPERTS
    wh, wl = _split_bf16(jnp.pad(router_w, ((0, 0), (0, pad))))
    rbias = jnp.pad(router_bias, (0, pad)).reshape(1, LANES)
    rows = lambda w: pl.BlockSpec((LN_ROWS, w), lambda i: (i, 0))
    const = lambda shape: pl.BlockSpec(shape, lambda i: (0, 0))
    h, hb, eid, gate = pl.pallas_call(
        functools.partial(_ln_residual_route_kernel, alpha=alpha), grid=(m // LN_ROWS,),
        in_specs=[rows(d), rows(d), const((1, d)), const((1, d)), const((d, LANES)), const((d, LANES)),
                  const((1, LANES))],
        out_specs=[rows(d), rows(d), rows(LANES), rows(LANES)],
        out_shape=[jax.ShapeDtypeStruct((m, d), F32), jax.ShapeDtypeStruct((m, d), BF16),
                   jax.ShapeDtypeStruct((m, LANES), jnp.int32), jax.ShapeDtypeStruct((m, LANES), F32)],
        compiler_params=_params("parallel"), name="layer_norm_route",
    )(x, residual, g.reshape(1, d), b.reshape(1, d), wh, wl, rbias)
    return h, hb, eid[:, :TOP_K], gate[:, :TOP_K]


def _layer_norm(x, g, b, alpha=1.0, gated_pair=None):
    m, d = x.shape
    rows = pl.BlockSpec((LN_ROWS, d), lambda i: (i, 0))
    vec = pl.BlockSpec((1, d), lambda i: (0, 0))
    assert m % LN_ROWS == 0
    if gated_pair is not None:
        y0, y1, w = gated_pair
        body = functools.partial(_ln_gated_pair_kernel, alpha=alpha)
        ins = [rows, rows, rows, pl.BlockSpec((LN_ROWS, w.shape[1]), lambda i: (i, 0)), vec, vec]
        args = (x, y0, y1, w, g.reshape(1, d), b.reshape(1, d))
    else:
        body, ins, args = _ln_kernel, [rows, vec, vec], (x, g.reshape(1, d), b.reshape(1, d))
    return pl.pallas_call(
        body, grid=(m // LN_ROWS,), in_specs=ins, out_specs=[rows, rows],
        out_shape=[jax.ShapeDtypeStruct((m, d), F32), jax.ShapeDtypeStruct((m, d), BF16)],
        compiler_params=_params("parallel"), name="layer_norm",
    )(*args)


MOE_SPLIT = 2


def _moe_up_kernel(blk_exp_ref, first_ref, n_act_ref, x_ref, wg_ref, wu_ref, hid_ref, wres_ref, acc_ref):
    i, k = pl.program_id(0), pl.program_id(1)
    active = i < n_act_ref[0]

    de = hid_ref.shape[1]

    @pl.when(active & (first_ref[i] == 1))
    def _():
        wres_ref[k, :, :de] = wg_ref[0, 0].astype(BF16)
        wres_ref[k, :, de:] = wu_ref[0, 0].astype(BF16)

    @pl.when(active)
    def _():
        gu = jnp.dot(x_ref[...].astype(BF16), wres_ref[k], preferred_element_type=F32)

        @pl.when(k == 0)
        def _():
            acc_ref[...] = gu

        @pl.when(k > 0)
        def _():
            acc_ref[...] += gu

        @pl.when(k == MOE_SPLIT - 1)
        def _():
            g_all = acc_ref[:, :de]
            hid_ref[...] = (g_all * jax.nn.sigmoid(g_all) * acc_ref[:, de:]).astype(hid_ref.dtype)

    @pl.when(jnp.logical_not(active) & (k == MOE_SPLIT - 1))
    def _():
        hid_ref[...] = jnp.zeros_like(hid_ref)


def _moe_down_kernel(blk_exp_ref, first_ref, n_act_ref, hid_ref, wd_ref, y_ref, wres_ref):
    i, n = pl.program_id(0), pl.program_id(1)
    active = i < n_act_ref[0]

    @pl.when(active & (first_ref[i] == 1))
    def _():
        wres_ref[n] = wd_ref[0, 0].astype(BF16)

    @pl.when(active)
    def _():
        y_ref[...] = jnp.dot(hid_ref[...], wres_ref[n], preferred_element_type=F32)

    @pl.when(jnp.logical_not(active))
    def _():
        y_ref[...] = jnp.zeros_like(y_ref)


def _moe_experts(xs, blk_exp, blk_first, n_act, layer, wg, wu, wd):
    cap, dm = xs.shape
    de = wg.shape[-1]
    n_blocks = cap // MOE_BLOCK
    tk, tn = dm // MOE_SPLIT, dm // MOE_SPLIT
    tile = lambda j, first: jnp.where(first == 1, j, MOE_SPLIT - 1)
    hid = pl.pallas_call(
        _moe_up_kernel,
        grid_spec=pltpu.PrefetchScalarGridSpec(
            num_scalar_prefetch=3, grid=(n_blocks, MOE_SPLIT),
            in_specs=[pl.BlockSpec((MOE_BLOCK, tk), lambda i, k, be, bf, na: (i, k)),
                      pl.BlockSpec((1, 1, tk, de), lambda i, k, be, bf, na: (layer, be[i], tile(k, bf[i]), 0)),
                      pl.BlockSpec((1, 1, tk, de), lambda i, k, be, bf, na: (layer, be[i], tile(k, bf[i]), 0))],
            out_specs=pl.BlockSpec((MOE_BLOCK, de), lambda i, k, be, bf, na: (i, 0)),
            scratch_shapes=[pltpu.VMEM((MOE_SPLIT, tk, 2 * de), BF16), pltpu.VMEM((MOE_BLOCK, 2 * de), F32)]),
        out_shape=jax.ShapeDtypeStruct((cap, de), BF16),
        compiler_params=_params("arbitrary", "arbitrary"),
        name="moe_gate_up",
    )(blk_exp, blk_first, n_act, xs, wg, wu)
    return pl.pallas_call(
        _moe_down_kernel,
        grid_spec=pltpu.PrefetchScalarGridSpec(
            num_scalar_prefetch=3, grid=(n_blocks, MOE_SPLIT),
            in_specs=[pl.BlockSpec((MOE_BLOCK, de), lambda i, n, be, bf, na: (i, 0)),
                      pl.BlockSpec((1, 1, de, tn), lambda i, n, be, bf, na: (layer, be[i], 0, tile(n, bf[i])))],
            out_specs=pl.BlockSpec((MOE_BLOCK, tn), lambda i, n, be, bf, na: (i, n)),
            scratch_shapes=[pltpu.VMEM((MOE_SPLIT, de, tn), BF16)]),
        out_shape=jax.ShapeDtypeStruct((cap, dm), F32),
        compiler_params=_params("arbitrary", "arbitrary"),
        name="moe_down",
    )(blk_exp, blk_first, n_act, hid, wd)


INT32_MIN = -2 ** 31
BISECT_UNROLL = 8


def _order_key(x):
    b = lax.bitcast_convert_type(x + 0.0, jnp.int32)
    return b ^ ((b >> 31) & 0x7FFFFFFF)


def _count(mask):
    return jnp.sum(jnp.where(mask, 1.0, 0.0), axis=-1, keepdims=True)


def _topk_mask(key, pos, k, n_pos_bits):
    kf = float(k)
    thr = jnp.where(_count(key >= 0) >= kf, 0, INT32_MIN).astype(jnp.int32)

    def value_step(i, thr):
        cand = thr + jnp.left_shift(jnp.int32(1), 30 - i)
        return jnp.where(_count(key >= cand) >= kf, cand, thr)

    thr = lax.fori_loop(0, 31, value_step, thr, unroll=BISECT_UNROLL)
    above = key > thr
    n_pos = 1 << n_pos_bits
    tied_pos = jnp.where(key == thr, pos, n_pos)
    need = kf - _count(above)

    def take_lowest(_):
        def pos_step(i, last):
            cand = last + jnp.left_shift(jnp.int32(1), n_pos_bits - 1 - i)
            return jnp.where(_count(tied_pos < cand) < need, cand, last)

        return lax.fori_loop(0, n_pos_bits, pos_step, jnp.zeros_like(thr), unroll=BISECT_UNROLL)

    def take_all(_):
        return jnp.full_like(thr, n_pos - 1)

    surplus = jnp.max(_count(tied_pos < n_pos) - need)
    last = lax.cond(surplus > 0.0, take_lowest, take_all, None)
    return above | (tied_pos <= last)


def _topk_mask_by_rank(key, pos, k, n):
    rank = jnp.zeros(key.shape, F32)
    for m in range(n):
        rank = rank + jnp.where(key[:, m:m + 1] > key - jnp.where(pos > m, 1, 0), 1.0, 0.0)
    return (rank < float(k)) & (pos < n)


def _mask_bias(mask):
    return jnp.where(mask, 0.0, -jnp.inf)


def _softmax_weights(s, bias):
    s = s + bias
    m = jnp.max(s, axis=-1, keepdims=True)
    m = jnp.where(jnp.isfinite(m), m, 0.0)
    e = jnp.exp(s - m)
    return e, jnp.maximum(jnp.sum(e, axis=-1, keepdims=True), 1e-30)


def _dot_nt(a, b):
    return lax.dot_general(a, b, (((1,), (1,)), ((), ())), preferred_element_type=F32)


def _attend(q, k, v, bias, scale):
    s = (_dot_nt(q, k) * scale).astype(BF16) + bias
    m = jnp.max(s, axis=-1, keepdims=True).astype(F32)
    m = jnp.where(jnp.isfinite(m), m, 0.0).astype(BF16)
    e = jnp.exp(s - m)
    both = jnp.dot(e, v, preferred_element_type=F32)
    dv = v.shape[1] // 2
    return both[:, :dv] / jnp.maximum(both[:, dv:], 1e-30)


def _with_ones(v):
    return jnp.concatenate([v, jnp.ones(v.shape, v.dtype)], axis=1)


KEY_TILE = 512


def _for_causal_extent(q_end, seq, body):
    for klen in range(KEY_TILE, seq + 1, KEY_TILE):
        pl.when((q_end > klen - KEY_TILE) & (q_end <= klen))(functools.partial(body, klen))


class _Packed:
    def __init__(self, d_model):
        self.qw = d_model // 4
        self.gw = B_KV_GROUPS * HEAD_DIM
        self.iqw = A_IDX_HEADS * A_IDX_DIM
        self.cw = d_model // 2
        self.n_bheads = self.qw // HEAD_DIM
        self.gate_cols = 3 * self.n_bheads // B_KV_GROUPS
        qw, gw = self.qw, self.gw
        self.in_sizes = (qw, HEAD_DIM, HEAD_DIM, self.iqw, A_IDX_DIM, A_IDX_HEADS, qw,
                         gw, gw, gw, gw, gw, gw, 3 * self.n_bheads, self.cw)
        self.ra_tile = 512
        o = 0
        self.a_q = o; o += qw
        self.b_q = o; o += qw
        self.a_iq = o; o += self.iqw
        self.b_ks = o; o += gw
        self.b_kw = o; o += gw
        self.a_k = o; o += HEAD_DIM
        self.a_ik = o; o += A_IDX_DIM
        self.ra_width = -(-o // self.ra_tile) * self.ra_tile
        self.b_vs, self.b_vw, self.a_v = 0, gw, 2 * gw
        self.pc_width = 2 * gw + HEAD_DIM
        self.c_u = 0
        self.b_vc = self.cw
        self.a_iw = self.b_vc + gw
        self.b_g = self.a_iw + LANES
        self.pd_width = self.b_g + B_KV_GROUPS * LANES
        self.pd_tile = next(t for t in (896, 768, 640, 512, 384, 256, 128) if self.pd_width % t == 0)
        assert qw % 512 == 0 and self.iqw % 512 == 0 and self.a_iq % 512 == 0

    def weights(self, w_in):
        cuts = np.cumsum(self.in_sizes)[:-1].tolist()
        (a_q, a_k, a_v, a_iq, a_ik, a_iw, b_q, b_kc, b_vc, b_ks, b_vs, b_kw, b_vw, b_g,
         c_u) = jnp.split(w_in.astype(BF16), cuts, axis=1)
        zeros = lambda n: jnp.zeros((w_in.shape[0], n), BF16)
        ra = [a_q, b_q, a_iq, b_ks, b_kw, a_k, a_ik]
        ra.append(zeros(self.ra_width - sum(t.shape[1] for t in ra)))
        pd = [c_u, b_vc, a_iw, zeros(LANES - a_iw.shape[1])]
        for g in range(B_KV_GROUPS):
            pd += [b_g[:, g * self.gate_cols:(g + 1) * self.gate_cols], zeros(LANES - self.gate_cols)]
        cat = lambda ts: jnp.concatenate(ts, axis=1)
        return cat(ra), b_kc, cat([b_vs, b_vw, a_v]), cat(pd)


def _rope_tables(positions):
    half = ROT_DIM // 2
    inv = ROPE_THETA ** (-jnp.arange(0, ROT_DIM, 2, dtype=F32) / ROT_DIM)
    ang = positions.astype(F32).reshape(-1, 1) * inv
    cos, sin = jnp.cos(ang), jnp.sin(ang)
    n = ang.shape[0]
    ones, zeros, z16 = jnp.ones((n, HEAD_DIM - ROT_DIM), F32), jnp.zeros((n, HEAD_DIM - ROT_DIM), F32), jnp.zeros((n, half), F32)
    return (jnp.concatenate([cos, cos, ones], axis=1),
            jnp.concatenate([z16, sin, zeros], axis=1),
            jnp.concatenate([-sin, z16, zeros], axis=1))


def _dsa_kernel(q_ref, iq_ref, iw_ref, k_ref, v_ref, ik_ref, o_ref, *, n_keep, n_heads):
    seq = k_ref.shape[1]
    start = pl.program_id(1) * A_QBLOCK
    qpos = start + lax.broadcasted_iota(jnp.int32, (A_QBLOCK, 1), 0)

    def block(klen):
        kpos = lax.broadcasted_iota(jnp.int32, (1, klen), 1)
        causal = kpos <= qpos
        ik = ik_ref[0, :klen, :]
        iw = iw_ref[0] * (A_IDX_HEADS * A_IDX_DIM) ** -0.5
        score = jnp.zeros((A_QBLOCK, klen), F32)
        for h in range(A_IDX_HEADS):
            rel = _dot_nt(iq_ref[0, :, h * A_IDX_DIM:(h + 1) * A_IDX_DIM], ik)
            score = score + iw[:, h:h + 1] * jnp.maximum(rel, 0.0)
        score = jnp.where(causal, score, -jnp.inf)
        sel = _topk_mask(_order_key(score), kpos, n_keep, max(1, (klen - 1).bit_length())) & causal
        bias = _mask_bias(sel).astype(BF16)
        k = k_ref[0, :klen, :]
        v = _with_ones(v_ref[0, :klen, :])
        for h in range(n_heads):
            hs = slice(h * HEAD_DIM, (h + 1) * HEAD_DIM)
            o_ref[0, :, hs] = _attend(q_ref[0, :, hs], k, v, bias, HEAD_DIM ** -0.5).astype(o_ref.dtype)

    _for_causal_extent(start + A_QBLOCK, seq, block)


def _dsa_mixer(ra, pc, pd, lay):
    bsz, seq, _ = ra.shape
    n_keep = min(A_TOPK_MAX, seq // 4)
    assert seq % KEY_TILE == 0 and KEY_TILE >= n_keep
    qb = lambda width, off: pl.BlockSpec((1, A_QBLOCK, width), lambda b, i: (b, i, off // width))
    kb = lambda width, off: pl.BlockSpec((1, seq, width), lambda b, i: (b, 0, off // width))
    return pl.pallas_call(
        functools.partial(_dsa_kernel, n_keep=n_keep, n_heads=lay.qw // HEAD_DIM),
        grid=(bsz, seq // A_QBLOCK),
        in_specs=[qb(lay.qw, lay.a_q), qb(lay.iqw, lay.a_iq), qb(LANES, lay.a_iw),
                  kb(HEAD_DIM, lay.a_k), kb(HEAD_DIM, lay.a_v), kb(A_IDX_DIM, lay.a_ik)],
        out_specs=pl.BlockSpec((1, A_QBLOCK, lay.qw), lambda b, i: (b, i, 0)),
        out_shape=jax.ShapeDtypeStruct((bsz, seq, lay.qw), BF16),
        compiler_params=_params("parallel", "arbitrary"),
        name="dsa_mixer",
    )(ra, ra, pd, ra, pc, ra)


NSA_QBLOCK = 256
GELU_C = math.sqrt(2.0 / math.pi)


def _gelu_tanh(x):
    return 0.5 * x * (1.0 + jnp.tanh(GELU_C * (x + 0.044715 * (x * x * x))))


def _nsa_compress_kernel(kc_ref, vc_ref, pe_ref, w1_ref, w2_ref, ko_ref, vo_ref):
    n_chunk = ko_ref.shape[2]
    half = B_CMP_LEN // 2
    assert B_CMP_STRIDE == half
    row = lax.broadcasted_iota(jnp.int32, (n_chunk, 1), 0)
    for t, (src, dst) in enumerate(((kc_ref, ko_ref), (vc_ref, vo_ref))):
        lo = jnp.zeros((n_chunk, w1_ref.shape[-1]), F32)
        hi = jnp.zeros((n_chunk, w1_ref.shape[-1]), F32)
        for l in range(half):
            x = src[0, pl.ds(l, n_chunk, stride=B_CMP_STRIDE), :]
            lo = lo + jnp.dot((x + pe_ref[t, l:l + 1, :]).astype(BF16),
                              w1_ref[t, l * HEAD_DIM:(l + 1) * HEAD_DIM, :], preferred_element_type=F32)
            hi = hi + jnp.dot((x + pe_ref[t, half + l:half + l + 1, :]).astype(BF16),
                              w1_ref[t, (half + l) * HEAD_DIM:(half + l + 1) * HEAD_DIM, :],
                              preferred_element_type=F32)
        pre = lo + pltpu.roll(hi, n_chunk - 1, 0)
        out = jnp.dot(_gelu_tanh(pre).astype(BF16), w2_ref[t], preferred_element_type=F32)
        dst[0, 0] = jnp.where(row < n_chunk - 1, out, 0.0).astype(dst.dtype)


def _nsa_compress(rb, pd, lay, phi_pe, phi_w1, phi_w2):
    bsz, seq, _ = rb.shape
    n_chunk = seq // B_CMP_STRIDE
    cmap = lambda b, g: (0, 0, 0)
    omap = lambda b, g: (b, g, 0, 0)
    vc_blk = lay.b_vc // HEAD_DIM
    out_sds = jax.ShapeDtypeStruct((bsz, B_KV_GROUPS, n_chunk, HEAD_DIM), BF16)
    return pl.pallas_call(
        _nsa_compress_kernel,
        grid=(bsz, B_KV_GROUPS),
        in_specs=[pl.BlockSpec((1, seq, HEAD_DIM), lambda b, g: (b, 0, g)),
                  pl.BlockSpec((1, seq, HEAD_DIM), lambda b, g: (b, 0, vc_blk + g)),
                  pl.BlockSpec(phi_pe.shape, cmap),
                  pl.BlockSpec(phi_w1.shape, cmap),
                  pl.BlockSpec(phi_w2.shape, cmap)],
        out_specs=[pl.BlockSpec((1, 1, n_chunk, HEAD_DIM), omap),
                   pl.BlockSpec((1, 1, n_chunk, HEAD_DIM), omap)],
        out_shape=[out_sds, out_sds],
        compiler_params=_params("parallel", "parallel"),
        name="nsa_compress",
    )(rb, pd, phi_pe, phi_w1.astype(BF16), phi_w2.astype(BF16))


def _nsa_kernel(q_ref, gate_ref, kcmp_ref, vcmp_ref, ks_ref, vs_ref, kw_ref, vw_ref, ovl_ref, expand_ref,
                o_ref, *, n_sel, heads_per_group):
    seq = ks_ref.shape[1]
    n_cmp = kcmp_ref.shape[2]
    n_blk = seq // B_SEL_LEN
    scale = HEAD_DIM ** -0.5
    start = pl.program_id(2) * NSA_QBLOCK
    qpos = start + lax.broadcasted_iota(jnp.int32, (NSA_QBLOCK, 1), 0)
    heads = [q_ref[0, :, r * HEAD_DIM:(r + 1) * HEAD_DIM] for r in range(heads_per_group)]

    cend = lax.broadcasted_iota(jnp.int32, (1, n_cmp), 1) * B_CMP_STRIDE + (B_CMP_LEN - 1)
    cmask = _mask_bias(cend <= qpos)
    kcmp = kcmp_ref[0, 0]
    vcmp = vcmp_ref[0, 0]
    o_cmp = []
    psum = jnp.zeros((NSA_QBLOCK, n_cmp), F32)
    for q in heads:
        e, denom = _softmax_weights(_dot_nt(q, kcmp) * scale, cmask)
        p = e / denom
        psum = psum + p
        o_cmp.append(jnp.dot(p.astype(BF16), vcmp, preferred_element_type=F32))
    p_hi = psum.astype(BF16)
    p_lo = (psum - p_hi.astype(F32)).astype(BF16)
    ovl = ovl_ref[...]
    imp = (jnp.dot(p_hi, ovl, preferred_element_type=F32) + jnp.dot(p_lo, ovl, preferred_element_type=F32))
    blk = lax.broadcasted_iota(jnp.int32, (1, imp.shape[1]), 1)
    cur = qpos // B_SEL_LEN
    imp = jnp.where((blk == 0) | (blk == cur) | (blk == cur - 1), jnp.inf,
                    jnp.where(blk > cur, -jnp.inf, imp))
    blk_sel = _topk_mask_by_rank(_order_key(imp), blk, n_sel, n_blk)

    n_win = B_WIN + NSA_QBLOCK
    kstart = pl.multiple_of(jnp.maximum(start - B_WIN, 0), NSA_QBLOCK)
    wpos = kstart + lax.broadcasted_iota(jnp.int32, (1, n_win), 1)
    win_bias = _mask_bias((wpos <= qpos) & (wpos > qpos - B_WIN)).astype(BF16)
    kw = kw_ref[0, pl.ds(kstart, n_win), :]
    vw = _with_ones(vw_ref[0, pl.ds(kstart, n_win), :])
    o_win = [_attend(q, kw, vw, win_bias, scale) for q in heads]

    gates = jax.nn.sigmoid(gate_ref[0])
    sel_rows = jnp.where(blk_sel, 1.0, 0.0).astype(BF16)

    def selected_and_combine(klen):
        kpos = lax.broadcasted_iota(jnp.int32, (1, klen), 1)
        in_sel = jnp.dot(sel_rows, expand_ref[:, :klen], preferred_element_type=F32)
        sel_bias = _mask_bias((in_sel > 0.5) & (kpos <= qpos)).astype(BF16)
        ks = ks_ref[0, :klen, :]
        vs = _with_ones(vs_ref[0, :klen, :])
        for r, q in enumerate(heads):
            gr = gates[:, 3 * r:3 * r + 3]
            out = (gr[:, 0:1] * o_cmp[r] + gr[:, 1:2] * _attend(q, ks, vs, sel_bias, scale)
                   + gr[:, 2:3] * o_win[r])
            o_ref[0, :, r * HEAD_DIM:(r + 1) * HEAD_DIM] = out.astype(o_ref.dtype)

    _for_causal_extent(start + NSA_QBLOCK, seq, selected_and_combine)


def _nsa_mixer(ra, rb, pc, pd, lay, phi_pe, phi_w1, phi_w2):
    bsz, seq, _ = ra.shape
    hpg = lay.n_bheads // B_KV_GROUPS
    gq = hpg * HEAD_DIM
    assert seq % KEY_TILE == 0 and seq >= B_WIN + NSA_QBLOCK and KEY_TILE % NSA_QBLOCK == 0
    kcmp, vcmp = _nsa_compress(rb, pd, lay, phi_pe, phi_w1, phi_w2)
    n_cmp = seq // B_CMP_STRIDE
    n_blk = seq // B_SEL_LEN
    n_sel = min(B_SEL_N, n_blk)
    assert n_blk <= LANES
    c_lo = np.arange(n_cmp)[:, None] * B_CMP_STRIDE
    b_lo = np.arange(LANES)[None, :] * B_SEL_LEN
    overlap = ((c_lo < b_lo + B_SEL_LEN) & (c_lo + B_CMP_LEN - 1 >= b_lo) & (np.arange(LANES)[None, :] < n_blk))
    expand = (np.arange(seq)[None, :] // B_SEL_LEN) == np.arange(LANES)[:, None]
    qb = lambda width, off: pl.BlockSpec((1, NSA_QBLOCK, width), lambda b, g, i: (b, i, off // width + g))
    kb = lambda off: pl.BlockSpec((1, seq, HEAD_DIM), lambda b, g, i: (b, 0, off // HEAD_DIM + g))
    cmap = lambda b, g, i: (b, g, 0, 0)
    const = lambda b, g, i: (0, 0)
    return pl.pallas_call(
        functools.partial(_nsa_kernel, n_sel=n_sel, heads_per_group=hpg),
        grid=(bsz, B_KV_GROUPS, seq // NSA_QBLOCK),
        in_specs=[qb(gq, lay.b_q), qb(LANES, lay.b_g),
                  pl.BlockSpec((1, 1, n_cmp, HEAD_DIM), cmap),
                  pl.BlockSpec((1, 1, n_cmp, HEAD_DIM), cmap),
                  kb(lay.b_ks), kb(lay.b_vs), kb(lay.b_kw), kb(lay.b_vw),
                  pl.BlockSpec((n_cmp, LANES), const),
                  pl.BlockSpec((LANES, seq), const)],
        out_specs=pl.BlockSpec((1, NSA_QBLOCK, gq), lambda b, g, i: (b, i, g)),
        out_shape=jax.ShapeDtypeStruct((bsz, seq, lay.qw), BF16),
        compiler_params=_params("parallel", "parallel", "arbitrary"),
        name="nsa_mixer",
    )(ra, pd, kcmp, vcmp, ra, pc, ra, pc, jnp.asarray(overlap, BF16), jnp.asarray(expand, BF16))


S5_TIME_BLOCK = 256
S5_UNROLL = 8


def _s5_scan_kernel(u_ref, bcat_ref, ccat_ref, a_ref, d_ref, z_ref, xs_ref, state_ref, bt_ref, tb_ref):
    bsz, tt, lanes = u_ref.shape
    ns = a_ref.shape[-1] // 2

    @pl.when(pl.program_id(1) == 0)
    def _():
        state_ref[...] = jnp.zeros_like(state_ref)

    bt_ref[...] = u_ref[...].reshape(bsz * tt, lanes)

    def to_time_major(t, carry):
        tb_ref[pl.ds(pl.multiple_of(t * bsz, bsz), bsz), :] = bt_ref[pl.ds(t, bsz, stride=tt), :]
        return carry

    lax.fori_loop(0, tt, to_time_major, 0, unroll=S5_UNROLL)
    u = tb_ref[...]
    xs_ref[...] = jnp.dot(u.astype(BF16), bcat_ref[0], preferred_element_type=F32)
    a_re = jnp.broadcast_to(a_ref[0, :, :ns], (bsz, ns))
    a_im = jnp.broadcast_to(a_ref[0, :, ns:], (bsz, ns))

    def step(t, carry):
        s_re, s_im = carry
        rows = pl.ds(pl.multiple_of(t * bsz, bsz), bsz)
        n_re = a_re * s_re - a_im * s_im + xs_ref[rows, :ns]
        n_im = a_re * s_im + a_im * s_re + xs_ref[rows, ns:]
        xs_ref[rows, :ns] = n_re
        xs_ref[rows, ns:] = n_im
        return n_re, n_im

    s_re, s_im = lax.fori_loop(0, tt, step, (state_ref[:, :ns], state_ref[:, ns:]), unroll=S5_UNROLL)
    state_ref[:, :ns] = s_re
    state_ref[:, ns:] = s_im
    y = jnp.dot(xs_ref[...].astype(BF16), ccat_ref[0], preferred_element_type=F32) + d_ref[0] * u
    tb_ref[...] = _gelu_tanh(y)
    for b in range(bsz):
        z_ref[b] = tb_ref[pl.ds(b, tt, stride=bsz), :]


def _s5_glu_kernel(za_ref, zt_ref, w_ref, b_ref, o_ref):
    zz = jnp.dot(za_ref[...].astype(BF16), w_ref[...], preferred_element_type=F32) + b_ref[...]
    o_ref[...] = (zt_ref[...] * jax.nn.sigmoid(zz)).astype(o_ref.dtype)


def _s5_mixer(pd, u_off, width, lam_re, lam_im, log_dt, b_re, b_im, c_re, c_im, d, glu_w, glu_b):
    bsz, seq, _ = pd.shape
    assert u_off % LANES == 0
    gpc = LANES // C_GROUP
    n_col = width // LANES
    ns = gpc * C_STATE
    tt = min(S5_TIME_BLOCK, seq)
    assert width % LANES == 0 and seq % tt == 0 and bsz % 8 == 0
    dt = jnp.exp(log_dt)[:, None]
    mag = jnp.exp(lam_re * dt)
    a_re, a_im = mag * jnp.cos(lam_im * dt), mag * jnp.sin(lam_im * dt)
    den = lam_re * lam_re + lam_im * lam_im
    co_re = ((a_re - 1.0) * lam_re + a_im * lam_im) / den
    co_im = (a_im * lam_re - (a_re - 1.0) * lam_im) / den
    bb_re = co_re[..., None] * b_re - co_im[..., None] * b_im
    bb_im = co_re[..., None] * b_im + co_im[..., None] * b_re
    eye = jnp.eye(gpc, dtype=F32)

    def block_diag(m):
        r, c = m.shape[-2:]
        return jnp.einsum('ngrc,gk->ngrkc', m, eye).reshape(n_col, gpc * r, gpc * c)

    to_in = lambda m: block_diag(m.reshape(n_col, gpc, C_STATE, C_GROUP).transpose(0, 1, 3, 2))
    to_out = lambda m: block_diag(m.reshape(n_col, gpc, C_GROUP, C_STATE).transpose(0, 1, 3, 2))
    bcat = jnp.concatenate([to_in(bb_re), to_in(bb_im)], axis=-1).astype(BF16)
    ccat = jnp.concatenate([to_out(c_re), to_out(-c_im)], axis=-2).astype(BF16)
    acat = jnp.concatenate([a_re.reshape(n_col, 1, ns), a_im.reshape(n_col, 1, ns)], axis=-1)
    z = pl.pallas_call(
        _s5_scan_kernel,
        grid=(n_col, seq // tt),
        in_specs=[pl.BlockSpec((bsz, tt, LANES), lambda c, t: (0, t, u_off // LANES + c)),
                  pl.BlockSpec((1, LANES, 2 * ns), lambda c, t: (c, 0, 0)),
                  pl.BlockSpec((1, 2 * ns, LANES), lambda c, t: (c, 0, 0)),
                  pl.BlockSpec((1, 1, 2 * ns), lambda c, t: (c, 0, 0)),
                  pl.BlockSpec((1, 1, LANES), lambda c, t: (c, 0, 0))],
        out_specs=pl.BlockSpec((bsz, tt, LANES), lambda c, t: (0, t, c)),
        out_shape=jax.ShapeDtypeStruct((bsz, seq, width), F32),
        scratch_shapes=[pltpu.VMEM((tt * bsz, 2 * ns), F32), pltpu.VMEM((bsz, 2 * ns), F32),
                        pltpu.VMEM((bsz * tt, LANES), F32), pltpu.VMEM((tt * bsz, LANES), F32)],
        compiler_params=_params("parallel", "arbitrary"),
        name="s5_scan",
    )(pd, bcat, ccat, acat, d.reshape(n_col, 1, LANES))
    z = z.reshape(bsz * seq, width)
    tm, tn = min(ROW_TILE, seq * bsz), min(512, width)
    return pl.pallas_call(
        _s5_glu_kernel,
        grid=(seq * bsz // tm, width // tn),
        in_specs=[pl.BlockSpec((tm, width), lambda i, j: (i, 0)),
                  pl.BlockSpec((tm, tn), lambda i, j: (i, j)),
                  pl.BlockSpec((width, tn), lambda i, j: (0, j)),
                  pl.BlockSpec((1, tn), lambda i, j: (0, j))],
        out_specs=pl.BlockSpec((tm, tn), lambda i, j: (i, j)),
        out_shape=jax.ShapeDtypeStruct((seq * bsz, width), BF16),
        compiler_params=_params("parallel", "parallel"),
        name="s5_glu",
    )(z, z, glu_w.astype(BF16), glu_b.reshape(1, width))


def _mixing_sublayer(hb, bsz, seq, rope, lay, w_in, phi_pe, phi_w1, phi_w2, lam_re, lam_im, log_dt,
                     b_re, b_im, c_re, c_im, s5_d, glu_w, glu_b, w_out):
    w_ra, w_rb, w_pc, w_pd = lay.weights(w_in)
    view = lambda t: t.reshape(bsz, seq, t.shape[-1])
    ra = view(_matmul(hb, w_ra, lay.ra_tile, BF16, rope))
    rb = view(_matmul(hb, w_rb, w_rb.shape[1], F32, rope))
    pc = view(_matmul(hb, w_pc, w_pc.shape[1], BF16))
    pd = view(_matmul(hb, w_pd, lay.pd_tile, F32))
    o_a = _dsa_mixer(ra, pc, pd, lay)
    o_b = _nsa_mixer(ra, rb, pc, pd, lay, phi_pe, phi_w1, phi_w2)
    o_c = _s5_mixer(pd, lay.c_u, lay.cw, lam_re, lam_im, log_dt, b_re, b_im, c_re, c_im, s5_d, glu_w, glu_b)
    return _out_proj(o_a.reshape(bsz * seq, -1), o_b.reshape(bsz * seq, -1), o_c, w_out.astype(BF16))


def _split_bf16(x):
    hi = x.astype(BF16)
    return hi, (x - hi.astype(F32)).astype(BF16)


def _route_rows(x, wh_ref, wl_ref, bias_ref, eid_ref, gate_ref):
    x_hi, x_lo = _split_bf16(x)
    wh = wh_ref[...]
    logits = (jnp.dot(x_hi, wh, preferred_element_type=F32) + jnp.dot(x_hi, wl_ref[...], preferred_element_type=F32)
              + jnp.dot(x_lo, wh, preferred_element_type=F32))
    aff = jax.nn.sigmoid(logits)
    lane_i = lax.broadcasted_iota(jnp.int32, (1, LANES), 1)
    lane = lane_i.astype(F32)
    grp = (lane_i // EXPERTS_PER_GROUP).astype(F32)
    sel = jnp.where(lane_i < N_EXPERTS, aff + bias_ref[...], -jnp.inf)

    def top2(v):
        m1 = jnp.max(v, axis=-1, keepdims=True)
        i1 = jnp.min(jnp.where(v == m1, lane, float(LANES)), axis=-1, keepdims=True)
        v2 = jnp.where(lane == i1, -jnp.inf, v)
        m2 = jnp.max(v2, axis=-1, keepdims=True)
        i2 = jnp.min(jnp.where(v2 == m2, lane, float(LANES)), axis=-1, keepdims=True)
        return m1, i1, m2, i2

    best_score = best_grp = None
    for g in range(N_EXPERT_GROUPS):
        m1, _, m2, _ = top2(jnp.where(grp == float(g), sel, -jnp.inf))
        score = m1 + m2
        if g == 0:
            best_score, best_grp = score, jnp.zeros_like(score)
        else:
            better = score > best_score
            best_score = jnp.where(better, score, best_score)
            best_grp = jnp.where(better, float(g), best_grp)
    _, i1, _, i2 = top2(jnp.where(grp == best_grp, sel, -jnp.inf))
    w1 = jnp.sum(jnp.where(lane == i1, aff, 0.0), axis=-1, keepdims=True)
    w2 = jnp.sum(jnp.where(lane == i2, aff, 0.0), axis=-1, keepdims=True)
    tot = w1 + w2
    eid_ref[...] = jnp.where(lane_i == 0, i1, jnp.where(lane_i == 1, i2, 0.0)).astype(jnp.int32)
    gate_ref[...] = jnp.where(lane_i == 0, w1 / tot, jnp.where(lane_i == 1, w2 / tot, 0.0))


def _moe(xt, eid, layer, w_gate, w_up, w_down):
    n_tok, dm = xt.shape
    n_asg = n_tok * TOP_K
    cap = -(-n_asg // MOE_BLOCK) * MOE_BLOCK + N_EXPERTS * MOE_BLOCK
    onehot = (eid.reshape(n_asg, 1) == jnp.arange(N_EXPERTS, dtype=jnp.int32)[None, :]).astype(jnp.int32)
    seen = jnp.cumsum(onehot, axis=0)
    counts = seen[-1]
    padded = (counts + MOE_BLOCK - 1) // MOE_BLOCK * MOE_BLOCK
    pend = jnp.cumsum(padded)
    pstart = pend - padded
    dest = jnp.sum(onehot * (seen - 1 + pstart[None, :]), axis=1).astype(jnp.int32)
    tok_flat = jnp.arange(n_asg, dtype=jnp.int32) // TOP_K
    slot_tok = (jnp.arange(cap, dtype=jnp.int32) % n_tok).at[dest].set(tok_flat)
    n_blocks = cap // MOE_BLOCK
    blk_start = jnp.arange(n_blocks) * MOE_BLOCK
    blk_exp = jnp.minimum(jnp.sum(pend[None, :] <= blk_start[:, None], axis=1), N_EXPERTS - 1).astype(jnp.int32)
    blk_first = jnp.concatenate([jnp.ones((1,), jnp.int32), (blk_exp[1:] != blk_exp[:-1]).astype(jnp.int32)])
    n_act = (pend[-1] // MOE_BLOCK).astype(jnp.int32).reshape(1)
    y = _moe_experts(xt[slot_tok], blk_exp, blk_first, n_act, layer, w_gate, w_up, w_down)
    pos = dest.reshape(n_tok, TOP_K)
    return y[pos[:, 0]], y[pos[:, 1]]


def kernel(x, positions, ln_in_g, ln_in_b, w_in, nsa_phi_pe, nsa_phi_w1, nsa_phi_w2, s5_lam_re, s5_lam_im, s5_log_dt, s5_b_re, s5_b_im, s5_c_re, s5_c_im, s5_d, s5_glu_w, s5_glu_b, w_out, ln_mix_g, ln_mix_b, router_w, router_bias, moe_w_gate, moe_w_up, moe_w_down, ln_ffn_g, ln_ffn_b):
    depth = w_in.shape[0]
    bsz, seq, d_model = x.shape
    alpha = (2 * depth) ** 0.25
    lay = _Packed(d_model)
    rope = _rope_tables(positions)
    h, hb = _layer_norm(x.reshape(bsz * seq, d_model), ln_in_g, ln_in_b)
    for l in range(depth):
        mix = _mixing_sublayer(hb, bsz, seq, rope, lay, w_in[l], nsa_phi_pe[l], nsa_phi_w1[l], nsa_phi_w2[l],
                               s5_lam_re[l], s5_lam_im[l], s5_log_dt[l], s5_b_re[l], s5_b_im[l],
                               s5_c_re[l], s5_c_im[l], s5_d[l], s5_glu_w[l], s5_glu_b[l], w_out[l])
        h, hb, eid, gate = _layer_norm_and_route(h, ln_mix_g[l], ln_mix_b[l], mix, alpha, router_w, router_bias)
        y0, y1 = _moe(h, eid, l, moe_w_gate, moe_w_up, moe_w_down)
        h, hb = _layer_norm(h, ln_ffn_g[l], ln_ffn_b[l], gated_pair=(y0, y1, gate), alpha=alpha)
    return h.reshape(bsz, seq, d_model)
```

```python
import functools
import math

import numpy as np
import jax
import jax.numpy as jnp
from jax import lax
from jax.experimental import pallas as pl
from jax.experimental.pallas import tpu as pltpu

F32 = jnp.float32
BF16 = jnp.bfloat16

LANES = 128
HEAD_DIM = 128
ROT_DIM = HEAD_DIM // 4
ROPE_THETA = 500000.0
A_IDX_HEADS = 4
A_IDX_DIM = 128
A_TOPK_MAX = 256
A_QBLOCK = 128
B_KV_GROUPS = 2
B_CMP_LEN = 32
B_CMP_STRIDE = 16
B_SEL_LEN = 64
B_SEL_N = 16
B_WIN = 512
C_GROUP = 16
C_STATE = 64
N_EXPERTS = 32
N_EXPERT_GROUPS = 4
EXPERTS_PER_GROUP = N_EXPERTS // N_EXPERT_GROUPS
TOP_K = 2
MOE_BLOCK = 256
LN_EPS = 1e-5

VMEM_LIMIT_BYTES = 56 * 1024 * 1024
ROW_TILE = 1024


def _params(*semantics):
    return pltpu.CompilerParams(dimension_semantics=semantics, vmem_limit_bytes=VMEM_LIMIT_BYTES)


def _mm_kernel(a_ref, b_ref, o_ref):
    o_ref[...] = jnp.dot(a_ref[...], b_ref[...], preferred_element_type=F32).astype(o_ref.dtype)


def _mm_rope_kernel(a_ref, b_ref, c_ref, sa_ref, sb_ref, o_ref):
    y = jnp.dot(a_ref[...], b_ref[...], preferred_element_type=F32)
    half = ROT_DIM // 2
    for hb in range(y.shape[1] // HEAD_DIM):
        cols = slice(hb * HEAD_DIM, (hb + 1) * HEAD_DIM)
        yh = y[:, cols]
        out = (yh * c_ref[...] + pltpu.roll(yh, half, 1) * sa_ref[...]
               + pltpu.roll(yh, HEAD_DIM - half, 1) * sb_ref[...])
        o_ref[:, cols] = out.astype(o_ref.dtype)


def _matmul(a, b, tn, out_dtype, rope=None):
    m, k = a.shape
    n = b.shape[1]
    tm = min(ROW_TILE, m)
    assert m % tm == 0 and n % tn == 0
    in_specs = [pl.BlockSpec((tm, k), lambda i, j: (i, 0)),
                pl.BlockSpec((k, tn), lambda i, j: (0, j))]
    args = (a, b)
    if rope is not None:
        in_specs += [pl.BlockSpec((tm, HEAD_DIM), lambda i, j: (i, 0))] * 3
        args += tuple(rope)
    return pl.pallas_call(
        _mm_kernel if rope is None else _mm_rope_kernel,
        grid=(m // tm, n // tn),
        in_specs=in_specs,
        out_specs=pl.BlockSpec((tm, tn), lambda i, j: (i, j)),
        out_shape=jax.ShapeDtypeStruct((m, n), out_dtype),
        compiler_params=_params("parallel", "parallel"),
        name="in_proj" if rope is None else "in_proj_rope",
    )(*args)


def _out_proj_kernel(a0_ref, a1_ref, a2_ref, w_ref, o_ref):
    k0 = a0_ref.shape[1]
    k1 = k0 + a1_ref.shape[1]
    acc = jnp.dot(a0_ref[...], w_ref[:k0, :], preferred_element_type=F32)
    acc = acc + jnp.dot(a1_ref[...], w_ref[k0:k1, :], preferred_element_type=F32)
    o_ref[...] = acc + jnp.dot(a2_ref[...], w_ref[k1:, :], preferred_element_type=F32)


def _out_proj(a0, a1, a2, w, tn=512):
    m = a0.shape[0]
    k, n = w.shape
    tm = min(ROW_TILE, m)
    assert a0.shape[1] + a1.shape[1] + a2.shape[1] == k and m % tm == 0 and n % tn == 0
    amap = lambda i, j: (i, 0)
    return pl.pallas_call(
        _out_proj_kernel,
        grid=(m // tm, n // tn),
        in_specs=[pl.BlockSpec((tm, a0.shape[1]), amap), pl.BlockSpec((tm, a1.shape[1]), amap),
                  pl.BlockSpec((tm, a2.shape[1]), amap), pl.BlockSpec((k, tn), lambda i, j: (0, j))],
        out_specs=pl.BlockSpec((tm, tn), lambda i, j: (i, j)),
        out_shape=jax.ShapeDtypeStruct((m, n), F32),
        compiler_params=_params("parallel", "parallel"),
        name="out_proj",
    )(a0, a1, a2, w)


LN_ROWS = 256


def _ln_store(v, g_ref, b_ref, o_ref, ob_ref):
    mu = jnp.mean(v, axis=-1, keepdims=True)
    c = v - mu
    var = jnp.mean(c * c, axis=-1, keepdims=True)
    out = c * lax.rsqrt(var + LN_EPS) * g_ref[...] + b_ref[...]
    o_ref[...] = out
    ob_ref[...] = out.astype(BF16)
    return out


def _ln_kernel(x_ref, g_ref, b_ref, o_ref, ob_ref):
    _ln_store(x_ref[...], g_ref, b_ref, o_ref, ob_ref)


def _ln_residual_route_kernel(x_ref, r_ref, g_ref, b_ref, wh_ref, wl_ref, rbias_ref, o_ref, ob_ref, eid_ref, gate_ref,
                              *, alpha):
    out = _ln_store(alpha * x_ref[...] + r_ref[...], g_ref, b_ref, o_ref, ob_ref)
    _route_rows(out, wh_ref, wl_ref, rbias_ref, eid_ref, gate_ref)


def _ln_gated_pair_kernel(x_ref, y0_ref, y1_ref, w_ref, g_ref, b_ref, o_ref, ob_ref, *, alpha):
    w = w_ref[...]
    ffn = w[:, 0:1] * y0_ref[...] + w[:, 1:2] * y1_ref[...]
    _ln_store(alpha * x_ref[...] + ffn, g_ref, b_ref, o_ref, ob_ref)


def _layer_norm_and_route(x, g, b, residual, alpha, router_w, router_bias):
    m, d = x.shape
    assert m % LN_ROWS == 0 and N_EXPERTS <= LANES
    pad = LANES - N_EXPERTS
    wh, wl = _split_bf16(jnp.pad(router_w, ((0, 0), (0, pad))))
    rbias = jnp.pad(router_bias, (0, pad)).reshape(1, LANES)
    rows = lambda w: pl.BlockSpec((LN_ROWS, w), lambda i: (i, 0))
    const = lambda shape: pl.BlockSpec(shape, lambda i: (0, 0))
    h, hb, eid, gate = pl.pallas_call(
        functools.partial(_ln_residual_route_kernel, alpha=alpha), grid=(m // LN_ROWS,),
        in_specs=[rows(d), rows(d), const((1, d)), const((1, d)), const((d, LANES)), const((d, LANES)),
                  const((1, LANES))],
        out_specs=[rows(d), rows(d), rows(LANES), rows(LANES)],
        out_shape=[jax.ShapeDtypeStruct((m, d), F32), jax.ShapeDtypeStruct((m, d), BF16),
                   jax.ShapeDtypeStruct((m, LANES), jnp.int32), jax.ShapeDtypeStruct((m, LANES), F32)],
        compiler_params=_params("parallel"), name="layer_norm_route",
    )(x, residual, g.reshape(1, d), b.reshape(1, d), wh, wl, rbias)
    return h, hb, eid[:, :TOP_K], gate[:, :TOP_K]


def _layer_norm(x, g, b, alpha=1.0, gated_pair=None):
    m, d = x.shape
    rows = pl.BlockSpec((LN_ROWS, d), lambda i: (i, 0))
    vec = pl.BlockSpec((1, d), lambda i: (0, 0))
    assert m % LN_ROWS == 0
    if gated_pair is not None:
        y0, y1, w = gated_pair
        body = functools.partial(_ln_gated_pair_kernel, alpha=alpha)
        ins = [rows, rows, rows, pl.BlockSpec((LN_ROWS, w.shape[1]), lambda i: (i, 0)), vec, vec]
        args = (x, y0, y1, w, g.reshape(1, d), b.reshape(1, d))
    else:
        body, ins, args = _ln_kernel, [rows, vec, vec], (x, g.reshape(1, d), b.reshape(1, d))
    return pl.pallas_call(
        body, grid=(m // LN_ROWS,), in_specs=ins, out_specs=[rows, rows],
        out_shape=[jax.ShapeDtypeStruct((m, d), F32), jax.ShapeDtypeStruct((m, d), BF16)],
        compiler_params=_params("parallel"), name="layer_norm",
    )(*args)


MOE_SPLIT = 2


def _moe_up_kernel(blk_exp_ref, first_ref, n_act_ref, x_ref, wg_ref, wu_ref, hid_ref, wres_ref, acc_ref):
    i, k = pl.program_id(0), pl.program_id(1)
    active = i < n_act_ref[0]

    de = hid_ref.shape[1]

    @pl.when(active & (first_ref[i] == 1))
    def _():
        wres_ref[k, :, :de] = wg_ref[0, 0].astype(BF16)
        wres_ref[k, :, de:] = wu_ref[0, 0].astype(BF16)

    @pl.when(active)
    def _():
        gu = jnp.dot(x_ref[...].astype(BF16), wres_ref[k], preferred_element_type=F32)

        @pl.when(k == 0)
        def _():
            acc_ref[...] = gu

        @pl.when(k > 0)
        def _():
            acc_ref[...] += gu

        @pl.when(k == MOE_SPLIT - 1)
        def _():
            g_all = acc_ref[:, :de]
            hid_ref[...] = (g_all * jax.nn.sigmoid(g_all) * acc_ref[:, de:]).astype(hid_ref.dtype)

    @pl.when(jnp.logical_not(active) & (k == MOE_SPLIT - 1))
    def _():
        hid_ref[...] = jnp.zeros_like(hid_ref)


def _moe_down_kernel(blk_exp_ref, first_ref, n_act_ref, hid_ref, wd_ref, y_ref, wres_ref):
    i, n = pl.program_id(0), pl.program_id(1)
    active = i < n_act_ref[0]

    @pl.when(active & (first_ref[i] == 1))
    def _():
        wres_ref[n] = wd_ref[0, 0].astype(BF16)

    @pl.when(active)
    def _():
        y_ref[...] = jnp.dot(hid_ref[...], wres_ref[n], preferred_element_type=F32)

    @pl.when(jnp.logical_not(active))
    def _():
        y_ref[...] = jnp.zeros_like(y_ref)


def _moe_experts(xs, blk_exp, blk_first, n_act, layer, wg, wu, wd):
    cap, dm = xs.shape
    de = wg.shape[-1]
    n_blocks = cap // MOE_BLOCK
    tk, tn = dm // MOE_SPLIT, dm // MOE_SPLIT
    tile = lambda j, first: jnp.where(first == 1, j, MOE_SPLIT - 1)
    hid = pl.pallas_call(
        _moe_up_kernel,
        grid_spec=pltpu.PrefetchScalarGridSpec(
            num_scalar_prefetch=3, grid=(n_blocks, MOE_SPLIT),
            in_specs=[pl.BlockSpec((MOE_BLOCK, tk), lambda i, k, be, bf, na: (i, k)),
                      pl.BlockSpec((1, 1, tk, de), lambda i, k, be, bf, na: (layer, be[i], tile(k, bf[i]), 0)),
                      pl.BlockSpec((1, 1, tk, de), lambda i, k, be, bf, na: (layer, be[i], tile(k, bf[i]), 0))],
            out_specs=pl.BlockSpec((MOE_BLOCK, de), lambda i, k, be, bf, na: (i, 0)),
            scratch_shapes=[pltpu.VMEM((MOE_SPLIT, tk, 2 * de), BF16), pltpu.VMEM((MOE_BLOCK, 2 * de), F32)]),
        out_shape=jax.ShapeDtypeStruct((cap, de), BF16),
        compiler_params=_params("arbitrary", "arbitrary"),
        name="moe_gate_up",
    )(blk_exp, blk_first, n_act, xs, wg, wu)
    return pl.pallas_call(
        _moe_down_kernel,
        grid_spec=pltpu.PrefetchScalarGridSpec(
            num_scalar_prefetch=3, grid=(n_blocks, MOE_SPLIT),
            in_specs=[pl.BlockSpec((MOE_BLOCK, de), lambda i, n, be, bf, na: (i, 0)),
                      pl.BlockSpec((1, 1, de, tn), lambda i, n, be, bf, na: (layer, be[i], 0, tile(n, bf[i])))],
            out_specs=pl.BlockSpec((MOE_BLOCK, tn), lambda i, n, be, bf, na: (i, n)),
            scratch_shapes=[pltpu.VMEM((MOE_SPLIT, de, tn), BF16)]),
        out_shape=jax.ShapeDtypeStruct((cap, dm), F32),
        compiler_params=_params("arbitrary", "arbitrary"),
        name="moe_down",
    )(blk_exp, blk_first, n_act, hid, wd)


INT32_MIN = -2 ** 31
BISECT_UNROLL = 8


def _order_key(x):
    b = lax.bitcast_convert_type(x + 0.0, jnp.int32)
    return b ^ ((b >> 31) & 0x7FFFFFFF)


def _count(mask):
    return jnp.sum(jnp.where(mask, 1.0, 0.0), axis=-1, keepdims=True)


def _topk_mask(key, pos, k, n_pos_bits):
    kf = float(k)
    thr = jnp.where(_count(key >= 0) >= kf, 0, INT32_MIN).astype(jnp.int32)

    def value_step(i, thr):
        cand = thr + jnp.left_shift(jnp.int32(1), 30 - i)
        return jnp.where(_count(key >= cand) >= kf, cand, thr)

    thr = lax.fori_loop(0, 31, value_step, thr, unroll=BISECT_UNROLL)
    above = key > thr
    n_pos = 1 << n_pos_bits
    tied_pos = jnp.where(key == thr, pos, n_pos)
    need = kf - _count(above)

    def take_lowest(_):
        def pos_step(i, last):
            cand = last + jnp.left_shift(jnp.int32(1), n_pos_bits - 1 - i)
            return jnp.where(_count(tied_pos < cand) < need, cand, last)

        return lax.fori_loop(0, n_pos_bits, pos_step, jnp.zeros_like(thr), unroll=BISECT_UNROLL)

    def take_all(_):
        return jnp.full_like(thr, n_pos - 1)

    surplus = jnp.max(_count(tied_pos < n_pos) - need)
    last = lax.cond(surplus > 0.0, take_lowest, take_all, None)
    return above | (tied_pos <= last)


def _topk_mask_by_rank(key, pos, k, n):
    rank = jnp.zeros(key.shape, F32)
    for m in range(n):
        rank = rank + jnp.where(key[:, m:m + 1] > key - jnp.where(pos > m, 1, 0), 1.0, 0.0)
    return (rank < float(k)) & (pos < n)


def _mask_bias(mask):
    return jnp.where(mask, 0.0, -jnp.inf)


def _softmax_weights(s, bias):
    s = s + bias
    m = jnp.max(s, axis=-1, keepdims=True)
    m = jnp.where(jnp.isfinite(m), m, 0.0)
    e = jnp.exp(s - m)
    return e, jnp.maximum(jnp.sum(e, axis=-1, keepdims=True), 1e-30)


def _dot_nt(a, b):
    return lax.dot_general(a, b, (((1,), (1,)), ((), ())), preferred_element_type=F32)


def _attend(q, k, v, bias, scale):
    s = (_dot_nt(q, k) * scale).astype(BF16) + bias
    m = jnp.max(s, axis=-1, keepdims=True).astype(F32)
    m = jnp.where(jnp.isfinite(m), m, 0.0).astype(BF16)
    e = jnp.exp(s - m)
    both = jnp.dot(e, v, preferred_element_type=F32)
    dv = v.shape[1] // 2
    return both[:, :dv] / jnp.maximum(both[:, dv:], 1e-30)


def _with_ones(v):
    return jnp.concatenate([v, jnp.ones(v.shape, v.dtype)], axis=1)


KEY_TILE = 512


def _for_causal_extent(q_end, seq, body):
    for klen in range(KEY_TILE, seq + 1, KEY_TILE):
        pl.when((q_end > klen - KEY_TILE) & (q_end <= klen))(functools.partial(body, klen))


class _Packed:
    def __init__(self, d_model):
        self.qw = d_model // 4
        self.gw = B_KV_GROUPS * HEAD_DIM
        self.iqw = A_IDX_HEADS * A_IDX_DIM
        self.cw = d_model // 2
        self.n_bheads = self.qw // HEAD_DIM
        self.gate_cols = 3 * self.n_bheads // B_KV_GROUPS
        qw, gw = self.qw, self.gw
        self.in_sizes = (qw, HEAD_DIM, HEAD_DIM, self.iqw, A_IDX_DIM, A_IDX_HEADS, qw,
                         gw, gw, gw, gw, gw, gw, 3 * self.n_bheads, self.cw)
        self.ra_tile = 512
        o = 0
        self.a_q = o; o += qw
        self.b_q = o; o += qw
        self.a_iq = o; o += self.iqw
        self.b_ks = o; o += gw
        self.b_kw = o; o += gw
        self.a_k = o; o += HEAD_DIM
        self.a_ik = o; o += A_IDX_DIM
        self.ra_width = -(-o // self.ra_tile) * self.ra_tile
        self.b_vs, self.b_vw, self.a_v = 0, gw, 2 * gw
        self.pc_width = 2 * gw + HEAD_DIM
        self.c_u = 0
        self.b_vc = self.cw
        self.a_iw = self.b_vc + gw
        self.b_g = self.a_iw + LANES
        self.pd_width = self.b_g + B_KV_GROUPS * LANES
        self.pd_tile = next(t for t in (896, 768, 640, 512, 384, 256, 128) if self.pd_width % t == 0)
        assert qw % 512 == 0 and self.iqw % 512 == 0 and self.a_iq % 512 == 0

    def weights(self, w_in):
        cuts = np.cumsum(self.in_sizes)[:-1].tolist()
        (a_q, a_k, a_v, a_iq, a_ik, a_iw, b_q, b_kc, b_vc, b_ks, b_vs, b_kw, b_vw, b_g,
         c_u) = jnp.split(w_in.astype(BF16), cuts, axis=1)
        zeros = lambda n: jnp.zeros((w_in.shape[0], n), BF16)
        ra = [a_q, b_q, a_iq, b_ks, b_kw, a_k, a_ik]
        ra.append(zeros(self.ra_width - sum(t.shape[1] for t in ra)))
        pd = [c_u, b_vc, a_iw, zeros(LANES - a_iw.shape[1])]
        for g in range(B_KV_GROUPS):
            pd += [b_g[:, g * self.gate_cols:(g + 1) * self.gate_cols], zeros(LANES - self.gate_cols)]
        cat = lambda ts: jnp.concatenate(ts, axis=1)
        return cat(ra), b_kc, cat([b_vs, b_vw, a_v]), cat(pd)


def _rope_tables(positions):
    half = ROT_DIM // 2
    inv = ROPE_THETA ** (-jnp.arange(0, ROT_DIM, 2, dtype=F32) / ROT_DIM)
    ang = positions.astype(F32).reshape(-1, 1) * inv
    cos, sin = jnp.cos(ang), jnp.sin(ang)
    n = ang.shape[0]
    ones, zeros, z16 = jnp.ones((n, HEAD_DIM - ROT_DIM), F32), jnp.zeros((n, HEAD_DIM - ROT_DIM), F32), jnp.zeros((n, half), F32)
    return (jnp.concatenate([cos, cos, ones], axis=1),
            jnp.concatenate([z16, sin, zeros], axis=1),
            jnp.concatenate([-sin, z16, zeros], axis=1))


def _dsa_kernel(q_ref, iq_ref, iw_ref, k_ref, v_ref, ik_ref, o_ref, *, n_keep, n_heads):
    seq = k_ref.shape[1]
    start = pl.program_id(1) * A_QBLOCK
    qpos = start + lax.broadcasted_iota(jnp.int32, (A_QBLOCK, 1), 0)

    def block(klen):
        kpos = lax.broadcasted_iota(jnp.int32, (1, klen), 1)
        causal = kpos <= qpos
        ik = ik_ref[0, :klen, :]
        iw = iw_ref[0] * (A_IDX_HEADS * A_IDX_DIM) ** -0.5
        score = jnp.zeros((A_QBLOCK, klen), F32)
        for h in range(A_IDX_HEADS):
            rel = _dot_nt(iq_ref[0, :, h * A_IDX_DIM:(h + 1) * A_IDX_DIM], ik)
            score = score + iw[:, h:h + 1] * jnp.maximum(rel, 0.0)
        score = jnp.where(causal, score, -jnp.inf)
        sel = _topk_mask(_order_key(score), kpos, n_keep, max(1, (klen - 1).bit_length())) & causal
        bias = _mask_bias(sel).astype(BF16)
        k = k_ref[0, :klen, :]
        v = _with_ones(v_ref[0, :klen, :])
        for h in range(n_heads):
            hs = slice(h * HEAD_DIM, (h + 1) * HEAD_DIM)
            o_ref[0, :, hs] = _attend(q_ref[0, :, hs], k, v, bias, HEAD_DIM ** -0.5).astype(o_ref.dtype)

    _for_causal_extent(start + A_QBLOCK, seq, block)


def _dsa_mixer(ra, pc, pd, lay):
    bsz, seq, _ = ra.shape
    n_keep = min(A_TOPK_MAX, seq // 4)
    assert seq % KEY_TILE == 0 and KEY_TILE >= n_keep
    qb = lambda width, off: pl.BlockSpec((1, A_QBLOCK, width), lambda b, i: (b, i, off // width))
    kb = lambda width, off: pl.BlockSpec((1, seq, width), lambda b, i: (b, 0, off // width))
    return pl.pallas_call(
        functools.partial(_dsa_kernel, n_keep=n_keep, n_heads=lay.qw // HEAD_DIM),
        grid=(bsz, seq // A_QBLOCK),
        in_specs=[qb(lay.qw, lay.a_q), qb(lay.iqw, lay.a_iq), qb(LANES, lay.a_iw),
                  kb(HEAD_DIM, lay.a_k), kb(HEAD_DIM, lay.a_v), kb(A_IDX_DIM, lay.a_ik)],
        out_specs=pl.BlockSpec((1, A_QBLOCK, lay.qw), lambda b, i: (b, i, 0)),
        out_shape=jax.ShapeDtypeStruct((bsz, seq, lay.qw), BF16),
        compiler_params=_params("parallel", "arbitrary"),
        name="dsa_mixer",
    )(ra, ra, pd, ra, pc, ra)


NSA_QBLOCK = 256
GELU_C = math.sqrt(2.0 / math.pi)


def _gelu_tanh(x):
    return 0.5 * x * (1.0 + jnp.tanh(GELU_C * (x + 0.044715 * (x * x * x))))


def _nsa_compress_kernel(kc_ref, vc_ref, pe_ref, w1_ref, w2_ref, ko_ref, vo_ref):
    n_chunk = ko_ref.shape[2]
    half = B_CMP_LEN // 2
    assert B_CMP_STRIDE == half
    row = lax.broadcasted_iota(jnp.int32, (n_chunk, 1), 0)
    for t, (src, dst) in enumerate(((kc_ref, ko_ref), (vc_ref, vo_ref))):
        lo = jnp.zeros((n_chunk, w1_ref.shape[-1]), F32)
        hi = jnp.zeros((n_chunk, w1_ref.shape[-1]), F32)
        for l in range(half):
            x = src[0, pl.ds(l, n_chunk, stride=B_CMP_STRIDE), :]
            lo = lo + jnp.dot((x + pe_ref[t, l:l + 1, :]).astype(BF16),
                              w1_ref[t, l * HEAD_DIM:(l + 1) * HEAD_DIM, :], preferred_element_type=F32)
            hi = hi + jnp.dot((x + pe_ref[t, half + l:half + l + 1, :]).astype(BF16),
                              w1_ref[t, (half + l) * HEAD_DIM:(half + l + 1) * HEAD_DIM, :],
                              preferred_element_type=F32)
        pre = lo + pltpu.roll(hi, n_chunk - 1, 0)
        out = jnp.dot(_gelu_tanh(pre).astype(BF16), w2_ref[t], preferred_element_type=F32)
        dst[0, 0] = jnp.where(row < n_chunk - 1, out, 0.0).astype(dst.dtype)


def _nsa_compress(rb, pd, lay, phi_pe, phi_w1, phi_w2):
    bsz, seq, _ = rb.shape
    n_chunk = seq // B_CMP_STRIDE
    cmap = lambda b, g: (0, 0, 0)
    omap = lambda b, g: (b, g, 0, 0)
    vc_blk = lay.b_vc // HEAD_DIM
    out_sds = jax.ShapeDtypeStruct((bsz, B_KV_GROUPS, n_chunk, HEAD_DIM), BF16)
    return pl.pallas_call(
        _nsa_compress_kernel,
        grid=(bsz, B_KV_GROUPS),
        in_specs=[pl.BlockSpec((1, seq, HEAD_DIM), lambda b, g: (b, 0, g)),
                  pl.BlockSpec((1, seq, HEAD_DIM), lambda b, g: (b, 0, vc_blk + g)),
                  pl.BlockSpec(phi_pe.shape, cmap),
                  pl.BlockSpec(phi_w1.shape, cmap),
                  pl.BlockSpec(phi_w2.shape, cmap)],
        out_specs=[pl.BlockSpec((1, 1, n_chunk, HEAD_DIM), omap),
                   pl.BlockSpec((1, 1, n_chunk, HEAD_DIM), omap)],
        out_shape=[out_sds, out_sds],
        compiler_params=_params("parallel", "parallel"),
        name="nsa_compress",
    )(rb, pd, phi_pe, phi_w1.astype(BF16), phi_w2.astype(BF16))


def _nsa_kernel(q_ref, gate_ref, kcmp_ref, vcmp_ref, ks_ref, vs_ref, kw_ref, vw_ref, ovl_ref, expand_ref,
                o_ref, *, n_sel, heads_per_group):
    seq = ks_ref.shape[1]
    n_cmp = kcmp_ref.shape[2]
    n_blk = seq // B_SEL_LEN
    scale = HEAD_DIM ** -0.5
    start = pl.program_id(2) * NSA_QBLOCK
    qpos = start + lax.broadcasted_iota(jnp.int32, (NSA_QBLOCK, 1), 0)
    heads = [q_ref[0, :, r * HEAD_DIM:(r + 1) * HEAD_DIM] for r in range(heads_per_group)]

    cend = lax.broadcasted_iota(jnp.int32, (1, n_cmp), 1) * B_CMP_STRIDE + (B_CMP_LEN - 1)
    cmask = _mask_bias(cend <= qpos)
    kcmp = kcmp_ref[0, 0]
    vcmp = vcmp_ref[0, 0]
    o_cmp = []
    psum = jnp.zeros((NSA_QBLOCK, n_cmp), F32)
    for q in heads:
        e, denom = _softmax_weights(_dot_nt(q, kcmp) * scale, cmask)
        p = e / denom
        psum = psum + p
        o_cmp.append(jnp.dot(p.astype(BF16), vcmp, preferred_element_type=F32))
    p_hi = psum.astype(BF16)
    p_lo = (psum - p_hi.astype(F32)).astype(BF16)
    ovl = ovl_ref[...]
    imp = (jnp.dot(p_hi, ovl, preferred_element_type=F32) + jnp.dot(p_lo, ovl, preferred_element_type=F32))
    blk = lax.broadcasted_iota(jnp.int32, (1, imp.shape[1]), 1)
    cur = qpos // B_SEL_LEN
    imp = jnp.where((blk == 0) | (blk == cur) | (blk == cur - 1), jnp.inf,
                    jnp.where(blk > cur, -jnp.inf, imp))
    blk_sel = _topk_mask_by_rank(_order_key(imp), blk, n_sel, n_blk)

    n_win = B_WIN + NSA_QBLOCK
    kstart = pl.multiple_of(jnp.maximum(start - B_WIN, 0), NSA_QBLOCK)
    wpos = kstart + lax.broadcasted_iota(jnp.int32, (1, n_win), 1)
    win_bias = _mask_bias((wpos <= qpos) & (wpos > qpos - B_WIN)).astype(BF16)
    kw = kw_ref[0, pl.ds(kstart, n_win), :]
    vw = _with_ones(vw_ref[0, pl.ds(kstart, n_win), :])
    o_win = [_attend(q, kw, vw, win_bias, scale) for q in heads]

    gates = jax.nn.sigmoid(gate_ref[0])
    sel_rows = jnp.where(blk_sel, 1.0, 0.0).astype(BF16)

    def selected_and_combine(klen):
        kpos = lax.broadcasted_iota(jnp.int32, (1, klen), 1)
        in_sel = jnp.dot(sel_rows, expand_ref[:, :klen], preferred_element_type=F32)
        sel_bias = _mask_bias((in_sel > 0.5) & (kpos <= qpos)).astype(BF16)
        ks = ks_ref[0, :klen, :]
        vs = _with_ones(vs_ref[0, :klen, :])
        for r, q in enumerate(heads):
            gr = gates[:, 3 * r:3 * r + 3]
            out = (gr[:, 0:1] * o_cmp[r] + gr[:, 1:2] * _attend(q, ks, vs, sel_bias, scale)
                   + gr[:, 2:3] * o_win[r])
            o_ref[0, :, r * HEAD_DIM:(r + 1) * HEAD_DIM] = out.astype(o_ref.dtype)

    _for_causal_extent(start + NSA_QBLOCK, seq, selected_and_combine)


def _nsa_mixer(ra, rb, pc, pd, lay, phi_pe, phi_w1, phi_w2):
    bsz, seq, _ = ra.shape
    hpg = lay.n_bheads // B_KV_GROUPS
    gq = hpg * HEAD_DIM
    assert seq % KEY_TILE == 0 and seq >= B_WIN + NSA_QBLOCK and KEY_TILE % NSA_QBLOCK == 0
    kcmp, vcmp = _nsa_compress(rb, pd, lay, phi_pe, phi_w1, phi_w2)
    n_cmp = seq // B_CMP_STRIDE
    n_blk = seq // B_SEL_LEN
    n_sel = min(B_SEL_N, n_blk)
    assert n_blk <= LANES
    c_lo = np.arange(n_cmp)[:, None] * B_CMP_STRIDE
    b_lo = np.arange(LANES)[None, :] * B_SEL_LEN
    overlap = ((c_lo < b_lo + B_SEL_LEN) & (c_lo + B_CMP_LEN - 1 >= b_lo) & (np.arange(LANES)[None, :] < n_blk))
    expand = (np.arange(seq)[None, :] // B_SEL_LEN) == np.arange(LANES)[:, None]
    qb = lambda width, off: pl.BlockSpec((1, NSA_QBLOCK, width), lambda b, g, i: (b, i, off // width + g))
    kb = lambda off: pl.BlockSpec((1, seq, HEAD_DIM), lambda b, g, i: (b, 0, off // HEAD_DIM + g))
    cmap = lambda b, g, i: (b, g, 0, 0)
    const = lambda b, g, i: (0, 0)
    return pl.pallas_call(
        functools.partial(_nsa_kernel, n_sel=n_sel, heads_per_group=hpg),
        grid=(bsz, B_KV_GROUPS, seq // NSA_QBLOCK),
        in_specs=[qb(gq, lay.b_q), qb(LANES, lay.b_g),
                  pl.BlockSpec((1, 1, n_cmp, HEAD_DIM), cmap),
                  pl.BlockSpec((1, 1, n_cmp, HEAD_DIM), cmap),
                  kb(lay.b_ks), kb(lay.b_vs), kb(lay.b_kw), kb(lay.b_vw),
                  pl.BlockSpec((n_cmp, LANES), const),
                  pl.BlockSpec((LANES, seq), const)],
        out_specs=pl.BlockSpec((1, NSA_QBLOCK, gq), lambda b, g, i: (b, i, g)),
        out_shape=jax.ShapeDtypeStruct((bsz, seq, lay.qw), BF16),
        compiler_params=_params("parallel", "parallel", "arbitrary"),
        name="nsa_mixer",
    )(ra, pd, kcmp, vcmp, ra, pc, ra, pc, jnp.asarray(overlap, BF16), jnp.asarray(expand, BF16))


S5_TIME_BLOCK = 512
S5_UNROLL = 8


def _s5_scan_kernel(u_ref, bcat_ref, ccat_ref, a_ref, d_ref, z_ref, xs_ref, state_ref, bt_ref, tb_ref):
    bsz, tt, lanes = u_ref.shape
    ns = a_ref.shape[-1] // 2

    @pl.when(pl.program_id(1) == 0)
    def _():
        state_ref[...] = jnp.zeros_like(state_ref)

    bt_ref[...] = u_ref[...].reshape(bsz * tt, lanes)

    def to_time_major(t, carry):
        tb_ref[pl.ds(pl.multiple_of(t * bsz, bsz), bsz), :] = bt_ref[pl.ds(t, bsz, stride=tt), :]
        return carry

    lax.fori_loop(0, tt, to_time_major, 0, unroll=S5_UNROLL)
    u = tb_ref[...]
    xs_ref[...] = jnp.dot(u.astype(BF16), bcat_ref[0], preferred_element_type=F32)
    a_re = jnp.broadcast_to(a_ref[0, :, :ns], (bsz, ns))
    a_im = jnp.broadcast_to(a_ref[0, :, ns:], (bsz, ns))

    def step(t, carry):
        s_re, s_im = carry
        rows = pl.ds(pl.multiple_of(t * bsz, bsz), bsz)
        n_re = a_re * s_re - a_im * s_im + xs_ref[rows, :ns]
        n_im = a_re * s_im + a_im * s_re + xs_ref[rows, ns:]
        xs_ref[rows, :ns] = n_re
        xs_ref[rows, ns:] = n_im
        return n_re, n_im

    s_re, s_im = lax.fori_loop(0, tt, step, (state_ref[:, :ns], state_ref[:, ns:]), unroll=S5_UNROLL)
    state_ref[:, :ns] = s_re
    state_ref[:, ns:] = s_im
    y = jnp.dot(xs_ref[...].astype(BF16), ccat_ref[0], preferred_element_type=F32) + d_ref[0] * u
    tb_ref[...] = _gelu_tanh(y)
    for b in range(bsz):
        z_ref[b] = tb_ref[pl.ds(b, tt, stride=bsz), :]


def _s5_glu_kernel(za_ref, zt_ref, w_ref, b_ref, o_ref):
    zz = jnp.dot(za_ref[...].astype(BF16), w_ref[...], preferred_element_type=F32) + b_ref[...]
    o_ref[...] = (zt_ref[...] * jax.nn.sigmoid(zz)).astype(o_ref.dtype)


def _s5_mixer(pd, u_off, width, lam_re, lam_im, log_dt, b_re, b_im, c_re, c_im, d, glu_w, glu_b):
    bsz, seq, _ = pd.shape
    assert u_off % LANES == 0
    gpc = LANES // C_GROUP
    n_col = width // LANES
    ns = gpc * C_STATE
    tt = min(S5_TIME_BLOCK, seq)
    assert width % LANES == 0 and seq % tt == 0 and bsz % 8 == 0
    dt = jnp.exp(log_dt)[:, None]
    mag = jnp.exp(lam_re * dt)
    a_re, a_im = mag * jnp.cos(lam_im * dt), mag * jnp.sin(lam_im * dt)
    den = lam_re * lam_re + lam_im * lam_im
    co_re = ((a_re - 1.0) * lam_re + a_im * lam_im) / den
    co_im = (a_im * lam_re - (a_re - 1.0) * lam_im) / den
    bb_re = co_re[..., None] * b_re - co_im[..., None] * b_im
    bb_im = co_re[..., None] * b_im + co_im[..., None] * b_re
    eye = jnp.eye(gpc, dtype=F32)

    def block_diag(m):
        r, c = m.shape[-2:]
        return jnp.einsum('ngrc,gk->ngrkc', m, eye).reshape(n_col, gpc * r, gpc * c)

    to_in = lambda m: block_diag(m.reshape(n_col, gpc, C_STATE, C_GROUP).transpose(0, 1, 3, 2))
    to_out = lambda m: block_diag(m.reshape(n_col, gpc, C_GROUP, C_STATE).transpose(0, 1, 3, 2))
    bcat = jnp.concatenate([to_in(bb_re), to_in(bb_im)], axis=-1).astype(BF16)
    ccat = jnp.concatenate([to_out(c_re), to_out(-c_im)], axis=-2).astype(BF16)
    acat = jnp.concatenate([a_re.reshape(n_col, 1, ns), a_im.reshape(n_col, 1, ns)], axis=-1)
    z = pl.pallas_call(
        _s5_scan_kernel,
        grid=(n_col, seq // tt),
        in_specs=[pl.BlockSpec((bsz, tt, LANES), lambda c, t: (0, t, u_off // LANES + c)),
                  pl.BlockSpec((1, LANES, 2 * ns), lambda c, t: (c, 0, 0)),
                  pl.BlockSpec((1, 2 * ns, LANES), lambda c, t: (c, 0, 0)),
                  pl.BlockSpec((1, 1, 2 * ns), lambda c, t: (c, 0, 0)),
                  pl.BlockSpec((1, 1, LANES), lambda c, t: (c, 0, 0))],
        out_specs=pl.BlockSpec((bsz, tt, LANES), lambda c, t: (0, t, c)),
        out_shape=jax.ShapeDtypeStruct((bsz, seq, width), F32),
        scratch_shapes=[pltpu.VMEM((tt * bsz, 2 * ns), F32), pltpu.VMEM((bsz, 2 * ns), F32),
                        pltpu.VMEM((bsz * tt, LANES), F32), pltpu.VMEM((tt * bsz, LANES), F32)],
        compiler_params=_params("parallel", "arbitrary"),
        name="s5_scan",
    )(pd, bcat, ccat, acat, d.reshape(n_col, 1, LANES))
    z = z.reshape(bsz * seq, width)
    tm, tn = min(ROW_TILE, seq * bsz), min(512, width)
    return pl.pallas_call(
        _s5_glu_kernel,
        grid=(seq * bsz // tm, width // tn),
        in_specs=[pl.BlockSpec((tm, width), lambda i, j: (i, 0)),
                  pl.BlockSpec((tm, tn), lambda i, j: (i, j)),
                  pl.BlockSpec((width, tn), lambda i, j: (0, j)),
                  pl.BlockSpec((1, tn), lambda i, j: (0, j))],
        out_specs=pl.BlockSpec((tm, tn), lambda i, j: (i, j)),
        out_shape=jax.ShapeDtypeStruct((seq * bsz, width), BF16),
        compiler_params=_params("parallel", "parallel"),
        name="s5_glu",
    )(z, z, glu_w.astype(BF16), glu_b.reshape(1, width))


def _mixing_sublayer(hb, bsz, seq, rope, lay, w_in, phi_pe, phi_w1, phi_w2, lam_re, lam_im, log_dt,
                     b_re, b_im, c_re, c_im, s5_d, glu_w, glu_b, w_out):
    w_ra, w_rb, w_pc, w_pd = lay.weights(w_in)
    view = lambda t: t.reshape(bsz, seq, t.shape[-1])
    ra = view(_matmul(hb, w_ra, lay.ra_tile, BF16, rope))
    rb = view(_matmul(hb, w_rb, w_rb.shape[1], F32, rope))
    pc = view(_matmul(hb, w_pc, w_pc.shape[1], BF16))
    pd = view(_matmul(hb, w_pd, lay.pd_tile, F32))
    o_a = _dsa_mixer(ra, pc, pd, lay)
    o_b = _nsa_mixer(ra, rb, pc, pd, lay, phi_pe, phi_w1, phi_w2)
    o_c = _s5_mixer(pd, lay.c_u, lay.cw, lam_re, lam_im, log_dt, b_re, b_im, c_re, c_im, s5_d, glu_w, glu_b)
    return _out_proj(o_a.reshape(bsz * seq, -1), o_b.reshape(bsz * seq, -1), o_c, w_out.astype(BF16))


def _split_bf16(x):
    hi = x.astype(BF16)
    return hi, (x - hi.astype(F32)).astype(BF16)


def _route_rows(x, wh_ref, wl_ref, bias_ref, eid_ref, gate_ref):
    x_hi, x_lo = _split_bf16(x)
    wh = wh_ref[...]
    logits = (jnp.dot(x_hi, wh, preferred_element_type=F32) + jnp.dot(x_hi, wl_ref[...], preferred_element_type=F32)
              + jnp.dot(x_lo, wh, preferred_element_type=F32))
    aff = jax.nn.sigmoid(logits)
    lane_i = lax.broadcasted_iota(jnp.int32, (1, LANES), 1)
    lane = lane_i.astype(F32)
    grp = (lane_i // EXPERTS_PER_GROUP).astype(F32)
    sel = jnp.where(lane_i < N_EXPERTS, aff + bias_ref[...], -jnp.inf)

    def top2(v):
        m1 = jnp.max(v, axis=-1, keepdims=True)
        i1 = jnp.min(jnp.where(v == m1, lane, float(LANES)), axis=-1, keepdims=True)
        v2 = jnp.where(lane == i1, -jnp.inf, v)
        m2 = jnp.max(v2, axis=-1, keepdims=True)
        i2 = jnp.min(jnp.where(v2 == m2, lane, float(LANES)), axis=-1, keepdims=True)
        return m1, i1, m2, i2

    best_score = best_grp = None
    for g in range(N_EXPERT_GROUPS):
        m1, _, m2, _ = top2(jnp.where(grp == float(g), sel, -jnp.inf))
        score = m1 + m2
        if g == 0:
            best_score, best_grp = score, jnp.zeros_like(score)
        else:
            better = score > best_score
            best_score = jnp.where(better, score, best_score)
            best_grp = jnp.where(better, float(g), best_grp)
    _, i1, _, i2 = top2(jnp.where(grp == best_grp, sel, -jnp.inf))
    w1 = jnp.sum(jnp.where(lane == i1, aff, 0.0), axis=-1, keepdims=True)
    w2 = jnp.sum(jnp.where(lane == i2, aff, 0.0), axis=-1, keepdims=True)
    tot = w1 + w2
    eid_ref[...] = jnp.where(lane_i == 0, i1, jnp.where(lane_i == 1, i2, 0.0)).astype(jnp.int32)
    gate_ref[...] = jnp.where(lane_i == 0, w1 / tot, jnp.where(lane_i == 1, w2 / tot, 0.0))


def _moe(xt, eid, layer, w_gate, w_up, w_down):
    n_tok, dm = xt.shape
    n_asg = n_tok * TOP_K
    cap = -(-n_asg // MOE_BLOCK) * MOE_BLOCK + N_EXPERTS * MOE_BLOCK
    onehot = (eid.reshape(n_asg, 1) == jnp.arange(N_EXPERTS, dtype=jnp.int32)[None, :]).astype(jnp.int32)
    seen = jnp.cumsum(onehot, axis=0)
    counts = seen[-1]
    padded = (counts + MOE_BLOCK - 1) // MOE_BLOCK * MOE_BLOCK
    pend = jnp.cumsum(padded)
    pstart = pend - padded
    dest = jnp.sum(onehot * (seen - 1 + pstart[None, :]), axis=1).astype(jnp.int32)
    tok_flat = jnp.arange(n_asg, dtype=jnp.int32) // TOP_K
    slot_tok = (jnp.arange(cap, dtype=jnp.int32) % n_tok).at[dest].set(tok_flat)
    n_blocks = cap // MOE_BLOCK
    blk_start = jnp.arange(n_blocks) * MOE_BLOCK
    blk_exp = jnp.minimum(jnp.sum(pend[None, :] <= blk_start[:, None], axis=1), N_EXPERTS - 1).astype(jnp.int32)
    blk_first = jnp.concatenate([jnp.ones((1,), jnp.int32), (blk_exp[1:] != blk_exp[:-1]).astype(jnp.int32)])
    n_act = (pend[-1] // MOE_BLOCK).astype(jnp.int32).reshape(1)
    y = _moe_experts(xt[slot_tok], blk_exp, blk_first, n_act, layer, w_gate, w_up, w_down)
    pos = dest.reshape(n_tok, TOP_K)
    return y[pos[:, 0]], y[pos[:, 1]]


def kernel(x, positions, ln_in_g, ln_in_b, w_in, nsa_phi_pe, nsa_phi_w1, nsa_phi_w2, s5_lam_re, s5_lam_im, s5_log_dt, s5_b_re, s5_b_im, s5_c_re, s5_c_im, s5_d, s5_glu_w, s5_glu_b, w_out, ln_mix_g, ln_mix_b, router_w, router_bias, moe_w_gate, moe_w_up, moe_w_down, ln_ffn_g, ln_ffn_b):
    depth = w_in.shape[0]
    bsz, seq, d_model = x.shape
    alpha = (2 * depth) ** 0.25
    lay = _Packed(d_model)
    rope = _rope_tables(positions)
    h, hb = _layer_norm(x.reshape(bsz * seq, d_model), ln_in_g, ln_in_b)
    for l in range(depth):
        mix = _mixing_sublayer(hb, bsz, seq, rope, lay, w_in[l], nsa_phi_pe[l], nsa_phi_w1[l], nsa_phi_w2[l],
                               s5_lam_re[l], s5_lam_im[l], s5_log_dt[l], s5_b_re[l], s5_b_im[l],
                               s5_c_re[l], s5_c_im[l], s5_d[l], s5_glu_w[l], s5_glu_b[l], w_out[l])
        h, hb, eid, gate = _layer_norm_and_route(h, ln_mix_g[l], ln_mix_b[l], mix, alpha, router_w, router_bias)
        y0, y1 = _moe(h, eid, l, moe_w_gate, moe_w_up, moe_w_down)
        h, hb = _layer_norm(h, ln_ffn_g[l], ln_ffn_b[l], gated_pair=(y0, y1, gate), alpha=alpha)
    return h.reshape(bsz, seq, d_model)
```

```python
import functools
import math

import numpy as np
import jax
import jax.numpy as jnp
from jax import lax
from jax.experimental import pallas as pl
from jax.experimental.pallas import tpu as pltpu

F32 = jnp.float32
BF16 = jnp.bfloat16

LANES = 128
HEAD_DIM = 128
ROT_DIM = HEAD_DIM // 4
ROPE_THETA = 500000.0
A_IDX_HEADS = 4
A_IDX_DIM = 128
A_TOPK_MAX = 256
A_QBLOCK = 128
B_KV_GROUPS = 2
B_CMP_LEN = 32
B_CMP_STRIDE = 16
B_SEL_LEN = 64
B_SEL_N = 16
B_WIN = 512
C_GROUP = 16
C_STATE = 64
N_EXPERTS = 32
N_EXPERT_GROUPS = 4
EXPERTS_PER_GROUP = N_EXPERTS // N_EXPERT_GROUPS
TOP_K = 2
MOE_BLOCK = 256
LN_EPS = 1e-5

VMEM_LIMIT_BYTES = 56 * 1024 * 1024
ROW_TILE = 1024


def _params(*semantics):
    return pltpu.CompilerParams(dimension_semantics=semantics, vmem_limit_bytes=VMEM_LIMIT_BYTES)


def _mm_kernel(a_ref, b_ref, o_ref):
    o_ref[...] = jnp.dot(a_ref[...], b_ref[...], preferred_element_type=F32).astype(o_ref.dtype)


def _mm_rope_kernel(a_ref, b_ref, c_ref, sa_ref, sb_ref, o_ref):
    y = jnp.dot(a_ref[...], b_ref[...], preferred_element_type=F32)
    half = ROT_DIM // 2
    for hb in range(y.shape[1] // HEAD_DIM):
        cols = slice(hb * HEAD_DIM, (hb + 1) * HEAD_DIM)
        yh = y[:, cols]
        out = (yh * c_ref[...] + pltpu.roll(yh, half, 1) * sa_ref[...]
               + pltpu.roll(yh, HEAD_DIM - half, 1) * sb_ref[...])
        o_ref[:, cols] = out.astype(o_ref.dtype)


def _matmul(a, b, tn, out_dtype, rope=None):
    m, k = a.shape
    n = b.shape[1]
    tm = min(ROW_TILE, m)
    assert m % tm == 0 and n % tn == 0
    in_specs = [pl.BlockSpec((tm, k), lambda i, j: (i, 0)),
                pl.BlockSpec((k, tn), lambda i, j: (0, j))]
    args = (a, b)
    if rope is not None:
        in_specs += [pl.BlockSpec((tm, HEAD_DIM), lambda i, j: (i, 0))] * 3
        args += tuple(rope)
    return pl.pallas_call(
        _mm_kernel if rope is None else _mm_rope_kernel,
        grid=(m // tm, n // tn),
        in_specs=in_specs,
        out_specs=pl.BlockSpec((tm, tn), lambda i, j: (i, j)),
        out_shape=jax.ShapeDtypeStruct((m, n), out_dtype),
        compiler_params=_params("parallel", "parallel"),
        name="in_proj" if rope is None else "in_proj_rope",
    )(*args)


def _out_proj_kernel(a0_ref, a1_ref, a2_ref, w_ref, o_ref):
    k0 = a0_ref.shape[1]
    k1 = k0 + a1_ref.shape[1]
    acc = jnp.dot(a0_ref[...], w_ref[:k0, :], preferred_element_type=F32)
    acc = acc + jnp.dot(a1_ref[...], w_ref[k0:k1, :], preferred_element_type=F32)
    o_ref[...] = acc + jnp.dot(a2_ref[...], w_ref[k1:, :], preferred_element_type=F32)


def _out_proj(a0, a1, a2, w, tn=512):
    m = a0.shape[0]
    k, n = w.shape
    tm = min(ROW_TILE, m)
    assert a0.shape[1] + a1.shape[1] + a2.shape[1] == k and m % tm == 0 and n % tn == 0
    amap = lambda i, j: (i, 0)
    return pl.pallas_call(
        _out_proj_kernel,
        grid=(m // tm, n // tn),
        in_specs=[pl.BlockSpec((tm, a0.shape[1]), amap), pl.BlockSpec((tm, a1.shape[1]), amap),
                  pl.BlockSpec((tm, a2.shape[1]), amap), pl.BlockSpec((k, tn), lambda i, j: (0, j))],
        out_specs=pl.BlockSpec((tm, tn), lambda i, j: (i, j)),
        out_shape=jax.ShapeDtypeStruct((m, n), F32),
        compiler_params=_params("parallel", "parallel"),
        name="out_proj",
    )(a0, a1, a2, w)


LN_ROWS = 256


def _ln_store(v, g_ref, b_ref, o_ref, ob_ref):
    mu = jnp.mean(v, axis=-1, keepdims=True)
    c = v - mu
    var = jnp.mean(c * c, axis=-1, keepdims=True)
    out = c * lax.rsqrt(var + LN_EPS) * g_ref[...] + b_ref[...]
    o_ref[...] = out
    ob_ref[...] = out.astype(BF16)
    return out


def _ln_kernel(x_ref, g_ref, b_ref, o_ref, ob_ref):
    _ln_store(x_ref[...], g_ref, b_ref, o_ref, ob_ref)


def _ln_residual_route_kernel(x_ref, r_ref, g_ref, b_ref, wh_ref, wl_ref, rbias_ref, o_ref, ob_ref, eid_ref, gate_ref,
                              *, alpha):
    out = _ln_store(alpha * x_ref[...] + r_ref[...], g_ref, b_ref, o_ref, ob_ref)
    _route_rows(out, wh_ref, wl_ref, rbias_ref, eid_ref, gate_ref)


def _ln_gated_pair_kernel(x_ref, y0_ref, y1_ref, w_ref, g_ref, b_ref, o_ref, ob_ref, *, alpha):
    w = w_ref[...]
    ffn = w[:, 0:1] * y0_ref[...] + w[:, 1:2] * y1_ref[...]
    _ln_store(alpha * x_ref[...] + ffn, g_ref, b_ref, o_ref, ob_ref)


def _layer_norm_and_route(x, g, b, residual, alpha, router_w, router_bias):
    m, d = x.shape
    assert m % LN_ROWS == 0 and N_EXPERTS <= LANES
    pad = LANES - N_EXPERTS
    wh, wl = _split_bf16(jnp.pad(router_w, ((0, 0), (0, pad))))
    rbias = jnp.pad(router_bias, (0, pad)).reshape(1, LANES)
    rows = lambda w: pl.BlockSpec((LN_ROWS, w), lambda i: (i, 0))
    const = lambda shape: pl.BlockSpec(shape, lambda i: (0, 0))
    h, hb, eid, gate = pl.pallas_call(
        functools.partial(_ln_residual_route_kernel, alpha=alpha), grid=(m // LN_ROWS,),
        in_specs=[rows(d), rows(d), const((1, d)), const((1, d)), const((d, LANES)), const((d, LANES)),
                  const((1, LANES))],
        out_specs=[rows(d), rows(d), rows(LANES), rows(LANES)],
        out_shape=[jax.ShapeDtypeStruct((m, d), F32), jax.ShapeDtypeStruct((m, d), BF16),
                   jax.ShapeDtypeStruct((m, LANES), jnp.int32), jax.ShapeDtypeStruct((m, LANES), F32)],
        compiler_params=_params("parallel"), name="layer_norm_route",
    )(x, residual, g.reshape(1, d), b.reshape(1, d), wh, wl, rbias)
    return h, hb, eid[:, :TOP_K], gate[:, :TOP_K]


def _layer_norm(x, g, b, alpha=1.0, gated_pair=None):
    m, d = x.shape
    rows = pl.BlockSpec((LN_ROWS, d), lambda i: (i, 0))
    vec = pl.BlockSpec((1, d), lambda i: (0, 0))
    assert m % LN_ROWS == 0
    if gated_pair is not None:
        y0, y1, w = gated_pair
        body = functools.partial(_ln_gated_pair_kernel, alpha=alpha)
        ins = [rows, rows, rows, pl.BlockSpec((LN_ROWS, w.shape[1]), lambda i: (i, 0)), vec, vec]
        args = (x, y0, y1, w, g.reshape(1, d), b.reshape(1, d))
    else:
        body, ins, args = _ln_kernel, [rows, vec, vec], (x, g.reshape(1, d), b.reshape(1, d))
    return pl.pallas_call(
        body, grid=(m // LN_ROWS,), in_specs=ins, out_specs=[rows, rows],
        out_shape=[jax.ShapeDtypeStruct((m, d), F32), jax.ShapeDtypeStruct((m, d), BF16)],
        compiler_params=_params("parallel"), name="layer_norm",
    )(*args)


MOE_SPLIT = 2


def _moe_up_kernel(blk_exp_ref, first_ref, n_act_ref, x_ref, wg_ref, wu_ref, hid_ref, wres_ref, acc_ref):
    i, k = pl.program_id(0), pl.program_id(1)
    active = i < n_act_ref[0]

    de = hid_ref.shape[1]

    @pl.when(active & (first_ref[i] == 1))
    def _():
        wres_ref[k, :, :de] = wg_ref[0, 0].astype(BF16)
        wres_ref[k, :, de:] = wu_ref[0, 0].astype(BF16)

    @pl.when(active)
    def _():
        gu = jnp.dot(x_ref[...].astype(BF16), wres_ref[k], preferred_element_type=F32)

        @pl.when(k == 0)
        def _():
            acc_ref[...] = gu

        @pl.when(k > 0)
        def _():
            acc_ref[...] += gu

        @pl.when(k == MOE_SPLIT - 1)
        def _():
            g_all = acc_ref[:, :de]
            hid_ref[...] = (g_all * jax.nn.sigmoid(g_all) * acc_ref[:, de:]).astype(hid_ref.dtype)

    @pl.when(jnp.logical_not(active) & (k == MOE_SPLIT - 1))
    def _():
        hid_ref[...] = jnp.zeros_like(hid_ref)


def _moe_down_kernel(blk_exp_ref, first_ref, n_act_ref, hid_ref, wd_ref, y_ref, wres_ref):
    i, n = pl.program_id(0), pl.program_id(1)
    active = i < n_act_ref[0]

    @pl.when(active & (first_ref[i] == 1))
    def _():
        wres_ref[n] = wd_ref[0, 0].astype(BF16)

    @pl.when(active)
    def _():
        y_ref[...] = jnp.dot(hid_ref[...], wres_ref[n], preferred_element_type=F32)

    @pl.when(jnp.logical_not(active))
    def _():
        y_ref[...] = jnp.zeros_like(y_ref)


def _moe_experts(xs, blk_exp, blk_first, n_act, layer, wg, wu, wd):
    cap, dm = xs.shape
    de = wg.shape[-1]
    n_blocks = cap // MOE_BLOCK
    tk, tn = dm // MOE_SPLIT, dm // MOE_SPLIT
    tile = lambda j, first: jnp.where(first == 1, j, MOE_SPLIT - 1)
    hid = pl.pallas_call(
        _moe_up_kernel,
        grid_spec=pltpu.PrefetchScalarGridSpec(
            num_scalar_prefetch=3, grid=(n_blocks, MOE_SPLIT),
            in_specs=[pl.BlockSpec((MOE_BLOCK, tk), lambda i, k, be, bf, na: (i, k)),
                      pl.BlockSpec((1, 1, tk, de), lambda i, k, be, bf, na: (layer, be[i], tile(k, bf[i]), 0)),
                      pl.BlockSpec((1, 1, tk, de), lambda i, k, be, bf, na: (layer, be[i], tile(k, bf[i]), 0))],
            out_specs=pl.BlockSpec((MOE_BLOCK, de), lambda i, k, be, bf, na: (i, 0)),
            scratch_shapes=[pltpu.VMEM((MOE_SPLIT, tk, 2 * de), BF16), pltpu.VMEM((MOE_BLOCK, 2 * de), F32)]),
        out_shape=jax.ShapeDtypeStruct((cap, de), BF16),
        compiler_params=_params("arbitrary", "arbitrary"),
        name="moe_gate_up",
    )(blk_exp, blk_first, n_act, xs, wg, wu)
    return pl.pallas_call(
        _moe_down_kernel,
        grid_spec=pltpu.PrefetchScalarGridSpec(
            num_scalar_prefetch=3, grid=(n_blocks, MOE_SPLIT),
            in_specs=[pl.BlockSpec((MOE_BLOCK, de), lambda i, n, be, bf, na: (i, 0)),
                      pl.BlockSpec((1, 1, de, tn), lambda i, n, be, bf, na: (layer, be[i], 0, tile(n, bf[i])))],
            out_specs=pl.BlockSpec((MOE_BLOCK, tn), lambda i, n, be, bf, na: (i, n)),
            scratch_shapes=[pltpu.VMEM((MOE_SPLIT, de, tn), BF16)]),
        out_shape=jax.ShapeDtypeStruct((cap, dm), F32),
        compiler_params=_params("arbitrary", "arbitrary"),
        name="moe_down",
    )(blk_exp, blk_first, n_act, hid, wd)


INT32_MIN = -2 ** 31
BISECT_UNROLL = 8


def _order_key(x):
    b = lax.bitcast_convert_type(x + 0.0, jnp.int32)
    return b ^ ((b >> 31) & 0x7FFFFFFF)


def _count(mask):
    return jnp.sum(jnp.where(mask, 1.0, 0.0), axis=-1, keepdims=True)


def _topk_mask(key, pos, k, n_pos_bits):
    kf = float(k)
    thr = jnp.where(_count(key >= 0) >= kf, 0, INT32_MIN).astype(jnp.int32)

    def value_step(i, thr):
        cand = thr + jnp.left_shift(jnp.int32(1), 30 - i)
        return jnp.where(_count(key >= cand) >= kf, cand, thr)

    thr = lax.fori_loop(0, 31, value_step, thr, unroll=BISECT_UNROLL)
    above = key > thr
    n_pos = 1 << n_pos_bits
    tied_pos = jnp.where(key == thr, pos, n_pos)
    need = kf - _count(above)

    def take_lowest(_):
        def pos_step(i, last):
            cand = last + jnp.left_shift(jnp.int32(1), n_pos_bits - 1 - i)
            return jnp.where(_count(tied_pos < cand) < need, cand, last)

        return lax.fori_loop(0, n_pos_bits, pos_step, jnp.zeros_like(thr), unroll=BISECT_UNROLL)

    def take_all(_):
        return jnp.full_like(thr, n_pos - 1)

    surplus = jnp.max(_count(tied_pos < n_pos) - need)
    last = lax.cond(surplus > 0.0, take_lowest, take_all, None)
    return above | (tied_pos <= last)


def _topk_mask_by_rank(key, pos, k, n):
    rank = jnp.zeros(key.shape, F32)
    for m in range(n):
        rank = rank + jnp.where(key[:, m:m + 1] > key - jnp.where(pos > m, 1, 0), 1.0, 0.0)
    return (rank < float(k)) & (pos < n)


def _mask_bias(mask):
    return jnp.where(mask, 0.0, -jnp.inf)


def _softmax_weights(s, bias):
    s = s + bias
    m = jnp.max(s, axis=-1, keepdims=True)
    m = jnp.where(jnp.isfinite(m), m, 0.0)
    e = jnp.exp(s - m)
    return e, jnp.maximum(jnp.sum(e, axis=-1, keepdims=True), 1e-30)


def _dot_nt(a, b):
    return lax.dot_general(a, b, (((1,), (1,)), ((), ())), preferred_element_type=F32)


def _attend(q, k, v, bias, scale):
    s = (_dot_nt(q, k) * scale).astype(BF16) + bias
    m = jnp.max(s, axis=-1, keepdims=True).astype(F32)
    m = jnp.where(jnp.isfinite(m), m, 0.0).astype(BF16)
    e = jnp.exp(s - m)
    both = jnp.dot(e, v, preferred_element_type=F32)
    dv = v.shape[1] // 2
    return both[:, :dv] / jnp.maximum(both[:, dv:], 1e-30)


def _with_ones(v):
    return jnp.concatenate([v, jnp.ones(v.shape, v.dtype)], axis=1)


KEY_TILE = 512


def _for_causal_extent(q_end, seq, body, tile=KEY_TILE):
    for klen in range(tile, seq + 1, tile):
        pl.when((q_end > klen - tile) & (q_end <= klen))(functools.partial(body, klen))


class _Packed:
    def __init__(self, d_model):
        self.qw = d_model // 4
        self.gw = B_KV_GROUPS * HEAD_DIM
        self.iqw = A_IDX_HEADS * A_IDX_DIM
        self.cw = d_model // 2
        self.n_bheads = self.qw // HEAD_DIM
        self.gate_cols = 3 * self.n_bheads // B_KV_GROUPS
        qw, gw = self.qw, self.gw
        self.in_sizes = (qw, HEAD_DIM, HEAD_DIM, self.iqw, A_IDX_DIM, A_IDX_HEADS, qw,
                         gw, gw, gw, gw, gw, gw, 3 * self.n_bheads, self.cw)
        self.ra_tile = 512
        o = 0
        self.a_q = o; o += qw
        self.b_q = o; o += qw
        self.a_iq = o; o += self.iqw
        self.b_ks = o; o += gw
        self.b_kw = o; o += gw
        self.a_k = o; o += HEAD_DIM
        self.a_ik = o; o += A_IDX_DIM
        self.ra_width = -(-o // self.ra_tile) * self.ra_tile
        self.b_vs, self.b_vw, self.a_v = 0, gw, 2 * gw
        self.pc_width = 2 * gw + HEAD_DIM
        self.c_u = 0
        self.b_vc = self.cw
        self.a_iw = self.b_vc + gw
        self.b_g = self.a_iw + LANES
        self.pd_width = self.b_g + B_KV_GROUPS * LANES
        self.pd_tile = next(t for t in (896, 768, 640, 512, 384, 256, 128) if self.pd_width % t == 0)
        assert qw % 512 == 0 and self.iqw % 512 == 0 and self.a_iq % 512 == 0

    def weights(self, w_in):
        cuts = np.cumsum(self.in_sizes)[:-1].tolist()
        (a_q, a_k, a_v, a_iq, a_ik, a_iw, b_q, b_kc, b_vc, b_ks, b_vs, b_kw, b_vw, b_g,
         c_u) = jnp.split(w_in.astype(BF16), cuts, axis=1)
        zeros = lambda n: jnp.zeros((w_in.shape[0], n), BF16)
        ra = [a_q, b_q, a_iq, b_ks, b_kw, a_k, a_ik]
        ra.append(zeros(self.ra_width - sum(t.shape[1] for t in ra)))
        pd = [c_u, b_vc, a_iw, zeros(LANES - a_iw.shape[1])]
        for g in range(B_KV_GROUPS):
            pd += [b_g[:, g * self.gate_cols:(g + 1) * self.gate_cols], zeros(LANES - self.gate_cols)]
        cat = lambda ts: jnp.concatenate(ts, axis=1)
        return cat(ra), b_kc, cat([b_vs, b_vw, a_v]), cat(pd)


def _rope_tables(positions):
    half = ROT_DIM // 2
    inv = ROPE_THETA ** (-jnp.arange(0, ROT_DIM, 2, dtype=F32) / ROT_DIM)
    ang = positions.astype(F32).reshape(-1, 1) * inv
    cos, sin = jnp.cos(ang), jnp.sin(ang)
    n = ang.shape[0]
    ones, zeros, z16 = jnp.ones((n, HEAD_DIM - ROT_DIM), F32), jnp.zeros((n, HEAD_DIM - ROT_DIM), F32), jnp.zeros((n, half), F32)
    return (jnp.concatenate([cos, cos, ones], axis=1),
            jnp.concatenate([z16, sin, zeros], axis=1),
            jnp.concatenate([-sin, z16, zeros], axis=1))


def _dsa_kernel(q_ref, iq_ref, iw_ref, k_ref, v_ref, ik_ref, o_ref, *, n_keep, n_heads):
    seq = k_ref.shape[1]
    start = pl.program_id(1) * A_QBLOCK
    qpos = start + lax.broadcasted_iota(jnp.int32, (A_QBLOCK, 1), 0)

    def block(klen):
        kpos = lax.broadcasted_iota(jnp.int32, (1, klen), 1)
        causal = kpos <= qpos
        ik = ik_ref[0, :klen, :]
        iw = iw_ref[0] * (A_IDX_HEADS * A_IDX_DIM) ** -0.5
        score = jnp.zeros((A_QBLOCK, klen), F32)
        for h in range(A_IDX_HEADS):
            rel = _dot_nt(iq_ref[0, :, h * A_IDX_DIM:(h + 1) * A_IDX_DIM], ik)
            score = score + iw[:, h:h + 1] * jnp.maximum(rel, 0.0)
        score = jnp.where(causal, score, -jnp.inf)
        sel = _topk_mask(_order_key(score), kpos, n_keep, max(1, (klen - 1).bit_length())) & causal
        bias = _mask_bias(sel).astype(BF16)
        k = k_ref[0, :klen, :]
        v = _with_ones(v_ref[0, :klen, :])
        for h in range(n_heads):
            hs = slice(h * HEAD_DIM, (h + 1) * HEAD_DIM)
            o_ref[0, :, hs] = _attend(q_ref[0, :, hs], k, v, bias, HEAD_DIM ** -0.5).astype(o_ref.dtype)

    _for_causal_extent(start + A_QBLOCK, seq, block)


def _dsa_mixer(ra, pc, pd, lay):
    bsz, seq, _ = ra.shape
    n_keep = min(A_TOPK_MAX, seq // 4)
    assert seq % KEY_TILE == 0 and KEY_TILE >= n_keep
    qb = lambda width, off: pl.BlockSpec((1, A_QBLOCK, width), lambda b, i: (b, i, off // width))
    kb = lambda width, off: pl.BlockSpec((1, seq, width), lambda b, i: (b, 0, off // width))
    return pl.pallas_call(
        functools.partial(_dsa_kernel, n_keep=n_keep, n_heads=lay.qw // HEAD_DIM),
        grid=(bsz, seq // A_QBLOCK),
        in_specs=[qb(lay.qw, lay.a_q), qb(lay.iqw, lay.a_iq), qb(LANES, lay.a_iw),
                  kb(HEAD_DIM, lay.a_k), kb(HEAD_DIM, lay.a_v), kb(A_IDX_DIM, lay.a_ik)],
        out_specs=pl.BlockSpec((1, A_QBLOCK, lay.qw), lambda b, i: (b, i, 0)),
        out_shape=jax.ShapeDtypeStruct((bsz, seq, lay.qw), BF16),
        compiler_params=_params("parallel", "arbitrary"),
        name="dsa_mixer",
    )(ra, ra, pd, ra, pc, ra)


NSA_QBLOCK = 256
GELU_C = math.sqrt(2.0 / math.pi)


def _gelu_tanh(x):
    return 0.5 * x * (1.0 + jnp.tanh(GELU_C * (x + 0.044715 * (x * x * x))))


def _nsa_compress_kernel(kc_ref, vc_ref, pe_ref, w1_ref, w2_ref, ko_ref, vo_ref):
    n_chunk = ko_ref.shape[2]
    half = B_CMP_LEN // 2
    assert B_CMP_STRIDE == half
    row = lax.broadcasted_iota(jnp.int32, (n_chunk, 1), 0)
    for t, (src, dst) in enumerate(((kc_ref, ko_ref), (vc_ref, vo_ref))):
        lo = jnp.zeros((n_chunk, w1_ref.shape[-1]), F32)
        hi = jnp.zeros((n_chunk, w1_ref.shape[-1]), F32)
        for l in range(half):
            x = src[0, pl.ds(l, n_chunk, stride=B_CMP_STRIDE), :]
            lo = lo + jnp.dot((x + pe_ref[t, l:l + 1, :]).astype(BF16),
                              w1_ref[t, l * HEAD_DIM:(l + 1) * HEAD_DIM, :], preferred_element_type=F32)
            hi = hi + jnp.dot((x + pe_ref[t, half + l:half + l + 1, :]).astype(BF16),
                              w1_ref[t, (half + l) * HEAD_DIM:(half + l + 1) * HEAD_DIM, :],
                              preferred_element_type=F32)
        pre = lo + pltpu.roll(hi, n_chunk - 1, 0)
        out = jnp.dot(_gelu_tanh(pre).astype(BF16), w2_ref[t], preferred_element_type=F32)
        dst[0, 0] = jnp.where(row < n_chunk - 1, out, 0.0).astype(dst.dtype)


def _nsa_compress(rb, pd, lay, phi_pe, phi_w1, phi_w2):
    bsz, seq, _ = rb.shape
    n_chunk = seq // B_CMP_STRIDE
    cmap = lambda b, g: (0, 0, 0)
    omap = lambda b, g: (b, g, 0, 0)
    vc_blk = lay.b_vc // HEAD_DIM
    out_sds = jax.ShapeDtypeStruct((bsz, B_KV_GROUPS, n_chunk, HEAD_DIM), BF16)
    return pl.pallas_call(
        _nsa_compress_kernel,
        grid=(bsz, B_KV_GROUPS),
        in_specs=[pl.BlockSpec((1, seq, HEAD_DIM), lambda b, g: (b, 0, g)),
                  pl.BlockSpec((1, seq, HEAD_DIM), lambda b, g: (b, 0, vc_blk + g)),
                  pl.BlockSpec(phi_pe.shape, cmap),
                  pl.BlockSpec(phi_w1.shape, cmap),
                  pl.BlockSpec(phi_w2.shape, cmap)],
        out_specs=[pl.BlockSpec((1, 1, n_chunk, HEAD_DIM), omap),
                   pl.BlockSpec((1, 1, n_chunk, HEAD_DIM), omap)],
        out_shape=[out_sds, out_sds],
        compiler_params=_params("parallel", "parallel"),
        name="nsa_compress",
    )(rb, pd, phi_pe, phi_w1.astype(BF16), phi_w2.astype(BF16))


def _nsa_kernel(q_ref, gate_ref, kcmp_ref, vcmp_ref, ks_ref, vs_ref, kw_ref, vw_ref, ovl_ref, expand_ref,
                o_ref, *, n_sel, heads_per_group):
    seq = ks_ref.shape[1]
    n_cmp = kcmp_ref.shape[2]
    n_blk = seq // B_SEL_LEN
    scale = HEAD_DIM ** -0.5
    start = pl.program_id(2) * NSA_QBLOCK
    qpos = start + lax.broadcasted_iota(jnp.int32, (NSA_QBLOCK, 1), 0)
    heads = [q_ref[0, :, r * HEAD_DIM:(r + 1) * HEAD_DIM] for r in range(heads_per_group)]

    cend = lax.broadcasted_iota(jnp.int32, (1, n_cmp), 1) * B_CMP_STRIDE + (B_CMP_LEN - 1)
    cmask = _mask_bias(cend <= qpos)
    kcmp = kcmp_ref[0, 0]
    vcmp = vcmp_ref[0, 0]
    o_cmp = []
    psum = jnp.zeros((NSA_QBLOCK, n_cmp), F32)
    for q in heads:
        e, denom = _softmax_weights(_dot_nt(q, kcmp) * scale, cmask)
        p = e / denom
        psum = psum + p
        o_cmp.append(jnp.dot(p.astype(BF16), vcmp, preferred_element_type=F32))
    p_hi = psum.astype(BF16)
    p_lo = (psum - p_hi.astype(F32)).astype(BF16)
    ovl = ovl_ref[...]
    imp = (jnp.dot(p_hi, ovl, preferred_element_type=F32) + jnp.dot(p_lo, ovl, preferred_element_type=F32))
    blk = lax.broadcasted_iota(jnp.int32, (1, imp.shape[1]), 1)
    cur = qpos // B_SEL_LEN
    imp = jnp.where((blk == 0) | (blk == cur) | (blk == cur - 1), jnp.inf,
                    jnp.where(blk > cur, -jnp.inf, imp))
    blk_sel = _topk_mask_by_rank(_order_key(imp), blk, n_sel, n_blk)

    n_win = B_WIN + NSA_QBLOCK
    kstart = pl.multiple_of(jnp.maximum(start - B_WIN, 0), NSA_QBLOCK)
    wpos = kstart + lax.broadcasted_iota(jnp.int32, (1, n_win), 1)
    win_bias = _mask_bias((wpos <= qpos) & (wpos > qpos - B_WIN)).astype(BF16)
    kw = kw_ref[0, pl.ds(kstart, n_win), :]
    vw = _with_ones(vw_ref[0, pl.ds(kstart, n_win), :])
    o_win = [_attend(q, kw, vw, win_bias, scale) for q in heads]

    gates = jax.nn.sigmoid(gate_ref[0])
    sel_rows = jnp.where(blk_sel, 1.0, 0.0).astype(BF16)

    def selected_and_combine(klen):
        kpos = lax.broadcasted_iota(jnp.int32, (1, klen), 1)
        in_sel = jnp.dot(sel_rows, expand_ref[:, :klen], preferred_element_type=F32)
        sel_bias = _mask_bias((in_sel > 0.5) & (kpos <= qpos)).astype(BF16)
        ks = ks_ref[0, :klen, :]
        vs = _with_ones(vs_ref[0, :klen, :])
        for r, q in enumerate(heads):
            gr = gates[:, 3 * r:3 * r + 3]
            out = (gr[:, 0:1] * o_cmp[r] + gr[:, 1:2] * _attend(q, ks, vs, sel_bias, scale)
                   + gr[:, 2:3] * o_win[r])
            o_ref[0, :, r * HEAD_DIM:(r + 1) * HEAD_DIM] = out.astype(o_ref.dtype)

    _for_causal_extent(start + NSA_QBLOCK, seq, selected_and_combine, tile=NSA_QBLOCK)


def _nsa_mixer(ra, rb, pc, pd, lay, phi_pe, phi_w1, phi_w2):
    bsz, seq, _ = ra.shape
    hpg = lay.n_bheads // B_KV_GROUPS
    gq = hpg * HEAD_DIM
    assert seq % NSA_QBLOCK == 0 and seq >= B_WIN + NSA_QBLOCK
    kcmp, vcmp = _nsa_compress(rb, pd, lay, phi_pe, phi_w1, phi_w2)
    n_cmp = seq // B_CMP_STRIDE
    n_blk = seq // B_SEL_LEN
    n_sel = min(B_SEL_N, n_blk)
    assert n_blk <= LANES
    c_lo = np.arange(n_cmp)[:, None] * B_CMP_STRIDE
    b_lo = np.arange(LANES)[None, :] * B_SEL_LEN
    overlap = ((c_lo < b_lo + B_SEL_LEN) & (c_lo + B_CMP_LEN - 1 >= b_lo) & (np.arange(LANES)[None, :] < n_blk))
    expand = (np.arange(seq)[None, :] // B_SEL_LEN) == np.arange(LANES)[:, None]
    qb = lambda width, off: pl.BlockSpec((1, NSA_QBLOCK, width), lambda b, g, i: (b, i, off // width + g))
    kb = lambda off: pl.BlockSpec((1, seq, HEAD_DIM), lambda b, g, i: (b, 0, off // HEAD_DIM + g))
    cmap = lambda b, g, i: (b, g, 0, 0)
    const = lambda b, g, i: (0, 0)
    return pl.pallas_call(
        functools.partial(_nsa_kernel, n_sel=n_sel, heads_per_group=hpg),
        grid=(bsz, B_KV_GROUPS, seq // NSA_QBLOCK),
        in_specs=[qb(gq, lay.b_q), qb(LANES, lay.b_g),
                  pl.BlockSpec((1, 1, n_cmp, HEAD_DIM), cmap),
                  pl.BlockSpec((1, 1, n_cmp, HEAD_DIM), cmap),
                  kb(lay.b_ks), kb(lay.b_vs), kb(lay.b_kw), kb(lay.b_vw),
                  pl.BlockSpec((n_cmp, LANES), const),
                  pl.BlockSpec((LANES, seq), const)],
        out_specs=pl.BlockSpec((1, NSA_QBLOCK, gq), lambda b, g, i: (b, i, g)),
        out_shape=jax.ShapeDtypeStruct((bsz, seq, lay.qw), BF16),
        compiler_params=_params("parallel", "parallel", "arbitrary"),
        name="nsa_mixer",
    )(ra, pd, kcmp, vcmp, ra, pc, ra, pc, jnp.asarray(overlap, BF16), jnp.asarray(expand, BF16))


S5_TIME_BLOCK = 512
S5_UNROLL = 8


def _s5_scan_kernel(u_ref, bcat_ref, ccat_ref, a_ref, d_ref, z_ref, xs_ref, state_ref, bt_ref, tb_ref):
    bsz, tt, lanes = u_ref.shape
    ns = a_ref.shape[-1] // 2

    @pl.when(pl.program_id(1) == 0)
    def _():
        state_ref[...] = jnp.zeros_like(state_ref)

    bt_ref[...] = u_ref[...].reshape(bsz * tt, lanes)

    def to_time_major(t, carry):
        tb_ref[pl.ds(pl.multiple_of(t * bsz, bsz), bsz), :] = bt_ref[pl.ds(t, bsz, stride=tt), :]
        return carry

    lax.fori_loop(0, tt, to_time_major, 0, unroll=S5_UNROLL)
    u = tb_ref[...]
    xs_ref[...] = jnp.dot(u.astype(BF16), bcat_ref[0], preferred_element_type=F32)
    a_re = jnp.broadcast_to(a_ref[0, :, :ns], (bsz, ns))
    a_im = jnp.broadcast_to(a_ref[0, :, ns:], (bsz, ns))

    def step(t, carry):
        s_re, s_im = carry
        rows = pl.ds(pl.multiple_of(t * bsz, bsz), bsz)
        n_re = a_re * s_re - a_im * s_im + xs_ref[rows, :ns]
        n_im = a_re * s_im + a_im * s_re + xs_ref[rows, ns:]
        xs_ref[rows, :ns] = n_re
        xs_ref[rows, ns:] = n_im
        return n_re, n_im

    s_re, s_im = lax.fori_loop(0, tt, step, (state_ref[:, :ns], state_ref[:, ns:]), unroll=S5_UNROLL)
    state_ref[:, :ns] = s_re
    state_ref[:, ns:] = s_im
    y = jnp.dot(xs_ref[...].astype(BF16), ccat_ref[0], preferred_element_type=F32) + d_ref[0] * u
    tb_ref[...] = _gelu_tanh(y)
    for b in range(bsz):
        z_ref[b] = tb_ref[pl.ds(b, tt, stride=bsz), :]


def _s5_glu_kernel(za_ref, zt_ref, w_ref, b_ref, o_ref):
    zz = jnp.dot(za_ref[...].astype(BF16), w_ref[...], preferred_element_type=F32) + b_ref[...]
    o_ref[...] = (zt_ref[...] * jax.nn.sigmoid(zz)).astype(o_ref.dtype)


def _s5_mixer(pd, u_off, width, lam_re, lam_im, log_dt, b_re, b_im, c_re, c_im, d, glu_w, glu_b):
    bsz, seq, _ = pd.shape
    assert u_off % LANES == 0
    gpc = LANES // C_GROUP
    n_col = width // LANES
    ns = gpc * C_STATE
    tt = min(S5_TIME_BLOCK, seq)
    assert width % LANES == 0 and seq % tt == 0 and bsz % 8 == 0
    dt = jnp.exp(log_dt)[:, None]
    mag = jnp.exp(lam_re * dt)
    a_re, a_im = mag * jnp.cos(lam_im * dt), mag * jnp.sin(lam_im * dt)
    den = lam_re * lam_re + lam_im * lam_im
    co_re = ((a_re - 1.0) * lam_re + a_im * lam_im) / den
    co_im = (a_im * lam_re - (a_re - 1.0) * lam_im) / den
    bb_re = co_re[..., None] * b_re - co_im[..., None] * b_im
    bb_im = co_re[..., None] * b_im + co_im[..., None] * b_re
    eye = jnp.eye(gpc, dtype=F32)

    def block_diag(m):
        r, c = m.shape[-2:]
        return jnp.einsum('ngrc,gk->ngrkc', m, eye).reshape(n_col, gpc * r, gpc * c)

    to_in = lambda m: block_diag(m.reshape(n_col, gpc, C_STATE, C_GROUP).transpose(0, 1, 3, 2))
    to_out = lambda m: block_diag(m.reshape(n_col, gpc, C_GROUP, C_STATE).transpose(0, 1, 3, 2))
    bcat = jnp.concatenate([to_in(bb_re), to_in(bb_im)], axis=-1).astype(BF16)
    ccat = jnp.concatenate([to_out(c_re), to_out(-c_im)], axis=-2).astype(BF16)
    acat = jnp.concatenate([a_re.reshape(n_col, 1, ns), a_im.reshape(n_col, 1, ns)], axis=-1)
    z = pl.pallas_call(
        _s5_scan_kernel,
        grid=(n_col, seq // tt),
        in_specs=[pl.BlockSpec((bsz, tt, LANES), lambda c, t: (0, t, u_off // LANES + c)),
                  pl.BlockSpec((1, LANES, 2 * ns), lambda c, t: (c, 0, 0)),
                  pl.BlockSpec((1, 2 * ns, LANES), lambda c, t: (c, 0, 0)),
                  pl.BlockSpec((1, 1, 2 * ns), lambda c, t: (c, 0, 0)),
                  pl.BlockSpec((1, 1, LANES), lambda c, t: (c, 0, 0))],
        out_specs=pl.BlockSpec((bsz, tt, LANES), lambda c, t: (0, t, c)),
        out_shape=jax.ShapeDtypeStruct((bsz, seq, width), F32),
        scratch_shapes=[pltpu.VMEM((tt * bsz, 2 * ns), F32), pltpu.VMEM((bsz, 2 * ns), F32),
                        pltpu.VMEM((bsz * tt, LANES), F32), pltpu.VMEM((tt * bsz, LANES), F32)],
        compiler_params=_params("parallel", "arbitrary"),
        name="s5_scan",
    )(pd, bcat, ccat, acat, d.reshape(n_col, 1, LANES))
    z = z.reshape(bsz * seq, width)
    tm, tn = min(ROW_TILE, seq * bsz), min(512, width)
    return pl.pallas_call(
        _s5_glu_kernel,
        grid=(seq * bsz // tm, width // tn),
        in_specs=[pl.BlockSpec((tm, width), lambda i, j: (i, 0)),
                  pl.BlockSpec((tm, tn), lambda i, j: (i, j)),
                  pl.BlockSpec((width, tn), lambda i, j: (0, j)),
                  pl.BlockSpec((1, tn), lambda i, j: (0, j))],
        out_specs=pl.BlockSpec((tm, tn), lambda i, j: (i, j)),
        out_shape=jax.ShapeDtypeStruct((seq * bsz, width), BF16),
        compiler_params=_params("parallel", "parallel"),
        name="s5_glu",
    )(z, z, glu_w.astype(BF16), glu_b.reshape(1, width))


def _mixing_sublayer(hb, bsz, seq, rope, lay, w_in, phi_pe, phi_w1, phi_w2, lam_re, lam_im, log_dt,
                     b_re, b_im, c_re, c_im, s5_d, glu_w, glu_b, w_out):
    w_ra, w_rb, w_pc, w_pd = lay.weights(w_in)
    view = lambda t: t.reshape(bsz, seq, t.shape[-1])
    ra = view(_matmul(hb, w_ra, lay.ra_tile, BF16, rope))
    rb = view(_matmul(hb, w_rb, w_rb.shape[1], F32, rope))
    pc = view(_matmul(hb, w_pc, w_pc.shape[1], BF16))
    pd = view(_matmul(hb, w_pd, lay.pd_tile, F32))
    o_a = _dsa_mixer(ra, pc, pd, lay)
    o_b = _nsa_mixer(ra, rb, pc, pd, lay, phi_pe, phi_w1, phi_w2)
    o_c = _s5_mixer(pd, lay.c_u, lay.cw, lam_re, lam_im, log_dt, b_re, b_im, c_re, c_im, s5_d, glu_w, glu_b)
    return _out_proj(o_a.reshape(bsz * seq, -1), o_b.reshape(bsz * seq, -1), o_c, w_out.astype(BF16))


def _split_bf16(x):
    hi = x.astype(BF16)
    return hi, (x - hi.astype(F32)).astype(BF16)


def _route_rows(x, wh_ref, wl_ref, bias_ref, eid_ref, gate_ref):
    x_hi, x_lo = _split_bf16(x)
    wh = wh_ref[...]
    logits = (jnp.dot(x_hi, wh, preferred_element_type=F32) + jnp.dot(x_hi, wl_ref[...], preferred_element_type=F32)
              + jnp.dot(x_lo, wh, preferred_element_type=F32))
    aff = jax.nn.sigmoid(logits)
    lane_i = lax.broadcasted_iota(jnp.int32, (1, LANES), 1)
    lane = lane_i.astype(F32)
    grp = (lane_i // EXPERTS_PER_GROUP).astype(F32)
    sel = jnp.where(lane_i < N_EXPERTS, aff + bias_ref[...], -jnp.inf)

    def top2(v):
        m1 = jnp.max(v, axis=-1, keepdims=True)
        i1 = jnp.min(jnp.where(v == m1, lane, float(LANES)), axis=-1, keepdims=True)
        v2 = jnp.where(lane == i1, -jnp.inf, v)
        m2 = jnp.max(v2, axis=-1, keepdims=True)
        i2 = jnp.min(jnp.where(v2 == m2, lane, float(LANES)), axis=-1, keepdims=True)
        return m1, i1, m2, i2

    best_score = best_grp = None
    for g in range(N_EXPERT_GROUPS):
        m1, _, m2, _ = top2(jnp.where(grp == float(g), sel, -jnp.inf))
        score = m1 + m2
        if g == 0:
            best_score, best_grp = score, jnp.zeros_like(score)
        else:
            better = score > best_score
            best_score = jnp.where(better, score, best_score)
            best_grp = jnp.where(better, float(g), best_grp)
    _, i1, _, i2 = top2(jnp.where(grp == best_grp, sel, -jnp.inf))
    w1 = jnp.sum(jnp.where(lane == i1, aff, 0.0), axis=-1, keepdims=True)
    w2 = jnp.sum(jnp.where(lane == i2, aff, 0.0), axis=-1, keepdims=True)
    tot = w1 + w2
    eid_ref[...] = jnp.where(lane_i == 0, i1, jnp.where(lane_i == 1, i2, 0.0)).astype(jnp.int32)
    gate_ref[...] = jnp.where(lane_i == 0, w1 / tot, jnp.where(lane_i == 1, w2 / tot, 0.0))


def _moe(xt, eid, layer, w_gate, w_up, w_down):
    n_tok, dm = xt.shape
    n_asg = n_tok * TOP_K
    cap = -(-n_asg // MOE_BLOCK) * MOE_BLOCK + N_EXPERTS * MOE_BLOCK
    onehot = (eid.reshape(n_asg, 1) == jnp.arange(N_EXPERTS, dtype=jnp.int32)[None, :]).astype(jnp.int32)
    seen = jnp.cumsum(onehot, axis=0)
    counts = seen[-1]
    padded = (counts + MOE_BLOCK - 1) // MOE_BLOCK * MOE_BLOCK
    pend = jnp.cumsum(padded)
    pstart = pend - padded
    dest = jnp.sum(onehot * (seen - 1 + pstart[None, :]), axis=1).astype(jnp.int32)
    tok_flat = jnp.arange(n_asg, dtype=jnp.int32) // TOP_K
    slot_tok = (jnp.arange(cap, dtype=jnp.int32) % n_tok).at[dest].set(tok_flat)
    n_blocks = cap // MOE_BLOCK
    blk_start = jnp.arange(n_blocks) * MOE_BLOCK
    blk_exp = jnp.minimum(jnp.sum(pend[None, :] <= blk_start[:, None], axis=1), N_EXPERTS - 1).astype(jnp.int32)
    blk_first = jnp.concatenate([jnp.ones((1,), jnp.int32), (blk_exp[1:] != blk_exp[:-1]).astype(jnp.int32)])
    n_act = (pend[-1] // MOE_BLOCK).astype(jnp.int32).reshape(1)
    y = _moe_experts(xt[slot_tok], blk_exp, blk_first, n_act, layer, w_gate, w_up, w_down)
    pos = dest.reshape(n_tok, TOP_K)
    return y[pos[:, 0]], y[pos[:, 1]]


def kernel(x, positions, ln_in_g, ln_in_b, w_in, nsa_phi_pe, nsa_phi_w1, nsa_phi_w2, s5_lam_re, s5_lam_im, s5_log_dt, s5_b_re, s5_b_im, s5_c_re, s5_c_im, s5_d, s5_glu_w, s5_glu_b, w_out, ln_mix_g, ln_mix_b, router_w, router_bias, moe_w_gate, moe_w_up, moe_w_down, ln_ffn_g, ln_ffn_b):
    depth = w_in.shape[0]
    bsz, seq, d_model = x.shape
    alpha = (2 * depth) ** 0.25
    lay = _Packed(d_model)
    rope = _rope_tables(positions)
    h, hb = _layer_norm(x.reshape(bsz * seq, d_model), ln_in_g, ln_in_b)
    for l in range(depth):
        mix = _mixing_sublayer(hb, bsz, seq, rope, lay, w_in[l], nsa_phi_pe[l], nsa_phi_w1[l], nsa_phi_w2[l],
                               s5_lam_re[l], s5_lam_im[l], s5_log_dt[l], s5_b_re[l], s5_b_im[l],
                               s5_c_re[l], s5_c_im[l], s5_d[l], s5_glu_w[l], s5_glu_b[l], w_out[l])
        h, hb, eid, gate = _layer_norm_and_route(h, ln_mix_g[l], ln_mix_b[l], mix, alpha, router_w, router_bias)
        y0, y1 = _moe(h, eid, l, moe_w_gate, moe_w_up, moe_w_down)
        h, hb = _layer_norm(h, ln_ffn_g[l], ln_ffn_b[l], gated_pair=(y0, y1, gate), alpha=alpha)
    return h.reshape(bsz, seq, d_model)
```

```python
import functools
import math

import numpy as np
import jax
import jax.numpy as jnp
from jax import lax
from jax.experimental import pallas as pl
from jax.experimental.pallas import tpu as pltpu

F32 = jnp.float32
BF16 = jnp.bfloat16

LANES = 128
HEAD_DIM = 128
ROT_DIM = HEAD_DIM // 4
ROPE_THETA = 500000.0
A_IDX_HEADS = 4
A_IDX_DIM = 128
A_TOPK_MAX = 256
A_QBLOCK = 128
B_KV_GROUPS = 2
B_CMP_LEN = 32
B_CMP_STRIDE = 16
B_SEL_LEN = 64
B_SEL_N = 16
B_WIN = 512
C_GROUP = 16
C_STATE = 64
N_EXPERTS = 32
N_EXPERT_GROUPS = 4
EXPERTS_PER_GROUP = N_EXPERTS // N_EXPERT_GROUPS
TOP_K = 2
MOE_BLOCK = 256
LN_EPS = 1e-5

VMEM_LIMIT_BYTES = 56 * 1024 * 1024
ROW_TILE = 1024


def _params(*semantics):
    return pltpu.CompilerParams(dimension_semantics=semantics, vmem_limit_bytes=VMEM_LIMIT_BYTES)


def _mm_kernel(a_ref, b_ref, o_ref):
    o_ref[...] = jnp.dot(a_ref[...], b_ref[...], preferred_element_type=F32).astype(o_ref.dtype)


def _mm_rope_kernel(a_ref, b_ref, c_ref, sa_ref, sb_ref, o_ref):
    y = jnp.dot(a_ref[...], b_ref[...], preferred_element_type=F32)
    half = ROT_DIM // 2
    for hb in range(y.shape[1] // HEAD_DIM):
        cols = slice(hb * HEAD_DIM, (hb + 1) * HEAD_DIM)
        yh = y[:, cols]
        out = (yh * c_ref[...] + pltpu.roll(yh, half, 1) * sa_ref[...]
               + pltpu.roll(yh, HEAD_DIM - half, 1) * sb_ref[...])
        o_ref[:, cols] = out.astype(o_ref.dtype)


def _matmul(a, b, tn, out_dtype, rope=None):
    m, k = a.shape
    n = b.shape[1]
    tm = min(ROW_TILE, m)
    assert m % tm == 0 and n % tn == 0
    in_specs = [pl.BlockSpec((tm, k), lambda i, j: (i, 0)),
                pl.BlockSpec((k, tn), lambda i, j: (0, j))]
    args = (a, b)
    if rope is not None:
        in_specs += [pl.BlockSpec((tm, HEAD_DIM), lambda i, j: (i, 0))] * 3
        args += tuple(rope)
    return pl.pallas_call(
        _mm_kernel if rope is None else _mm_rope_kernel,
        grid=(m // tm, n // tn),
        in_specs=in_specs,
        out_specs=pl.BlockSpec((tm, tn), lambda i, j: (i, j)),
        out_shape=jax.ShapeDtypeStruct((m, n), out_dtype),
        compiler_params=_params("parallel", "parallel"),
        name="in_proj" if rope is None else "in_proj_rope",
    )(*args)


def _out_proj_kernel(a0_ref, a1_ref, a2_ref, w_ref, o_ref):
    k0 = a0_ref.shape[1]
    k1 = k0 + a1_ref.shape[1]
    acc = jnp.dot(a0_ref[...], w_ref[:k0, :], preferred_element_type=F32)
    acc = acc + jnp.dot(a1_ref[...], w_ref[k0:k1, :], preferred_element_type=F32)
    o_ref[...] = acc + jnp.dot(a2_ref[...], w_ref[k1:, :], preferred_element_type=F32)


def _out_proj(a0, a1, a2, w, tn=512):
    m = a0.shape[0]
    k, n = w.shape
    tm = min(ROW_TILE, m)
    assert a0.shape[1] + a1.shape[1] + a2.shape[1] == k and m % tm == 0 and n % tn == 0
    amap = lambda i, j: (i, 0)
    return pl.pallas_call(
        _out_proj_kernel,
        grid=(m // tm, n // tn),
        in_specs=[pl.BlockSpec((tm, a0.shape[1]), amap), pl.BlockSpec((tm, a1.shape[1]), amap),
                  pl.BlockSpec((tm, a2.shape[1]), amap), pl.BlockSpec((k, tn), lambda i, j: (0, j))],
        out_specs=pl.BlockSpec((tm, tn), lambda i, j: (i, j)),
        out_shape=jax.ShapeDtypeStruct((m, n), F32),
        compiler_params=_params("parallel", "parallel"),
        name="out_proj",
    )(a0, a1, a2, w)


LN_ROWS = 256


def _ln_store(v, g_ref, b_ref, o_ref, ob_ref):
    mu = jnp.mean(v, axis=-1, keepdims=True)
    c = v - mu
    var = jnp.mean(c * c, axis=-1, keepdims=True)
    out = c * lax.rsqrt(var + LN_EPS) * g_ref[...] + b_ref[...]
    o_ref[...] = out
    ob_ref[...] = out.astype(BF16)
    return out


def _ln_kernel(x_ref, g_ref, b_ref, o_ref, ob_ref):
    _ln_store(x_ref[...], g_ref, b_ref, o_ref, ob_ref)


def _ln_residual_route_kernel(x_ref, r_ref, g_ref, b_ref, wh_ref, wl_ref, rbias_ref, o_ref, ob_ref, eid_ref, gate_ref,
                              *, alpha):
    out = _ln_store(alpha * x_ref[...] + r_ref[...], g_ref, b_ref, o_ref, ob_ref)
    _route_rows(out, wh_ref, wl_ref, rbias_ref, eid_ref, gate_ref)


def _ln_gated_pair_kernel(x_ref, y0_ref, y1_ref, w_ref, g_ref, b_ref, o_ref, ob_ref, *, alpha):
    w = w_ref[...]
    ffn = w[:, 0:1] * y0_ref[...] + w[:, 1:2] * y1_ref[...]
    _ln_store(alpha * x_ref[...] + ffn, g_ref, b_ref, o_ref, ob_ref)


def _layer_norm_and_route(x, g, b, residual, alpha, router_w, router_bias):
    m, d = x.shape
    assert m % LN_ROWS == 0 and N_EXPERTS <= LANES
    pad = LANES - N_EXPERTS
    wh, wl = _split_bf16(jnp.pad(router_w, ((0, 0), (0, pad))))
    rbias = jnp.pad(router_bias, (0, pad)).reshape(1, LANES)
    rows = lambda w: pl.BlockSpec((LN_ROWS, w), lambda i: (i, 0))
    const = lambda shape: pl.BlockSpec(shape, lambda i: (0, 0))
    h, hb, eid, gate = pl.pallas_call(
        functools.partial(_ln_residual_route_kernel, alpha=alpha), grid=(m // LN_ROWS,),
        in_specs=[rows(d), rows(d), const((1, d)), const((1, d)), const((d, LANES)), const((d, LANES)),
                  const((1, LANES))],
        out_specs=[rows(d), rows(d), rows(LANES), rows(LANES)],
        out_shape=[jax.ShapeDtypeStruct((m, d), F32), jax.ShapeDtypeStruct((m, d), BF16),
                   jax.ShapeDtypeStruct((m, LANES), jnp.int32), jax.ShapeDtypeStruct((m, LANES), F32)],
        compiler_params=_params("parallel"), name="layer_norm_route",
    )(x, residual, g.reshape(1, d), b.reshape(1, d), wh, wl, rbias)
    return h, hb, eid[:, :TOP_K], gate[:, :TOP_K]


def _layer_norm(x, g, b, alpha=1.0, gated_pair=None):
    m, d = x.shape
    rows = pl.BlockSpec((LN_ROWS, d), lambda i: (i, 0))
    vec = pl.BlockSpec((1, d), lambda i: (0, 0))
    assert m % LN_ROWS == 0
    if gated_pair is not None:
        y0, y1, w = gated_pair
        body = functools.partial(_ln_gated_pair_kernel, alpha=alpha)
        ins = [rows, rows, rows, pl.BlockSpec((LN_ROWS, w.shape[1]), lambda i: (i, 0)), vec, vec]
        args = (x, y0, y1, w, g.reshape(1, d), b.reshape(1, d))
    else:
        body, ins, args = _ln_kernel, [rows, vec, vec], (x, g.reshape(1, d), b.reshape(1, d))
    return pl.pallas_call(
        body, grid=(m // LN_ROWS,), in_specs=ins, out_specs=[rows, rows],
        out_shape=[jax.ShapeDtypeStruct((m, d), F32), jax.ShapeDtypeStruct((m, d), BF16)],
        compiler_params=_params("parallel"), name="layer_norm",
    )(*args)


MOE_SPLIT = 2


def _moe_up_kernel(blk_exp_ref, first_ref, n_act_ref, x_ref, wg_ref, wu_ref, hid_ref, wres_ref, acc_ref):
    i, k = pl.program_id(0), pl.program_id(1)
    active = i < n_act_ref[0]

    de = hid_ref.shape[1]

    @pl.when(active & (first_ref[i] == 1))
    def _():
        wres_ref[k, :, :de] = wg_ref[0, 0].astype(BF16)
        wres_ref[k, :, de:] = wu_ref[0, 0].astype(BF16)

    @pl.when(active)
    def _():
        gu = jnp.dot(x_ref[...].astype(BF16), wres_ref[k], preferred_element_type=F32)

        @pl.when(k == 0)
        def _():
            acc_ref[...] = gu

        @pl.when(k > 0)
        def _():
            acc_ref[...] += gu

        @pl.when(k == MOE_SPLIT - 1)
        def _():
            g_all = acc_ref[:, :de]
            hid_ref[...] = (g_all * jax.nn.sigmoid(g_all) * acc_ref[:, de:]).astype(hid_ref.dtype)

    @pl.when(jnp.logical_not(active) & (k == MOE_SPLIT - 1))
    def _():
        hid_ref[...] = jnp.zeros_like(hid_ref)


def _moe_down_kernel(blk_exp_ref, first_ref, n_act_ref, hid_ref, wd_ref, y_ref, wres_ref):
    i, n = pl.program_id(0), pl.program_id(1)
    active = i < n_act_ref[0]

    @pl.when(active & (first_ref[i] == 1))
    def _():
        wres_ref[n] = wd_ref[0, 0].astype(BF16)

    @pl.when(active)
    def _():
        y_ref[...] = jnp.dot(hid_ref[...], wres_ref[n], preferred_element_type=F32)

    @pl.when(jnp.logical_not(active))
    def _():
        y_ref[...] = jnp.zeros_like(y_ref)


def _moe_experts(xs, blk_exp, blk_first, n_act, layer, wg, wu, wd):
    cap, dm = xs.shape
    de = wg.shape[-1]
    n_blocks = cap // MOE_BLOCK
    tk, tn = dm // MOE_SPLIT, dm // MOE_SPLIT
    tile = lambda j, first: jnp.where(first == 1, j, MOE_SPLIT - 1)
    hid = pl.pallas_call(
        _moe_up_kernel,
        grid_spec=pltpu.PrefetchScalarGridSpec(
            num_scalar_prefetch=3, grid=(n_blocks, MOE_SPLIT),
            in_specs=[pl.BlockSpec((MOE_BLOCK, tk), lambda i, k, be, bf, na: (i, k)),
                      pl.BlockSpec((1, 1, tk, de), lambda i, k, be, bf, na: (layer, be[i], tile(k, bf[i]), 0)),
                      pl.BlockSpec((1, 1, tk, de), lambda i, k, be, bf, na: (layer, be[i], tile(k, bf[i]), 0))],
            out_specs=pl.BlockSpec((MOE_BLOCK, de), lambda i, k, be, bf, na: (i, 0)),
            scratch_shapes=[pltpu.VMEM((MOE_SPLIT, tk, 2 * de), BF16), pltpu.VMEM((MOE_BLOCK, 2 * de), F32)]),
        out_shape=jax.ShapeDtypeStruct((cap, de), BF16),
        compiler_params=_params("arbitrary", "arbitrary"),
        name="moe_gate_up",
    )(blk_exp, blk_first, n_act, xs, wg, wu)
    return pl.pallas_call(
        _moe_down_kernel,
        grid_spec=pltpu.PrefetchScalarGridSpec(
            num_scalar_prefetch=3, grid=(n_blocks, MOE_SPLIT),
            in_specs=[pl.BlockSpec((MOE_BLOCK, de), lambda i, n, be, bf, na: (i, 0)),
                      pl.BlockSpec((1, 1, de, tn), lambda i, n, be, bf, na: (layer, be[i], 0, tile(n, bf[i])))],
            out_specs=pl.BlockSpec((MOE_BLOCK, tn), lambda i, n, be, bf, na: (i, n)),
            scratch_shapes=[pltpu.VMEM((MOE_SPLIT, de, tn), BF16)]),
        out_shape=jax.ShapeDtypeStruct((cap, dm), F32),
        compiler_params=_params("arbitrary", "arbitrary"),
        name="moe_down",
    )(blk_exp, blk_first, n_act, hid, wd)


INT32_MIN = -2 ** 31
BISECT_UNROLL = 8


def _order_key(x):
    b = lax.bitcast_convert_type(x + 0.0, jnp.int32)
    return b ^ ((b >> 31) & 0x7FFFFFFF)


def _count(mask):
    return jnp.sum(jnp.where(mask, 1.0, 0.0), axis=-1, keepdims=True)


def _topk_mask(key, pos, k, n_pos_bits):
    kf = float(k)
    thr = jnp.where(_count(key >= 0) >= kf, 0, INT32_MIN).astype(jnp.int32)

    ones = jnp.ones((key.shape[1], LANES), BF16)

    def value_step(i, thr):
        cand = thr + jnp.left_shift(jnp.int32(1), 30 - i)
        hits = jnp.dot(jnp.where(key >= cand, 1.0, 0.0).astype(BF16), ones, preferred_element_type=F32)
        return jnp.where(hits[:, :1] >= kf, cand, thr)

    thr = lax.fori_loop(0, 31, value_step, thr, unroll=BISECT_UNROLL)
    above = key > thr
    n_pos = 1 << n_pos_bits
    tied_pos = jnp.where(key == thr, pos, n_pos)
    need = kf - _count(above)

    def take_lowest(_):
        def pos_step(i, last):
            cand = last + jnp.left_shift(jnp.int32(1), n_pos_bits - 1 - i)
            return jnp.where(_count(tied_pos < cand) < need, cand, last)

        return lax.fori_loop(0, n_pos_bits, pos_step, jnp.zeros_like(thr), unroll=BISECT_UNROLL)

    def take_all(_):
        return jnp.full_like(thr, n_pos - 1)

    surplus = jnp.max(_count(tied_pos < n_pos) - need)
    last = lax.cond(surplus > 0.0, take_lowest, take_all, None)
    return above | (tied_pos <= last)


def _topk_mask_by_rank(key, pos, k, n):
    rank = jnp.zeros(key.shape, F32)
    for m in range(n):
        rank = rank + jnp.where(key[:, m:m + 1] > key - jnp.where(pos > m, 1, 0), 1.0, 0.0)
    return (rank < float(k)) & (pos < n)


def _mask_bias(mask):
    return jnp.where(mask, 0.0, -jnp.inf)


def _softmax_weights(s, bias):
    s = s + bias
    m = jnp.max(s, axis=-1, keepdims=True)
    m = jnp.where(jnp.isfinite(m), m, 0.0)
    e = jnp.exp(s - m)
    return e, jnp.maximum(jnp.sum(e, axis=-1, keepdims=True), 1e-30)


def _dot_nt(a, b):
    return lax.dot_general(a, b, (((1,), (1,)), ((), ())), preferred_element_type=F32)


def _attend(q, k, v, bias, scale):
    s = (_dot_nt(q, k) * scale).astype(BF16) + bias
    m = jnp.max(s, axis=-1, keepdims=True).astype(F32)
    m = jnp.where(jnp.isfinite(m), m, 0.0).astype(BF16)
    e = jnp.exp(s - m)
    both = jnp.dot(e, v, preferred_element_type=F32)
    dv = v.shape[1] // 2
    return both[:, :dv] / jnp.maximum(both[:, dv:], 1e-30)


def _with_ones(v):
    return jnp.concatenate([v, jnp.ones(v.shape, v.dtype)], axis=1)


KEY_TILE = 512


def _for_causal_extent(q_end, seq, body):
    for klen in range(KEY_TILE, seq + 1, KEY_TILE):
        pl.when((q_end > klen - KEY_TILE) & (q_end <= klen))(functools.partial(body, klen))


class _Packed:
    def __init__(self, d_model):
        self.qw = d_model // 4
        self.gw = B_KV_GROUPS * HEAD_DIM
        self.iqw = A_IDX_HEADS * A_IDX_DIM
        self.cw = d_model // 2
        self.n_bheads = self.qw // HEAD_DIM
        self.gate_cols = 3 * self.n_bheads // B_KV_GROUPS
        qw, gw = self.qw, self.gw
        self.in_sizes = (qw, HEAD_DIM, HEAD_DIM, self.iqw, A_IDX_DIM, A_IDX_HEADS, qw,
                         gw, gw, gw, gw, gw, gw, 3 * self.n_bheads, self.cw)
        self.ra_tile = 512
        o = 0
        self.a_q = o; o += qw
        self.b_q = o; o += qw
        self.a_iq = o; o += self.iqw
        self.b_ks = o; o += gw
        self.b_kw = o; o += gw
        self.a_k = o; o += HEAD_DIM
        self.a_ik = o; o += A_IDX_DIM
        self.ra_width = -(-o // self.ra_tile) * self.ra_tile
        self.b_vs, self.b_vw, self.a_v = 0, gw, 2 * gw
        self.pc_width = 2 * gw + HEAD_DIM
        self.c_u = 0
        self.b_vc = self.cw
        self.a_iw = self.b_vc + gw
        self.b_g = self.a_iw + LANES
        self.pd_width = self.b_g + B_KV_GROUPS * LANES
        self.pd_tile = next(t for t in (896, 768, 640, 512, 384, 256, 128) if self.pd_width % t == 0)
        assert qw % 512 == 0 and self.iqw % 512 == 0 and self.a_iq % 512 == 0

    def weights(self, w_in):
        cuts = np.cumsum(self.in_sizes)[:-1].tolist()
        (a_q, a_k, a_v, a_iq, a_ik, a_iw, b_q, b_kc, b_vc, b_ks, b_vs, b_kw, b_vw, b_g,
         c_u) = jnp.split(w_in.astype(BF16), cuts, axis=1)
        zeros = lambda n: jnp.zeros((w_in.shape[0], n), BF16)
        ra = [a_q, b_q, a_iq, b_ks, b_kw, a_k, a_ik]
        ra.append(zeros(self.ra_width - sum(t.shape[1] for t in ra)))
        pd = [c_u, b_vc, a_iw, zeros(LANES - a_iw.shape[1])]
        for g in range(B_KV_GROUPS):
            pd += [b_g[:, g * self.gate_cols:(g + 1) * self.gate_cols], zeros(LANES - self.gate_cols)]
        cat = lambda ts: jnp.concatenate(ts, axis=1)
        return cat(ra), b_kc, cat([b_vs, b_vw, a_v]), cat(pd)


def _rope_tables(positions):
    half = ROT_DIM // 2
    inv = ROPE_THETA ** (-jnp.arange(0, ROT_DIM, 2, dtype=F32) / ROT_DIM)
    ang = positions.astype(F32).reshape(-1, 1) * inv
    cos, sin = jnp.cos(ang), jnp.sin(ang)
    n = ang.shape[0]
    ones, zeros, z16 = jnp.ones((n, HEAD_DIM - ROT_DIM), F32), jnp.zeros((n, HEAD_DIM - ROT_DIM), F32), jnp.zeros((n, half), F32)
    return (jnp.concatenate([cos, cos, ones], axis=1),
            jnp.concatenate([z16, sin, zeros], axis=1),
            jnp.concatenate([-sin, z16, zeros], axis=1))


def _dsa_kernel(q_ref, iq_ref, iw_ref, k_ref, v_ref, ik_ref, o_ref, *, n_keep, n_heads):
    seq = k_ref.shape[1]
    start = pl.program_id(1) * A_QBLOCK
    qpos = start + lax.broadcasted_iota(jnp.int32, (A_QBLOCK, 1), 0)

    def block(klen):
        kpos = lax.broadcasted_iota(jnp.int32, (1, klen), 1)
        causal = kpos <= qpos
        ik = ik_ref[0, :klen, :]
        iw = iw_ref[0] * (A_IDX_HEADS * A_IDX_DIM) ** -0.5
        score = jnp.zeros((A_QBLOCK, klen), F32)
        for h in range(A_IDX_HEADS):
            rel = _dot_nt(iq_ref[0, :, h * A_IDX_DIM:(h + 1) * A_IDX_DIM], ik)
            score = score + iw[:, h:h + 1] * jnp.maximum(rel, 0.0)
        score = jnp.where(causal, score, -jnp.inf)
        sel = _topk_mask(_order_key(score), kpos, n_keep, max(1, (klen - 1).bit_length())) & causal
        bias = _mask_bias(sel).astype(BF16)
        k = k_ref[0, :klen, :]
        v = _with_ones(v_ref[0, :klen, :])
        for h in range(n_heads):
            hs = slice(h * HEAD_DIM, (h + 1) * HEAD_DIM)
            o_ref[0, :, hs] = _attend(q_ref[0, :, hs], k, v, bias, HEAD_DIM ** -0.5).astype(o_ref.dtype)

    _for_causal_extent(start + A_QBLOCK, seq, block)


def _dsa_mixer(ra, pc, pd, lay):
    bsz, seq, _ = ra.shape
    n_keep = min(A_TOPK_MAX, seq // 4)
    assert seq % KEY_TILE == 0 and KEY_TILE >= n_keep
    qb = lambda width, off: pl.BlockSpec((1, A_QBLOCK, width), lambda b, i: (b, i, off // width))
    kb = lambda width, off: pl.BlockSpec((1, seq, width), lambda b, i: (b, 0, off // width))
    return pl.pallas_call(
        functools.partial(_dsa_kernel, n_keep=n_keep, n_heads=lay.qw // HEAD_DIM),
        grid=(bsz, seq // A_QBLOCK),
        in_specs=[qb(lay.qw, lay.a_q), qb(lay.iqw, lay.a_iq), qb(LANES, lay.a_iw),
                  kb(HEAD_DIM, lay.a_k), kb(HEAD_DIM, lay.a_v), kb(A_IDX_DIM, lay.a_ik)],
        out_specs=pl.BlockSpec((1, A_QBLOCK, lay.qw), lambda b, i: (b, i, 0)),
        out_shape=jax.ShapeDtypeStruct((bsz, seq, lay.qw), BF16),
        compiler_params=_params("parallel", "arbitrary"),
        name="dsa_mixer",
    )(ra, ra, pd, ra, pc, ra)


NSA_QBLOCK = 256
GELU_C = math.sqrt(2.0 / math.pi)


def _gelu_tanh(x):
    return 0.5 * x * (1.0 + jnp.tanh(GELU_C * (x + 0.044715 * (x * x * x))))


def _nsa_compress_kernel(kc_ref, vc_ref, pe_ref, w1_ref, w2_ref, ko_ref, vo_ref):
    n_chunk = ko_ref.shape[2]
    half = B_CMP_LEN // 2
    assert B_CMP_STRIDE == half
    row = lax.broadcasted_iota(jnp.int32, (n_chunk, 1), 0)
    for t, (src, dst) in enumerate(((kc_ref, ko_ref), (vc_ref, vo_ref))):
        lo = jnp.zeros((n_chunk, w1_ref.shape[-1]), F32)
        hi = jnp.zeros((n_chunk, w1_ref.shape[-1]), F32)
        for l in range(half):
            x = src[0, pl.ds(l, n_chunk, stride=B_CMP_STRIDE), :]
            lo = lo + jnp.dot((x + pe_ref[t, l:l + 1, :]).astype(BF16),
                              w1_ref[t, l * HEAD_DIM:(l + 1) * HEAD_DIM, :], preferred_element_type=F32)
            hi = hi + jnp.dot((x + pe_ref[t, half + l:half + l + 1, :]).astype(BF16),
                              w1_ref[t, (half + l) * HEAD_DIM:(half + l + 1) * HEAD_DIM, :],
                              preferred_element_type=F32)
        pre = lo + pltpu.roll(hi, n_chunk - 1, 0)
        out = jnp.dot(_gelu_tanh(pre).astype(BF16), w2_ref[t], preferred_element_type=F32)
        dst[0, 0] = jnp.where(row < n_chunk - 1, out, 0.0).astype(dst.dtype)


def _nsa_compress(rb, pd, lay, phi_pe, phi_w1, phi_w2):
    bsz, seq, _ = rb.shape
    n_chunk = seq // B_CMP_STRIDE
    cmap = lambda b, g: (0, 0, 0)
    omap = lambda b, g: (b, g, 0, 0)
    vc_blk = lay.b_vc // HEAD_DIM
    out_sds = jax.ShapeDtypeStruct((bsz, B_KV_GROUPS, n_chunk, HEAD_DIM), BF16)
    return pl.pallas_call(
        _nsa_compress_kernel,
        grid=(bsz, B_KV_GROUPS),
        in_specs=[pl.BlockSpec((1, seq, HEAD_DIM), lambda b, g: (b, 0, g)),
                  pl.BlockSpec((1, seq, HEAD_DIM), lambda b, g: (b, 0, vc_blk + g)),
                  pl.BlockSpec(phi_pe.shape, cmap),
                  pl.BlockSpec(phi_w1.shape, cmap),
                  pl.BlockSpec(phi_w2.shape, cmap)],
        out_specs=[pl.BlockSpec((1, 1, n_chunk, HEAD_DIM), omap),
                   pl.BlockSpec((1, 1, n_chunk, HEAD_DIM), omap)],
        out_shape=[out_sds, out_sds],
        compiler_params=_params("parallel", "parallel"),
        name="nsa_compress",
    )(rb, pd, phi_pe, phi_w1.astype(BF16), phi_w2.astype(BF16))


def _nsa_kernel(q_ref, gate_ref, kcmp_ref, vcmp_ref, ks_ref, vs_ref, kw_ref, vw_ref, ovl_ref, expand_ref,
                o_ref, *, n_sel, heads_per_group):
    seq = ks_ref.shape[1]
    n_cmp = kcmp_ref.shape[2]
    n_blk = seq // B_SEL_LEN
    scale = HEAD_DIM ** -0.5
    start = pl.program_id(2) * NSA_QBLOCK
    qpos = start + lax.broadcasted_iota(jnp.int32, (NSA_QBLOCK, 1), 0)
    heads = [q_ref[0, :, r * HEAD_DIM:(r + 1) * HEAD_DIM] for r in range(heads_per_group)]

    cend = lax.broadcasted_iota(jnp.int32, (1, n_cmp), 1) * B_CMP_STRIDE + (B_CMP_LEN - 1)
    cmask = _mask_bias(cend <= qpos)
    kcmp = kcmp_ref[0, 0]
    vcmp = vcmp_ref[0, 0]
    o_cmp = []
    psum = jnp.zeros((NSA_QBLOCK, n_cmp), F32)
    for q in heads:
        e, denom = _softmax_weights(_dot_nt(q, kcmp) * scale, cmask)
        p = e / denom
        psum = psum + p
        o_cmp.append(jnp.dot(p.astype(BF16), vcmp, preferred_element_type=F32))
    p_hi = psum.astype(BF16)
    p_lo = (psum - p_hi.astype(F32)).astype(BF16)
    ovl = ovl_ref[...]
    imp = (jnp.dot(p_hi, ovl, preferred_element_type=F32) + jnp.dot(p_lo, ovl, preferred_element_type=F32))
    blk = lax.broadcasted_iota(jnp.int32, (1, imp.shape[1]), 1)
    cur = qpos // B_SEL_LEN
    imp = jnp.where((blk == 0) | (blk == cur) | (blk == cur - 1), jnp.inf,
                    jnp.where(blk > cur, -jnp.inf, imp))
    blk_sel = _topk_mask_by_rank(_order_key(imp), blk, n_sel, n_blk)

    n_win = B_WIN + NSA_QBLOCK
    kstart = pl.multiple_of(jnp.maximum(start - B_WIN, 0), NSA_QBLOCK)
    wpos = kstart + lax.broadcasted_iota(jnp.int32, (1, n_win), 1)
    win_bias = _mask_bias((wpos <= qpos) & (wpos > qpos - B_WIN)).astype(BF16)
    kw = kw_ref[0, pl.ds(kstart, n_win), :]
    vw = _with_ones(vw_ref[0, pl.ds(kstart, n_win), :])
    o_win = [_attend(q, kw, vw, win_bias, scale) for q in heads]

    gates = jax.nn.sigmoid(gate_ref[0])
    sel_rows = jnp.where(blk_sel, 1.0, 0.0).astype(BF16)

    def selected_and_combine(klen):
        kpos = lax.broadcasted_iota(jnp.int32, (1, klen), 1)
        in_sel = jnp.dot(sel_rows, expand_ref[:, :klen], preferred_element_type=F32)
        sel_bias = _mask_bias((in_sel > 0.5) & (kpos <= qpos)).astype(BF16)
        ks = ks_ref[0, :klen, :]
        vs = _with_ones(vs_ref[0, :klen, :])
        for r, q in enumerate(heads):
            gr = gates[:, 3 * r:3 * r + 3]
            out = (gr[:, 0:1] * o_cmp[r] + gr[:, 1:2] * _attend(q, ks, vs, sel_bias, scale)
                   + gr[:, 2:3] * o_win[r])
            o_ref[0, :, r * HEAD_DIM:(r + 1) * HEAD_DIM] = out.astype(o_ref.dtype)

    _for_causal_extent(start + NSA_QBLOCK, seq, selected_and_combine)


def _nsa_mixer(ra, rb, pc, pd, lay, phi_pe, phi_w1, phi_w2):
    bsz, seq, _ = ra.shape
    hpg = lay.n_bheads // B_KV_GROUPS
    gq = hpg * HEAD_DIM
    assert seq % KEY_TILE == 0 and seq >= B_WIN + NSA_QBLOCK and KEY_TILE % NSA_QBLOCK == 0
    kcmp, vcmp = _nsa_compress(rb, pd, lay, phi_pe, phi_w1, phi_w2)
    n_cmp = seq // B_CMP_STRIDE
    n_blk = seq // B_SEL_LEN
    n_sel = min(B_SEL_N, n_blk)
    assert n_blk <= LANES
    c_lo = np.arange(n_cmp)[:, None] * B_CMP_STRIDE
    b_lo = np.arange(LANES)[None, :] * B_SEL_LEN
    overlap = ((c_lo < b_lo + B_SEL_LEN) & (c_lo + B_CMP_LEN - 1 >= b_lo) & (np.arange(LANES)[None, :] < n_blk))
    expand = (np.arange(seq)[None, :] // B_SEL_LEN) == np.arange(LANES)[:, None]
    qb = lambda width, off: pl.BlockSpec((1, NSA_QBLOCK, width), lambda b, g, i: (b, i, off // width + g))
    kb = lambda off: pl.BlockSpec((1, seq, HEAD_DIM), lambda b, g, i: (b, 0, off // HEAD_DIM + g))
    cmap = lambda b, g, i: (b, g, 0, 0)
    const = lambda b, g, i: (0, 0)
    return pl.pallas_call(
        functools.partial(_nsa_kernel, n_sel=n_sel, heads_per_group=hpg),
        grid=(bsz, B_KV_GROUPS, seq // NSA_QBLOCK),
        in_specs=[qb(gq, lay.b_q), qb(LANES, lay.b_g),
                  pl.BlockSpec((1, 1, n_cmp, HEAD_DIM), cmap),
                  pl.BlockSpec((1, 1, n_cmp, HEAD_DIM), cmap),
                  kb(lay.b_ks), kb(lay.b_vs), kb(lay.b_kw), kb(lay.b_vw),
                  pl.BlockSpec((n_cmp, LANES), const),
                  pl.BlockSpec((LANES, seq), const)],
        out_specs=pl.BlockSpec((1, NSA_QBLOCK, gq), lambda b, g, i: (b, i, g)),
        out_shape=jax.ShapeDtypeStruct((bsz, seq, lay.qw), BF16),
        compiler_params=_params("parallel", "parallel", "arbitrary"),
        name="nsa_mixer",
    )(ra, pd, kcmp, vcmp, ra, pc, ra, pc, jnp.asarray(overlap, BF16), jnp.asarray(expand, BF16))


S5_TIME_BLOCK = 512
S5_UNROLL = 8


def _s5_scan_kernel(u_ref, bcat_ref, ccat_ref, a_ref, d_ref, z_ref, xs_ref, state_ref, bt_ref, tb_ref):
    bsz, tt, lanes = u_ref.shape
    ns = a_ref.shape[-1] // 2

    @pl.when(pl.program_id(1) == 0)
    def _():
        state_ref[...] = jnp.zeros_like(state_ref)

    bt_ref[...] = u_ref[...].reshape(bsz * tt, lanes)

    def to_time_major(t, carry):
        tb_ref[pl.ds(pl.multiple_of(t * bsz, bsz), bsz), :] = bt_ref[pl.ds(t, bsz, stride=tt), :]
        return carry

    lax.fori_loop(0, tt, to_time_major, 0, unroll=S5_UNROLL)
    u = tb_ref[...]
    xs_ref[...] = jnp.dot(u.astype(BF16), bcat_ref[0], preferred_element_type=F32)
    a_re = jnp.broadcast_to(a_ref[0, :, :ns], (bsz, ns))
    a_im = jnp.broadcast_to(a_ref[0, :, ns:], (bsz, ns))

    def step(t, carry):
        s_re, s_im = carry
        rows = pl.ds(pl.multiple_of(t * bsz, bsz), bsz)
        n_re = a_re * s_re - a_im * s_im + xs_ref[rows, :ns]
        n_im = a_re * s_im + a_im * s_re + xs_ref[rows, ns:]
        xs_ref[rows, :ns] = n_re
        xs_ref[rows, ns:] = n_im
        return n_re, n_im

    s_re, s_im = lax.fori_loop(0, tt, step, (state_ref[:, :ns], state_ref[:, ns:]), unroll=S5_UNROLL)
    state_ref[:, :ns] = s_re
    state_ref[:, ns:] = s_im
    y = jnp.dot(xs_ref[...].astype(BF16), ccat_ref[0], preferred_element_type=F32) + d_ref[0] * u
    tb_ref[...] = _gelu_tanh(y)
    for b in range(bsz):
        z_ref[b] = tb_ref[pl.ds(b, tt, stride=bsz), :]


def _s5_glu_kernel(za_ref, zt_ref, w_ref, b_ref, o_ref):
    zz = jnp.dot(za_ref[...].astype(BF16), w_ref[...], preferred_element_type=F32) + b_ref[...]
    o_ref[...] = (zt_ref[...] * jax.nn.sigmoid(zz)).astype(o_ref.dtype)


def _s5_mixer(pd, u_off, width, lam_re, lam_im, log_dt, b_re, b_im, c_re, c_im, d, glu_w, glu_b):
    bsz, seq, _ = pd.shape
    assert u_off % LANES == 0
    gpc = LANES // C_GROUP
    n_col = width // LANES
    ns = gpc * C_STATE
    tt = min(S5_TIME_BLOCK, seq)
    assert width % LANES == 0 and seq % tt == 0 and bsz % 8 == 0
    dt = jnp.exp(log_dt)[:, None]
    mag = jnp.exp(lam_re * dt)
    a_re, a_im = mag * jnp.cos(lam_im * dt), mag * jnp.sin(lam_im * dt)
    den = lam_re * lam_re + lam_im * lam_im
    co_re = ((a_re - 1.0) * lam_re + a_im * lam_im) / den
    co_im = (a_im * lam_re - (a_re - 1.0) * lam_im) / den
    bb_re = co_re[..., None] * b_re - co_im[..., None] * b_im
    bb_im = co_re[..., None] * b_im + co_im[..., None] * b_re
    eye = jnp.eye(gpc, dtype=F32)

    def block_diag(m):
        r, c = m.shape[-2:]
        return jnp.einsum('ngrc,gk->ngrkc', m, eye).reshape(n_col, gpc * r, gpc * c)

    to_in = lambda m: block_diag(m.reshape(n_col, gpc, C_STATE, C_GROUP).transpose(0, 1, 3, 2))
    to_out = lambda m: block_diag(m.reshape(n_col, gpc, C_GROUP, C_STATE).transpose(0, 1, 3, 2))
    bcat = jnp.concatenate([to_in(bb_re), to_in(bb_im)], axis=-1).astype(BF16)
    ccat = jnp.concatenate([to_out(c_re), to_out(-c_im)], axis=-2).astype(BF16)
    acat = jnp.concatenate([a_re.reshape(n_col, 1, ns), a_im.reshape(n_col, 1, ns)], axis=-1)
    z = pl.pallas_call(
        _s5_scan_kernel,
        grid=(n_col, seq // tt),
        in_specs=[pl.BlockSpec((bsz, tt, LANES), lambda c, t: (0, t, u_off // LANES + c)),
                  pl.BlockSpec((1, LANES, 2 * ns), lambda c, t: (c, 0, 0)),
                  pl.BlockSpec((1, 2 * ns, LANES), lambda c, t: (c, 0, 0)),
                  pl.BlockSpec((1, 1, 2 * ns), lambda c, t: (c, 0, 0)),
                  pl.BlockSpec((1, 1, LANES), lambda c, t: (c, 0, 0))],
        out_specs=pl.BlockSpec((bsz, tt, LANES), lambda c, t: (0, t, c)),
        out_shape=jax.ShapeDtypeStruct((bsz, seq, width), F32),
        scratch_shapes=[pltpu.VMEM((tt * bsz, 2 * ns), F32), pltpu.VMEM((bsz, 2 * ns), F32),
                        pltpu.VMEM((bsz * tt, LANES), F32), pltpu.VMEM((tt * bsz, LANES), F32)],
        compiler_params=_params("parallel", "arbitrary"),
        name="s5_scan",
    )(pd, bcat, ccat, acat, d.reshape(n_col, 1, LANES))
    z = z.reshape(bsz * seq, width)
    tm, tn = min(ROW_TILE, seq * bsz), min(512, width)
    return pl.pallas_call(
        _s5_glu_kernel,
        grid=(seq * bsz // tm, width // tn),
        in_specs=[pl.BlockSpec((tm, width), lambda i, j: (i, 0)),
                  pl.BlockSpec((tm, tn), lambda i, j: (i, j)),
                  pl.BlockSpec((width, tn), lambda i, j: (0, j)),
                  pl.BlockSpec((1, tn), lambda i, j: (0, j))],
        out_specs=pl.BlockSpec((tm, tn), lambda i, j: (i, j)),
        out_shape=jax.ShapeDtypeStruct((seq * bsz, width), BF16),
        compiler_params=_params("parallel", "parallel"),
        name="s5_glu",
    )(z, z, glu_w.astype(BF16), glu_b.reshape(1, width))


def _mixing_sublayer(hb, bsz, seq, rope, lay, w_in, phi_pe, phi_w1, phi_w2, lam_re, lam_im, log_dt,
                     b_re, b_im, c_re, c_im, s5_d, glu_w, glu_b, w_out):
    w_ra, w_rb, w_pc, w_pd = lay.weights(w_in)
    view = lambda t: t.reshape(bsz, seq, t.shape[-1])
    ra = view(_matmul(hb, w_ra, lay.ra_tile, BF16, rope))
    rb = view(_matmul(hb, w_rb, w_rb.shape[1], F32, rope))
    pc = view(_matmul(hb, w_pc, w_pc.shape[1], BF16))
    pd = view(_matmul(hb, w_pd, lay.pd_tile, F32))
    o_a = _dsa_mixer(ra, pc, pd, lay)
    o_b = _nsa_mixer(ra, rb, pc, pd, lay, phi_pe, phi_w1, phi_w2)
    o_c = _s5_mixer(pd, lay.c_u, lay.cw, lam_re, lam_im, log_dt, b_re, b_im, c_re, c_im, s5_d, glu_w, glu_b)
    return _out_proj(o_a.reshape(bsz * seq, -1), o_b.reshape(bsz * seq, -1), o_c, w_out.astype(BF16))


def _split_bf16(x):
    hi = x.astype(BF16)
    return hi, (x - hi.astype(F32)).astype(BF16)


def _route_rows(x, wh_ref, wl_ref, bias_ref, eid_ref, gate_ref):
    x_hi, x_lo = _split_bf16(x)
    wh = wh_ref[...]
    logits = (jnp.dot(x_hi, wh, preferred_element_type=F32) + jnp.dot(x_hi, wl_ref[...], preferred_element_type=F32)
              + jnp.dot(x_lo, wh, preferred_element_type=F32))
    aff = jax.nn.sigmoid(logits)
    lane_i = lax.broadcasted_iota(jnp.int32, (1, LANES), 1)
    lane = lane_i.astype(F32)
    grp = (lane_i // EXPERTS_PER_GROUP).astype(F32)
    sel = jnp.where(lane_i < N_EXPERTS, aff + bias_ref[...], -jnp.inf)

    def top2(v):
        m1 = jnp.max(v, axis=-1, keepdims=True)
        i1 = jnp.min(jnp.where(v == m1, lane, float(LANES)), axis=-1, keepdims=True)
        v2 = jnp.where(lane == i1, -jnp.inf, v)
        m2 = jnp.max(v2, axis=-1, keepdims=True)
        i2 = jnp.min(jnp.where(v2 == m2, lane, float(LANES)), axis=-1, keepdims=True)
        return m1, i1, m2, i2

    best_score = best_grp = None
    for g in range(N_EXPERT_GROUPS):
        m1, _, m2, _ = top2(jnp.where(grp == float(g), sel, -jnp.inf))
        score = m1 + m2
        if g == 0:
            best_score, best_grp = score, jnp.zeros_like(score)
        else:
            better = score > best_score
            best_score = jnp.where(better, score, best_score)
            best_grp = jnp.where(better, float(g), best_grp)
    _, i1, _, i2 = top2(jnp.where(grp == best_grp, sel, -jnp.inf))
    w1 = jnp.sum(jnp.where(lane == i1, aff, 0.0), axis=-1, keepdims=True)
    w2 = jnp.sum(jnp.where(lane == i2, aff, 0.0), axis=-1, keepdims=True)
    tot = w1 + w2
    eid_ref[...] = jnp.where(lane_i == 0, i1, jnp.where(lane_i == 1, i2, 0.0)).astype(jnp.int32)
    gate_ref[...] = jnp.where(lane_i == 0, w1 / tot, jnp.where(lane_i == 1, w2 / tot, 0.0))


def _moe(xt, eid, layer, w_gate, w_up, w_down):
    n_tok, dm = xt.shape
    n_asg = n_tok * TOP_K
    cap = -(-n_asg // MOE_BLOCK) * MOE_BLOCK + N_EXPERTS * MOE_BLOCK
    onehot = (eid.reshape(n_asg, 1) == jnp.arange(N_EXPERTS, dtype=jnp.int32)[None, :]).astype(jnp.int32)
    seen = jnp.cumsum(onehot, axis=0)
    counts = seen[-1]
    padded = (counts + MOE_BLOCK - 1) // MOE_BLOCK * MOE_BLOCK
    pend = jnp.cumsum(padded)
    pstart = pend - padded
    dest = jnp.sum(onehot * (seen - 1 + pstart[None, :]), axis=1).astype(jnp.int32)
    tok_flat = jnp.arange(n_asg, dtype=jnp.int32) // TOP_K
    slot_tok = (jnp.arange(cap, dtype=jnp.int32) % n_tok).at[dest].set(tok_flat)
    n_blocks = cap // MOE_BLOCK
    blk_start = jnp.arange(n_blocks) * MOE_BLOCK
    blk_exp = jnp.minimum(jnp.sum(pend[None, :] <= blk_start[:, None], axis=1), N_EXPERTS - 1).astype(jnp.int32)
    blk_first = jnp.concatenate([jnp.ones((1,), jnp.int32), (blk_exp[1:] != blk_exp[:-1]).astype(jnp.int32)])
    n_act = (pend[-1] // MOE_BLOCK).astype(jnp.int32).reshape(1)
    y = _moe_experts(xt[slot_tok], blk_exp, blk_first, n_act, layer, w_gate, w_up, w_down)
    pos = dest.reshape(n_tok, TOP_K)
    return y[pos[:, 0]], y[pos[:, 1]]


def kernel(x, positions, ln_in_g, ln_in_b, w_in, nsa_phi_pe, nsa_phi_w1, nsa_phi_w2, s5_lam_re, s5_lam_im, s5_log_dt, s5_b_re, s5_b_im, s5_c_re, s5_c_im, s5_d, s5_glu_w, s5_glu_b, w_out, ln_mix_g, ln_mix_b, router_w, router_bias, moe_w_gate, moe_w_up, moe_w_down, ln_ffn_g, ln_ffn_b):
    depth = w_in.shape[0]
    bsz, seq, d_model = x.shape
    alpha = (2 * depth) ** 0.25
    lay = _Packed(d_model)
    rope = _rope_tables(positions)
    h, hb = _layer_norm(x.reshape(bsz * seq, d_model), ln_in_g, ln_in_b)
    for l in range(depth):
        mix = _mixing_sublayer(hb, bsz, seq, rope, lay, w_in[l], nsa_phi_pe[l], nsa_phi_w1[l], nsa_phi_w2[l],
                               s5_lam_re[l], s5_lam_im[l], s5_log_dt[l], s5_b_re[l], s5_b_im[l],
                               s5_c_re[l], s5_c_im[l], s5_d[l], s5_glu_w[l], s5_glu_b[l], w_out[l])
        h, hb, eid, gate = _layer_norm_and_route(h, ln_mix_g[l], ln_mix_b[l], mix, alpha, router_w, router_bias)
        y0, y1 = _moe(h, eid, l, moe_w_gate, moe_w_up, moe_w_down)
        h, hb = _layer_norm(h, ln_ffn_g[l], ln_ffn_b[l], gated_pair=(y0, y1, gate), alpha=alpha)
    return h.reshape(bsz, seq, d_model)
```
